```python
import math
import jax, jax.numpy as jnp
from jax import lax
import numpy as np

D_MODEL = 2048
BATCH = 4
SEQ = 8192
DEPTH = 2

MEM_LEN = 256
HEAD_DIM = 64
A_WIDTH = 3 * D_MODEL // 8
A_HEADS = A_WIDTH // HEAD_DIM
A_PATTERNS = ((128, 1), (512, 4), (2048, 16))
B_WIDTH = D_MODEL // 4
B_HEADS = B_WIDTH // HEAD_DIM
BLOCK_Q = 128
C_HEADS = 4
C_VW = 3 * D_MODEL // 8
C_DV = C_VW // C_HEADS
C_DK = C_DV // 2
C_KW = C_HEADS * C_DK
C_GATE_RANK = 16
C_GATE_TAU = 16.0
C_CHUNK = 64
MIX_WIDTH = A_WIDTH + B_WIDTH + C_VW
PROJ_SPLITS = (A_WIDTH, A_WIDTH, A_WIDTH, B_WIDTH, B_WIDTH, B_WIDTH, B_HEADS, C_KW, C_KW, C_VW, C_VW, C_GATE_RANK)
PROJ_WIDTH = sum(PROJ_SPLITS)
CROSS_HEADS = 4
CROSS_DH = 128
CROSS_WIDTH = CROSS_HEADS * CROSS_DH
D_FF = ((8 * D_MODEL // 3 + 127) // 128) * 128
CONV_W = 3
REL_BUCKETS = 32
REL_MAX_DIST = 2048
EPS = 1e-6

kernel_name = 'hybrid_dilated_fox_gla_block'


def rms_norm(x, g):
    xf = x.astype(jnp.float32)
    y = xf * lax.rsqrt(jnp.mean(xf * xf, axis=-1, keepdims=True) + EPS)
    return (y * g.astype(jnp.float32)).astype(x.dtype)


def t5_bucket(dist):
    max_exact = REL_BUCKETS // 2
    d = np.maximum(dist, 1).astype(np.float32)
    large = max_exact + (np.log(d / max_exact) / math.log(REL_MAX_DIST / max_exact)
                         * (REL_BUCKETS - max_exact)).astype(np.int32)
    return np.where(dist < max_exact, dist, np.minimum(large, REL_BUCKETS - 1)).astype(np.int32)


def dilated_pattern(q, k, v, rel_table, window, dilation):
    B, S, H, Dh = q.shape
    span = window // dilation
    L = S // dilation
    nb = -(-L // span)
    Lp = nb * span

    def classes(t):
        t = t.reshape(B, L, dilation, H, Dh).transpose(0, 3, 2, 1, 4)
        return jnp.pad(t, ((0, 0), (0, 0), (0, 0), (0, Lp - L), (0, 0)))

    def band(t):
        t = jnp.pad(classes(t), ((0, 0), (0, 0), (0, 0), (span, 0), (0, 0)))
        t = t.reshape(B, H, dilation, nb + 1, span, Dh)
        return jnp.concatenate([t[:, :, :, :-1], t[:, :, :, 1:]], axis=4)

    qb = classes(q).reshape(B, H, dilation, nb, span, Dh)
    kb, vb = band(k), band(v)
    i = np.arange(span)[:, None]
    j = np.arange(2 * span)[None, :]
    step = i + span - j
    valid = (step >= 0) & (step <= span) & (np.arange(nb)[:, None, None] * span + j - span >= 0)
    bias = rel_table[t5_bucket(np.clip(step, 0, span) * dilation)]
    bias = jnp.transpose(bias, (2, 0, 1)).astype(jnp.float32)
    logits = jnp.einsum('bhrnid,bhrnjd->bhrnij', qb, kb,
                        preferred_element_type=jnp.float32) * HEAD_DIM ** -0.5
    logits = jnp.where(valid, logits + bias[None, :, None, None], -jnp.inf)
    m = jnp.max(logits, axis=-1, keepdims=True)
    p = jnp.exp(logits - m)
    s = jnp.sum(p, axis=-1, keepdims=True)
    o = jnp.einsum('bhrnij,bhrnjd->bhrnid', p.astype(v.dtype), vb,
                   preferred_element_type=jnp.float32) / s
    lse = (m + jnp.log(s))[..., 0]
    o = o.reshape(B, H, dilation, Lp, Dh)[:, :, :, :L].transpose(0, 3, 2, 1, 4).reshape(B, S, H, Dh)
    lse = lse.reshape(B, H, dilation, Lp)[:, :, :, :L].transpose(0, 3, 2, 1).reshape(B, S, H)
    return o, lse


def dilated_mixture_attention(q, k, v, rel_table):
    outs, lses = [], []
    for window, dilation in A_PATTERNS:
        o, lse = dilated_pattern(q, k, v, rel_table, window, dilation)
        outs.append(o)
        lses.append(lse)
    w = jax.nn.softmax(jnp.stack(lses, axis=0), axis=0)
    return jnp.sum(w[..., None] * jnp.stack(outs, axis=0), axis=0)


def forgetting_attention(q, k, v, f_logit):
    B, S, H, Dh = q.shape
    c = jnp.cumsum(jax.nn.log_sigmoid(f_logit.astype(jnp.float32)), axis=1)
    cT = c.transpose(0, 2, 1)
    nq = S // BLOCK_Q
    qb = q.reshape(B, nq, BLOCK_Q, H, Dh).transpose(1, 0, 2, 3, 4)
    cb = cT.reshape(B, H, nq, BLOCK_Q).transpose(2, 0, 1, 3)
    kpos = jnp.arange(S)

    def block(args):
        n, qn, cn = args
        logits = jnp.einsum('bihd,bjhd->bhij', qn, k,
                            preferred_element_type=jnp.float32) * HEAD_DIM ** -0.5
        logits = logits + (cn[..., :, None] - cT[..., None, :])
        qpos = n * BLOCK_Q + jnp.arange(BLOCK_Q)
        logits = jnp.where(kpos[None, :] <= qpos[:, None], logits, -jnp.inf)
        p = jax.nn.softmax(logits, axis=-1)
        return jnp.einsum('bhij,bjhd->bihd', p.astype(v.dtype), v)

    ob = lax.map(block, (jnp.arange(nq), qb, cb))
    return ob.transpose(1, 0, 2, 3, 4).reshape(B, S, H, Dh)


def gla_chunked(q, k, v, log_alpha):
    B, S, H, DK = q.shape
    nc = S // C_CHUNK

    def chunks(t):
        return t.astype(jnp.float32).reshape(B, nc, C_CHUNK, H, t.shape[-1]).transpose(1, 0, 3, 2, 4)

    qc, kc, vc, gc = chunks(q * DK ** -0.5), chunks(k), chunks(v), chunks(log_alpha)
    causal = np.tril(np.ones((C_CHUNK, C_CHUNK), dtype=bool))[:, :, None]

    def step(state, inp):
        qn, kn, vn, gn = inp
        b = jnp.cumsum(gn, axis=2)
        o_inter = jnp.einsum('bhik,bhkv->bhiv', qn * jnp.exp(b), state)
        diff = b[:, :, :, None, :] - b[:, :, None, :, :]
        decay = jnp.exp(jnp.where(causal, diff, -jnp.inf))
        a = jnp.einsum('bhik,bhjk,bhijk->bhij', qn, kn, decay)
        o_intra = jnp.einsum('bhij,bhjv->bhiv', a, vn)
        b_last = b[:, :, -1:, :]
        state = state * jnp.exp(b_last[:, :, 0, :])[..., None] + \
            jnp.einsum('bhjk,bhjv->bhkv', kn * jnp.exp(b_last - b), vn)
        return state, o_inter + o_intra

    state0 = jnp.zeros((B, H, DK, v.shape[-1]), jnp.float32)
    _, o = lax.scan(step, state0, (qc, kc, vc, gc))
    return o.transpose(1, 0, 3, 2, 4).reshape(B, S, H, v.shape[-1])


def hybrid_mixer(h, w_in, f_bias, rel_table, c_gate_w2, c_gate_b, c_norm, w_out):
    B, S, _ = h.shape
    proj = h @ w_in
    qa, ka, va, qb, kb, vb, fl, qc, kc, vc, rc, gl = jnp.split(
        proj, np.cumsum(PROJ_SPLITS)[:-1].tolist(), axis=-1)

    def heads(t, n):
        return t.reshape(B, S, n, -1)

    o_a = dilated_mixture_attention(heads(qa, A_HEADS), heads(ka, A_HEADS), heads(va, A_HEADS), rel_table)
    o_b = forgetting_attention(heads(qb, B_HEADS), heads(kb, B_HEADS), heads(vb, B_HEADS), fl + f_bias)
    log_alpha = jax.nn.log_sigmoid((gl @ c_gate_w2 + c_gate_b).astype(jnp.float32)) / C_GATE_TAU
    o_c = gla_chunked(heads(qc, C_HEADS), heads(kc, C_HEADS), heads(vc, C_HEADS), heads(log_alpha, C_HEADS))
    o_c = rms_norm(o_c.astype(h.dtype), c_norm).reshape(B, S, C_VW) * jax.nn.silu(rc)
    mixed = jnp.concatenate([o_a.reshape(B, S, A_WIDTH).astype(h.dtype),
                             o_b.reshape(B, S, B_WIDTH).astype(h.dtype),
                             o_c.astype(h.dtype)], axis=-1)
    return mixed @ w_out


def memory_cross_attention(h, mem_n, w_q, w_kv, w_o):
    B, S, _ = h.shape
    M = mem_n.shape[1]
    q = (h @ w_q).reshape(B, S, CROSS_HEADS, CROSS_DH)
    k, v = jnp.split(mem_n @ w_kv, 2, axis=-1)
    k = k.reshape(B, M, CROSS_HEADS, CROSS_DH)
    v = v.reshape(B, M, CROSS_HEADS, CROSS_DH)
    logits = jnp.einsum('bshd,bmhd->bhsm', q, k, preferred_element_type=jnp.float32) * CROSS_DH ** -0.5
    p = jax.nn.softmax(logits, axis=-1)
    o = jnp.einsum('bhsm,bmhd->bshd', p.astype(v.dtype), v).reshape(B, S, CROSS_WIDTH)
    return o @ w_o


def conv_ffn(h, w_up, conv_w, conv_b, w_down):
    S = h.shape[1]
    val, gate = jnp.split(h @ w_up, 2, axis=-1)
    gp = jnp.pad(gate, ((0, 0), (CONV_W - 1, 0), (0, 0)))
    g = conv_b
    for i in range(CONV_W):
        g = g + conv_w[i] * gp[:, i:i + S]
    return (jax.nn.silu(g) * val) @ w_down


def setup_inputs(seed: int = 0) -> dict:
    key = jax.random.key(seed)
    ks = jax.random.split(key, 21)

    def nrm(k, shape, scale):
        return jax.random.normal(k, shape, jnp.float32) * scale

    def gain(k, shape):
        return 1.0 + 0.05 * jax.random.normal(k, shape, jnp.float32)

    return {
        'x': nrm(ks[0], (BATCH, SEQ, D_MODEL), 1.0),
        'mem': nrm(ks[1], (BATCH, MEM_LEN, D_MODEL), 1.0),
        'rel_table': nrm(ks[2], (REL_BUCKETS, A_HEADS), 0.5),
        'mem_norm': gain(ks[3], (D_MODEL,)),
        'norm_final': gain(ks[4], (D_MODEL,)),
        'norm_mix': gain(ks[5], (DEPTH, D_MODEL)),
        'w_in': nrm(ks[6], (DEPTH, D_MODEL, PROJ_WIDTH), D_MODEL ** -0.5),
        'f_bias': 3.0 + nrm(ks[7], (DEPTH, B_HEADS), 0.1),
        'c_gate_w2': nrm(ks[8], (DEPTH, C_GATE_RANK, C_KW), C_GATE_RANK ** -0.5),
        'c_gate_b': nrm(ks[9], (DEPTH, C_KW), 0.1),
        'c_norm': gain(ks[10], (DEPTH, C_DV)),
        'w_out': nrm(ks[11], (DEPTH, MIX_WIDTH, D_MODEL), MIX_WIDTH ** -0.5),
        'norm_cross': gain(ks[12], (DEPTH, D_MODEL)),
        'w_cq': nrm(ks[13], (DEPTH, D_MODEL, CROSS_WIDTH), D_MODEL ** -0.5),
        'w_ckv': nrm(ks[14], (DEPTH, D_MODEL, 2 * CROSS_WIDTH), D_MODEL ** -0.5),
        'w_co': nrm(ks[15], (DEPTH, CROSS_WIDTH, D_MODEL), CROSS_WIDTH ** -0.5),
        'norm_ffn': gain(ks[16], (DEPTH, D_MODEL)),
        'w_up': nrm(ks[17], (DEPTH, D_MODEL, 2 * D_FF), D_MODEL ** -0.5),
        'conv_w': nrm(ks[18], (DEPTH, CONV_W, D_FF), CONV_W ** -0.5),
        'conv_b': nrm(ks[19], (DEPTH, D_FF), 0.02),
        'w_down': nrm(ks[20], (DEPTH, D_FF, D_MODEL), D_FF ** -0.5),
    }


def reference(x, mem, rel_table, mem_norm, norm_final, norm_mix, w_in, f_bias, c_gate_w2,
              c_gate_b, c_norm, w_out, norm_cross, w_cq, w_ckv, w_co, norm_ffn, w_up,
              conv_w, conv_b, w_down):
    mem_n = rms_norm(mem, mem_norm)
    for l in range(DEPTH):
        x = x + hybrid_mixer(rms_norm(x, norm_mix[l]), w_in[l], f_bias[l], rel_table,
                             c_gate_w2[l], c_gate_b[l], c_norm[l], w_out[l])
        x = x + memory_cross_attention(rms_norm(x, norm_cross[l]), mem_n, w_cq[l], w_ckv[l], w_co[l])
        x = x + conv_ffn(rms_norm(x, norm_ffn[l]), w_up[l], conv_w[l], conv_b[l], w_down[l])
    return rms_norm(x, norm_final)
```

```python
import functools
import math

import numpy as np
import jax
import jax.numpy as jnp
from jax import lax
from jax.experimental import pallas as pl
from jax.experimental.pallas import tpu as pltpu

F32 = jnp.float32
BF16 = jnp.bfloat16

V7X_LANES = 128
V7X_VMEM_LIMIT_BYTES = 56 * 1024 * 1024

D_MODEL = 2048
HEAD_DIM = 64
A_WIDTH = 3 * D_MODEL // 8
A_HEADS = A_WIDTH // HEAD_DIM
A_PATTERNS = ((128, 1), (512, 4), (2048, 16))
A_SPAN = 128
B_WIDTH = D_MODEL // 4
B_HEADS = B_WIDTH // HEAD_DIM
C_HEADS = 4
C_VW = 3 * D_MODEL // 8
C_DV = C_VW // C_HEADS
C_DK = C_DV // 2
C_KW = C_HEADS * C_DK
C_GATE_RANK = 16
C_GATE_TAU = 16.0
C_CHUNK = 64
C_SUB = 16
C_DK_PAD = 128
C_DV_PAD = 256
CROSS_HEADS = 4
CROSS_DH = 128
CROSS_WIDTH = CROSS_HEADS * CROSS_DH
D_FF = ((8 * D_MODEL // 3 + 127) // 128) * 128
FF_TILE = 512
D_FF_PAD = ((D_FF + FF_TILE - 1) // FF_TILE) * FF_TILE
CONV_W = 3
REL_BUCKETS = 32
REL_MAX_DIST = 2048
EPS = 1e-6
NEG = -1e30

COL_QA = 0
COL_KA = COL_QA + A_WIDTH
COL_VA = COL_KA + A_WIDTH
COL_QB = COL_VA + A_WIDTH
COL_KB = COL_QB + B_WIDTH
COL_VB = COL_KB + B_WIDTH
COL_QC = COL_VB + B_WIDTH
COL_KC = COL_QC + C_HEADS * C_DK_PAD
COL_VC = COL_KC + C_HEADS * C_DK_PAD
COL_RC = COL_VC + C_HEADS * C_DV_PAD
PROJ_PAD = COL_RC + C_HEADS * C_DV_PAD
SMALL_FL = 0
SMALL_GL = B_HEADS


def _cparams(sem):
    return pltpu.CompilerParams(dimension_semantics=sem,
                                vmem_limit_bytes=V7X_VMEM_LIMIT_BYTES)


def _rms_rows(x, g):
    ms = jnp.mean(x * x, axis=-1, keepdims=True)
    return x * lax.rsqrt(ms + EPS) * g


def _split3(x):
    hi = x.astype(BF16)
    r1 = x - hi.astype(F32)
    mid = r1.astype(BF16)
    lo = (r1 - mid.astype(F32)).astype(BF16)
    return hi, mid, lo


def _log_sigmoid(x):
    return jnp.minimum(x, 0.0) - jnp.log(1.0 + jnp.exp(-jnp.abs(x)))


def _tril_ones(n, dtype):
    r = lax.broadcasted_iota(jnp.int32, (n, n), 0)
    c = lax.broadcasted_iota(jnp.int32, (n, n), 1)
    return jnp.where(r >= c, 1.0, 0.0).astype(dtype)


NORM_ROWS = 256


def _norm_to_scratch(x_ref, g_ref, h_scr):
    def body(r, c):
        sl = pl.ds(pl.multiple_of(r * NORM_ROWS, NORM_ROWS), NORM_ROWS)
        h_scr[sl, :] = _rms_rows(x_ref[sl, :], g_ref[...]).astype(BF16)
        return c
    lax.fori_loop(0, x_ref.shape[0] // NORM_ROWS, body, 0)


def _norm_matmul_kernel(x_ref, g_ref, w_ref, o_ref, h_scr):
    @pl.when(pl.program_id(1) == 0)
    def _():
        _norm_to_scratch(x_ref, g_ref, h_scr)
    o_ref[...] = jnp.dot(h_scr[...], w_ref[...],
                         preferred_element_type=F32).astype(o_ref.dtype)


def _norm_matmul_small_kernel(x_ref, g_ref, w_ref, ws_ref, o_ref, os_ref, h_scr):
    @pl.when(pl.program_id(1) == 0)
    def _():
        _norm_to_scratch(x_ref, g_ref, h_scr)
        os_ref[...] = jnp.dot(h_scr[...], ws_ref[...], preferred_element_type=F32)
    o_ref[...] = jnp.dot(h_scr[...], w_ref[...],
                         preferred_element_type=F32).astype(o_ref.dtype)


def norm_matmul(x2d, gain, w, w_small=None, *, tm, tn, name):
    m, k = x2d.shape
    n = w.shape[1]
    grid = (m // tm, n // tn)
    in_specs = [pl.BlockSpec((tm, k), lambda i, j: (i, 0)),
                pl.BlockSpec((1, k), lambda i, j: (0, 0)),
                pl.BlockSpec((k, tn), lambda i, j: (0, j))]
    out_specs = pl.BlockSpec((tm, tn), lambda i, j: (i, j))
    out_shape = jax.ShapeDtypeStruct((m, n), BF16)
    args = [x2d, gain.reshape(1, k), w]
    kern = _norm_matmul_kernel
    if w_small is not None:
        ns = w_small.shape[1]
        in_specs.append(pl.BlockSpec((k, ns), lambda i, j: (0, 0)))
        out_specs = [out_specs, pl.BlockSpec((tm, ns), lambda i, j: (i, 0))]
        out_shape = [out_shape, jax.ShapeDtypeStruct((m, ns), F32)]
        args.append(w_small)
        kern = _norm_matmul_small_kernel
    return pl.pallas_call(
        kern, grid=grid, in_specs=in_specs, out_specs=out_specs, out_shape=out_shape,
        scratch_shapes=[pltpu.VMEM((tm, k), BF16)],
        compiler_params=_cparams(("parallel", "arbitrary")), name=name)(*args)


def _t5_bucket(dist):
    max_exact = REL_BUCKETS // 2
    d = np.maximum(dist, 1).astype(np.float32)
    large = max_exact + (np.log(d / max_exact) / math.log(REL_MAX_DIST / max_exact)
                         * (REL_BUCKETS - max_exact)).astype(np.int32)
    return np.where(dist < max_exact, dist, np.minimum(large, REL_BUCKETS - 1)).astype(np.int32)


def _dilated_bias(rel_table, dilation):
    span = A_SPAN
    i = np.arange(span)[:, None]
    j = np.arange(2 * span)[None, :]
    step = i + span - j
    valid = (step >= 0) & (step <= span)
    bucket = _t5_bucket(np.clip(step, 0, span) * dilation)
    b = jnp.transpose(rel_table[bucket], (2, 0, 1)).astype(F32)
    later = jnp.where(valid[None], b, NEG)
    first = jnp.where((valid & (j >= span))[None], b, NEG)
    return jnp.stack([first, later], axis=0)


def _dilated_kernel(*refs, first, last):
    if first:
        q_ref, kp_ref, kc_ref, vp_ref, vc_ref, bias_ref = refs[:6]
        outs = refs[6:]
    else:
        q_ref, kp_ref, kc_ref, vp_ref, vc_ref, bias_ref, op_ref, lp_ref = refs[:8]
        outs = refs[8:]
    o_ref = outs[0]
    sel = jnp.where(pl.program_id(2) == 0, 0, 1)
    span = A_SPAN
    lane_q = lax.broadcasted_iota(jnp.int32, (span, V7X_LANES), 1)
    lane_k = lax.broadcasted_iota(jnp.int32, (2 * span, V7X_LANES), 1)
    low_q = lane_q < HEAD_DIM
    scale = HEAD_DIM ** -0.5
    qmask = (jnp.where(low_q, scale, 0.0).astype(BF16), jnp.where(low_q, 0.0, scale).astype(BF16))
    vkeep = (jnp.where(lane_k < HEAD_DIM, 1.0, 0.0).astype(BF16),
             jnp.where(lane_k < HEAD_DIM, 0.0, 1.0).astype(BF16))
    for g in range(A_HEADS // 2):
        cs = slice(V7X_LANES * g, V7X_LANES * (g + 1))
        qg = q_ref[:, cs]
        kg = jnp.concatenate([kp_ref[:, cs], kc_ref[:, cs]], axis=0)
        vg = jnp.concatenate([vp_ref[:, cs], vc_ref[:, cs]], axis=0)
        o_pair = None
        l_pair = None
        for e in range(2):
            qe = qg * qmask[e]
            logits = lax.dot_general(qe, kg, (((1,), (1,)), ((), ())),
                                     preferred_element_type=F32) + bias_ref[sel, 2 * g + e]
            m = jnp.max(logits, axis=-1, keepdims=True)
            p = jnp.exp(logits - m)
            ve = vg * vkeep[e] + vkeep[1 - e]
            pv = jnp.dot(p.astype(BF16), ve, preferred_element_type=F32)
            sw = pltpu.roll(pv, HEAD_DIM, axis=1)
            o_e = pv / sw
            l_e = m + jnp.log(sw)
            if e == 0:
                o_pair, l_pair = o_e, l_e
            else:
                o_pair = jnp.where(low_q, o_pair, o_e)
                l_pair = jnp.where(low_q, l_pair, l_e)
        if not first:
            lp = lp_ref[:, cs]
            op = op_ref[:, cs]
            mx = jnp.maximum(lp, l_pair)
            a = jnp.exp(lp - mx)
            b = jnp.exp(l_pair - mx)
            den = a + b
            o_pair = (a * op + b * o_pair) / den
            l_pair = mx + jnp.log(den)
        o_ref[:, cs] = o_pair.astype(o_ref.dtype)
        if not last:
            outs[1][:, cs] = l_pair


def dilated_pattern(proj3, bias, state, *, dilation, first, last, name):
    bsz, s, pw = proj3.shape
    d = dilation
    l = s // d
    nb = l // A_SPAN
    cols = pw // A_WIDTH
    pv = proj3.reshape(bsz, l, d * pw)
    blk = (None, A_SPAN, A_WIDTH)
    cq, ck, cv = COL_QA // A_WIDTH, COL_KA // A_WIDTH, COL_VA // A_WIDTH
    in_specs = [
        pl.BlockSpec(blk, lambda b, r, n: (b, n, r * cols + cq)),
        pl.BlockSpec(blk, lambda b, r, n: (b, jnp.maximum(n - 1, 0), r * cols + ck)),
        pl.BlockSpec(blk, lambda b, r, n: (b, n, r * cols + ck)),
        pl.BlockSpec(blk, lambda b, r, n: (b, jnp.maximum(n - 1, 0), r * cols + cv)),
        pl.BlockSpec(blk, lambda b, r, n: (b, n, r * cols + cv)),
        pl.BlockSpec(bias.shape, lambda b, r, n: (0, 0, 0, 0)),
    ]
    args = [pv, pv, pv, pv, pv, bias]
    st_spec = pl.BlockSpec(blk, lambda b, r, n: (b, n, r))
    if not first:
        in_specs += [st_spec, st_spec]
        args += [state[0].reshape(bsz, l, d * A_WIDTH), state[1].reshape(bsz, l, d * A_WIDTH)]
    if last:
        out_specs = st_spec
        out_shape = jax.ShapeDtypeStruct((bsz, l, d * A_WIDTH), BF16)
    else:
        out_specs = [st_spec, st_spec]
        out_shape = [jax.ShapeDtypeStruct((bsz, l, d * A_WIDTH), F32)] * 2
    res = pl.pallas_call(
        functools.partial(_dilated_kernel, first=first, last=last),
        grid=(bsz, d, nb), in_specs=in_specs, out_specs=out_specs, out_shape=out_shape,
        compiler_params=_cparams(("parallel", "parallel", "arbitrary")), name=name)(*args)
    if last:
        return res.reshape(bsz, s, A_WIDTH)
    return res[0].reshape(bsz, s, A_WIDTH), res[1].reshape(bsz, s, A_WIDTH)


FOX_PREP_ROWS = 512
FOX_TQ = 512
FOX_TK = 512


def _fox_prep_kernel(sm_ref, fb_ref, q0_ref, q1_ref, k0_ref, k1_ref, v0_ref, v1_ref,
                     qa_ref, ka_ref, va_ref, carry_scr):
    t = sm_ref.shape[0]

    @pl.when(pl.program_id(1) == 0)
    def _():
        carry_scr[...] = jnp.zeros_like(carry_scr)

    x = _log_sigmoid(sm_ref[...] + fb_ref[...])
    tri = _tril_ones(t, BF16)
    hi, mid, lo = _split3(x)
    c = (jnp.dot(tri, hi, preferred_element_type=F32)
         + jnp.dot(tri, mid, preferred_element_type=F32)
         + jnp.dot(tri, lo, preferred_element_type=F32)) + carry_scr[0:1, :]
    carry_scr[...] = jnp.broadcast_to(c[t - 1:t, :], carry_scr.shape)
    chi, cmid, clo = [p.astype(F32) for p in _split3(c)]

    lane = lax.broadcasted_iota(jnp.int32, (t, V7X_LANES), 1)
    low = lane < HEAD_DIM
    scale = HEAD_DIM ** -0.5
    q_refs, k_refs, v_refs = (q0_ref, q1_ref), (k0_ref, k1_ref), (v0_ref, v1_ref)
    for h in range(B_HEADS):
        src, pair = divmod(h // 2, 2)
        cs = slice(V7X_LANES * pair, V7X_LANES * (pair + 1))
        qp = q_refs[src][:, cs].astype(F32)
        kp = k_refs[src][:, cs].astype(F32)
        vp = v_refs[src][:, cs].astype(F32)
        if h % 2 == 1:
            qp = pltpu.roll(qp, HEAD_DIM, axis=1)
            kp = pltpu.roll(kp, HEAD_DIM, axis=1)
        c1, c2, c3 = chi[:, h:h + 1], cmid[:, h:h + 1], clo[:, h:h + 1]
        one = jnp.ones((t, 1), F32)
        zero = jnp.zeros((t, 1), F32)
        q_aug, k_aug = zero, zero
        for idx, (qv, kv) in enumerate(((c1, one), (c2, one), (c3, one),
                                        (one, -c1), (one, -c2), (one, -c3))):
            hit = lane == HEAD_DIM + idx
            q_aug = jnp.where(hit, qv, q_aug)
            k_aug = jnp.where(hit, kv, k_aug)
        qa_ref[h] = jnp.where(low, qp * scale, q_aug).astype(BF16)
        ka_ref[h] = jnp.where(low, kp, k_aug).astype(BF16)
        if h % 2 == 0:
            va_ref[h] = jnp.where(low, vp, 1.0).astype(BF16)
        else:
            va_ref[h] = jnp.where(low, 1.0, vp).astype(BF16)


def fox_prep(proj3, small3, f_bias):
    bsz, s, _ = proj3.shape
    t = FOX_PREP_ROWS
    half = B_WIDTH // 2
    fb = jnp.zeros((1, V7X_LANES), F32).at[0, SMALL_FL:SMALL_FL + B_HEADS].set(f_bias)

    def colspec(col):
        return pl.BlockSpec((None, t, half), lambda b, i, c=col // half: (b, i, c))

    aug_spec = pl.BlockSpec((None, B_HEADS, t, V7X_LANES), lambda b, i: (b, 0, i, 0))
    aug_shape = jax.ShapeDtypeStruct((bsz, B_HEADS, s, V7X_LANES), BF16)
    return pl.pallas_call(
        _fox_prep_kernel, grid=(bsz, s // t),
        in_specs=[pl.BlockSpec((None, t, V7X_LANES), lambda b, i: (b, i, 0)),
                  pl.BlockSpec((1, V7X_LANES), lambda b, i: (0, 0)),
                  colspec(COL_QB), colspec(COL_QB + half),
                  colspec(COL_KB), colspec(COL_KB + half),
                  colspec(COL_VB), colspec(COL_VB + half)],
        out_specs=[aug_spec, aug_spec, aug_spec],
        out_shape=[aug_shape, aug_shape, aug_shape],
        scratch_shapes=[pltpu.VMEM((8, V7X_LANES), F32)],
        compiler_params=_cparams(("parallel", "arbitrary")), name="fox_prep",
    )(small3, fb, proj3, proj3, proj3, proj3, proj3, proj3)


def _fox_kernel(q_ref, k_ref, v_ref, o_ref):
    tq, tk = FOX_TQ, FOX_TK
    qi = pl.program_id(2)
    nfull = qi * (tq // tk)
    row = lax.broadcasted_iota(jnp.int32, (tq, tk), 0)
    col = lax.broadcasted_iota(jnp.int32, (tq, tk), 1)
    lane = lax.broadcasted_iota(jnp.int32, (tq, V7X_LANES), 1)
    res = []
    for e in range(2):
        q = q_ref[e]

        def tile(kt, m, acc, mask):
            ks = k_ref[e, pl.ds(pl.multiple_of(kt * tk, tk), tk), :]
            vs = v_ref[e, pl.ds(pl.multiple_of(kt * tk, tk), tk), :]
            s = lax.dot_general(q, ks, (((1,), (1,)), ((), ())), preferred_element_type=F32)
            if mask is not None:
                s = jnp.where(mask, s, NEG)
            m_new = jnp.maximum(m, jnp.max(s, axis=-1, keepdims=True))
            alpha = jnp.exp(m - m_new)
            p = jnp.exp(s - m_new)
            acc = alpha * acc + jnp.dot(p.astype(BF16), vs, preferred_element_type=F32)
            return m_new, acc

        m0 = jnp.full((tq, 1), NEG, F32)
        acc0 = jnp.zeros((tq, V7X_LANES), F32)
        m, acc = lax.fori_loop(0, nfull, lambda kt, c: tile(kt, c[0], c[1], None), (m0, acc0))
        for dd in range(tq // tk):
            m, acc = tile(nfull + dd, m, acc, col + dd * tk <= row)
        res.append(acc / pltpu.roll(acc, HEAD_DIM, axis=1))
    o_ref[...] = jnp.where(lane < HEAD_DIM, res[0], res[1]).astype(o_ref.dtype)


def fox_attention(q_aug, k_aug, v_aug):
    bsz, nh, s, _ = q_aug.shape
    tq = FOX_TQ
    return pl.pallas_call(
        _fox_kernel, grid=(bsz, nh // 2, s // tq),
        in_specs=[pl.BlockSpec((None, 2, tq, V7X_LANES), lambda b, g, i: (b, g, i, 0)),
                  pl.BlockSpec((None, 2, s, V7X_LANES), lambda b, g, i: (b, g, 0, 0)),
                  pl.BlockSpec((None, 2, s, V7X_LANES), lambda b, g, i: (b, g, 0, 0))],
        out_specs=pl.BlockSpec((None, tq, V7X_LANES), lambda b, g, i: (b, i, g)),
        out_shape=jax.ShapeDtypeStruct((bsz, s, B_WIDTH), BF16),
        compiler_params=_cparams(("parallel", "parallel", "arbitrary")), name="fox_attention",
    )(q_aug, k_aug, v_aug)


GLA_ROWS = 256


def _gla_kernel(q_ref, k_ref, v_ref, r_ref, sm_ref, w2_ref, cb_ref, gn_ref, o_ref,
                st_scr, b_scr, k_scr, v_scr):
    ck, sub = C_CHUNK, C_SUB

    @pl.when(pl.program_id(2) == 0)
    def _():
        st_scr[...] = jnp.zeros_like(st_scr)

    tri = _tril_ones(ck, BF16)
    row_sub = lax.broadcasted_iota(jnp.int32, (sub, 1), 0)
    row_ck = lax.broadcasted_iota(jnp.int32, (ck, 1), 0)
    nt = (((1,), (1,)), ((), ()))

    def chunk(c, carry):
        rows = pl.ds(pl.multiple_of(c * ck, ck), ck)
        g = jnp.dot(sm_ref[rows, :].astype(BF16), w2_ref[...],
                    preferred_element_type=F32) + cb_ref[...]
        la = _log_sigmoid(g) / C_GATE_TAU
        hi, mid, lo = _split3(la)
        b = (jnp.dot(tri, hi, preferred_element_type=F32)
             + jnp.dot(tri, mid, preferred_element_type=F32)
             + jnp.dot(tri, lo, preferred_element_type=F32))
        qf = q_ref[rows, :].astype(F32) * (C_DK ** -0.5)
        kf = k_ref[rows, :].astype(F32)
        vb = v_ref[rows, :]
        b_scr[...] = b
        k_scr[...] = kf
        v_scr[...] = vb.astype(F32)
        st = st_scr[...]
        o_inter = lax.dot_general((qf * jnp.exp(b)).astype(BF16), st.astype(BF16), nt,
                                  preferred_element_type=F32)
        outs = []
        for blk in range(ck // sub):
            r0 = blk * sub
            qb = qf[r0:r0 + sub, :]
            bb = b[r0:r0 + sub, :]
            ob = o_inter[r0:r0 + sub, :]
            if blk > 0:
                ref_row = b_scr[r0 - 1:r0, :]
                kd = jnp.where(row_ck < r0, kf * jnp.exp(jnp.minimum(ref_row - b, 0.0)), 0.0)
                a = lax.dot_general((qb * jnp.exp(bb - ref_row)).astype(BF16), kd.astype(BF16), nt,
                                    preferred_element_type=F32)
                ob = ob + jnp.dot(a.astype(BF16), vb, preferred_element_type=F32)
            for j in range(sub):
                bj = b_scr[r0 + j:r0 + j + 1, :]
                kj = k_scr[r0 + j:r0 + j + 1, :]
                vj = v_scr[r0 + j:r0 + j + 1, :]
                x = qb * kj * jnp.exp(jnp.minimum(bb - bj, 0.0))
                colv = jnp.sum(x, axis=-1, keepdims=True)
                colv = jnp.where(row_sub >= j, colv, 0.0)
                ob = ob + colv * vj
            outs.append(ob)
        o = jnp.concatenate(outs, axis=0)
        bl = b_scr[ck - 1:ck, :]
        kdec = (kf * jnp.exp(bl - b)).astype(BF16)
        st_scr[...] = st * jnp.exp(bl) + lax.dot_general(
            vb, kdec, (((0,), (0,)), ((), ())), preferred_element_type=F32)
        ms = jnp.sum(o * o, axis=-1, keepdims=True) * (1.0 / C_DV)
        on = o * lax.rsqrt(ms + EPS) * gn_ref[...]
        rr = r_ref[rows, :].astype(F32)
        o_ref[rows, :] = (on * (rr / (1.0 + jnp.exp(-rr)))).astype(o_ref.dtype)
        return carry

    lax.fori_loop(0, q_ref.shape[0] // ck, chunk, 0)


def gla_mixer(proj3, small3, w2p, cbp, gnp):
    bsz, s, _ = proj3.shape
    t = GLA_ROWS

    def col(base, width):
        return pl.BlockSpec((None, t, width), lambda b, h, i, c=base // width: (b, i, c + h))

    return pl.pallas_call(
        _gla_kernel, grid=(bsz, C_HEADS, s // t),
        in_specs=[col(COL_QC, C_DK_PAD), col(COL_KC, C_DK_PAD),
                  col(COL_VC, C_DV_PAD), col(COL_RC, C_DV_PAD),
                  pl.BlockSpec((None, t, V7X_LANES), lambda b, h, i: (b, i, 0)),
                  pl.BlockSpec((None, V7X_LANES, C_DK_PAD), lambda b, h, i: (h, 0, 0)),
                  pl.BlockSpec((None, 1, C_DK_PAD), lambda b, h, i: (h, 0, 0)),
                  pl.BlockSpec((1, C_DV_PAD), lambda b, h, i: (0, 0))],
        out_specs=pl.BlockSpec((None, t, C_DV_PAD), lambda b, h, i: (b, i, h)),
        out_shape=jax.ShapeDtypeStruct((bsz, s, C_HEADS * C_DV_PAD), BF16),
        scratch_shapes=[pltpu.VMEM((C_DV_PAD, C_DK_PAD), F32),
                        pltpu.VMEM((C_CHUNK, C_DK_PAD), F32),
                        pltpu.VMEM((C_CHUNK, C_DK_PAD), F32),
                        pltpu.VMEM((C_CHUNK, C_DV_PAD), F32)],
        compiler_params=_cparams(("parallel", "parallel", "arbitrary")), name="gla_mixer",
    )(proj3, proj3, proj3, proj3, small3, w2p, cbp, gnp)


def _out_proj_kernel(x_ref, a_ref, b_ref, c_ref, wa_ref, wb_ref, wc_ref, o_ref):
    acc = jnp.dot(a_ref[...], wa_ref[...], preferred_element_type=F32)
    acc += jnp.dot(b_ref[...], wb_ref[...], preferred_element_type=F32)
    acc += jnp.dot(c_ref[...], wc_ref[...], preferred_element_type=F32)
    o_ref[...] = x_ref[...] + acc


def out_proj(x2d, oa, ob, oc, wa, wb, wc, *, tm=1024, tn=512):
    m, n = x2d.shape

    def rows(a):
        return pl.BlockSpec((tm, a.shape[1]), lambda i, j: (i, 0))

    def cols(w):
        return pl.BlockSpec((w.shape[0], tn), lambda i, j: (0, j))

    xs = pl.BlockSpec((tm, tn), lambda i, j: (i, j))
    return pl.pallas_call(
        _out_proj_kernel, grid=(m // tm, n // tn),
        in_specs=[xs, rows(oa), rows(ob), rows(oc), cols(wa), cols(wb), cols(wc)],
        out_specs=xs, out_shape=jax.ShapeDtypeStruct((m, n), F32),
        compiler_params=_cparams(("parallel", "arbitrary")), name="out_proj",
    )(x2d, oa, ob, oc, wa, wb, wc)


CROSS_ROWS = 512


def _cross_kernel(x_ref, g_ref, wq_ref, k_ref, v_ref, wo_ref, o_ref):
    x = x_ref[...]
    h = _rms_rows(x, g_ref[...]).astype(BF16)
    q = jnp.dot(h, wq_ref[...], preferred_element_type=F32).astype(BF16)
    heads = []
    for hd in range(CROSS_HEADS):
        cs = slice(CROSS_DH * hd, CROSS_DH * (hd + 1))
        logits = lax.dot_general(q[:, cs], k_ref[:, cs], (((1,), (1,)), ((), ())),
                                 preferred_element_type=F32) * (CROSS_DH ** -0.5)
        m = jnp.max(logits, axis=-1, keepdims=True)
        p = jnp.exp(logits - m)
        ssum = jnp.sum(p, axis=-1, keepdims=True)
        pv = jnp.dot(p.astype(BF16), v_ref[:, cs], preferred_element_type=F32)
        heads.append((pv / ssum).astype(BF16))
    o = jnp.concatenate(heads, axis=1)
    o_ref[...] = x + jnp.dot(o, wo_ref[...], preferred_element_type=F32)


def cross_attention(x2d, gain, wq, kv, wo, *, seq):
    m, d = x2d.shape
    tm = CROSS_ROWS
    mem_len = kv.shape[1]
    per_batch = seq // tm
    return pl.pallas_call(
        _cross_kernel, grid=(m // tm,),
        in_specs=[pl.BlockSpec((tm, d), lambda i: (i, 0)),
                  pl.BlockSpec((1, d), lambda i: (0, 0)),
                  pl.BlockSpec((d, CROSS_WIDTH), lambda i: (0, 0)),
                  pl.BlockSpec((None, mem_len, CROSS_WIDTH), lambda i: (i // per_batch, 0, 0)),
                  pl.BlockSpec((None, mem_len, CROSS_WIDTH), lambda i: (i // per_batch, 0, 1)),
                  pl.BlockSpec((CROSS_WIDTH, d), lambda i: (0, 0))],
        out_specs=pl.BlockSpec((tm, d), lambda i: (i, 0)),
        out_shape=jax.ShapeDtypeStruct((m, d), F32),
        compiler_params=_cparams(("parallel",)), name="cross_attention",
    )(x2d, gain.reshape(1, d), wq, kv, kv, wo)


FFN_ROWS = 512
FFN_HALO = 16


def _ffn_kernel(x_ref, xh_ref, g_ref, wv_ref, wg_ref, cw_ref, cb_ref, wd_ref, gf_ref, o_ref,
                h_scr, hh_scr, *, final):
    j = pl.program_id(1)

    @pl.when(j == 0)
    def _():
        _norm_to_scratch(x_ref, g_ref, h_scr)
        hh_scr[...] = _rms_rows(xh_ref[...], g_ref[...]).astype(BF16)
        o_ref[...] = x_ref[...]

    h = h_scr[...]
    val = jnp.dot(h, wv_ref[...], preferred_element_type=F32)
    gate = jnp.dot(h, wg_ref[...], preferred_element_type=F32)
    gh = jnp.dot(hh_scr[...], wg_ref[...], preferred_element_type=F32)
    row = lax.broadcasted_iota(jnp.int32, gate.shape, 0)
    prev1 = gh[FFN_HALO - 1:FFN_HALO, :]
    prev2 = gh[FFN_HALO - 2:FFN_HALO - 1, :]
    g1 = jnp.where(row == 0, prev1, pltpu.roll(gate, 1, axis=0))
    g2 = jnp.where(row == 0, prev2, jnp.where(row == 1, prev1, pltpu.roll(gate, 2, axis=0)))
    gc = cb_ref[...] + cw_ref[0:1, :] * g2
    gc = gc + cw_ref[1:2, :] * g1
    gc = gc + cw_ref[2:3, :] * gate
    act = (gc / (1.0 + jnp.exp(-gc)) * val).astype(BF16)
    o_ref[...] += jnp.dot(act, wd_ref[...], preferred_element_type=F32)

    if final:
        @pl.when(j == pl.num_programs(1) - 1)
        def _():
            def body(r, c):
                sl = pl.ds(pl.multiple_of(r * NORM_ROWS, NORM_ROWS), NORM_ROWS)
                o_ref[sl, :] = _rms_rows(o_ref[sl, :], gf_ref[...])
                return c
            lax.fori_loop(0, o_ref.shape[0] // NORM_ROWS, body, 0)


def conv_ffn(x2d, gain, wv, wg, cw, cb, wd, gain_final, *, seq, final):
    m, d = x2d.shape
    tm, tf = FFN_ROWS, FF_TILE
    nt = m // tm
    tail = x2d.reshape(nt, tm, d)[:, tm - FFN_HALO:, :]
    halo = jnp.concatenate([jnp.zeros_like(tail[:1]), tail[:-1]], axis=0)
    starts = (jnp.arange(nt) % (seq // tm) == 0)[:, None, None]
    halo = jnp.where(starts, 0.0, halo)
    ff = wv.shape[1]
    return pl.pallas_call(
        functools.partial(_ffn_kernel, final=final), grid=(nt, ff // tf),
        in_specs=[pl.BlockSpec((tm, d), lambda i, j: (i, 0)),
                  pl.BlockSpec((None, FFN_HALO, d), lambda i, j: (i, 0, 0)),
                  pl.BlockSpec((1, d), lambda i, j: (0, 0)),
                  pl.BlockSpec((d, tf), lambda i, j: (0, j)),
                  pl.BlockSpec((d, tf), lambda i, j: (0, j)),
                  pl.BlockSpec((CONV_W, tf), lambda i, j: (0, j)),
                  pl.BlockSpec((1, tf), lambda i, j: (0, j)),
                  pl.BlockSpec((tf, d), lambda i, j: (j, 0)),
                  pl.BlockSpec((1, d), lambda i, j: (0, 0))],
        out_specs=pl.BlockSpec((tm, d), lambda i, j: (i, 0)),
        out_shape=jax.ShapeDtypeStruct((m, d), F32),
        scratch_shapes=[pltpu.VMEM((tm, d), BF16), pltpu.VMEM((FFN_HALO, d), BF16)],
        compiler_params=_cparams(("parallel", "arbitrary")), name="conv_ffn",
    )(x2d, halo, gain.reshape(1, d), wv, wg, cw, cb.reshape(1, ff), wd, gain_final.reshape(1, d))


def _pad_heads(w, heads, width, padded):
    lead = w.shape[:-1]
    w = w.reshape(lead + (heads, width))
    w = jnp.pad(w, [(0, 0)] * len(lead) + [(0, 0), (0, padded - width)])
    return w.reshape(lead + (heads * padded,))


def _layer_params(l, w_in, c_gate_w2, c_gate_b, c_norm, w_out, w_up, conv_w, conv_b, w_down):
    splits = np.cumsum([A_WIDTH] * 3 + [B_WIDTH] * 3 + [B_HEADS, C_KW, C_KW, C_VW, C_VW, C_GATE_RANK])
    qa_kb_vb = w_in[l][:, :splits[5]]
    fl, qc, kc, vc, rc, gl = jnp.split(w_in[l][:, splits[5]:], (splits[6:11] - splits[5]).tolist(), axis=1)
    w_main = jnp.concatenate([
        qa_kb_vb,
        _pad_heads(qc, C_HEADS, C_DK, C_DK_PAD), _pad_heads(kc, C_HEADS, C_DK, C_DK_PAD),
        _pad_heads(vc, C_HEADS, C_DV, C_DV_PAD), _pad_heads(rc, C_HEADS, C_DV, C_DV_PAD)],
        axis=1).astype(BF16)
    w_small = jnp.zeros((D_MODEL, V7X_LANES), F32)
    w_small = w_small.at[:, SMALL_FL:SMALL_FL + B_HEADS].set(fl)
    w_small = w_small.at[:, SMALL_GL:SMALL_GL + C_GATE_RANK].set(gl).astype(BF16)
    w2 = _pad_heads(c_gate_w2[l], C_HEADS, C_DK, C_DK_PAD).reshape(C_GATE_RANK, C_HEADS, C_DK_PAD)
    w2p = jnp.zeros((C_HEADS, V7X_LANES, C_DK_PAD), F32)
    w2p = w2p.at[:, SMALL_GL:SMALL_GL + C_GATE_RANK, :].set(jnp.transpose(w2, (1, 0, 2))).astype(BF16)
    cbp = _pad_heads(c_gate_b[l], C_HEADS, C_DK, C_DK_PAD).reshape(C_HEADS, 1, C_DK_PAD)
    gnp = jnp.pad(c_norm[l], (0, C_DV_PAD - C_DV)).reshape(1, C_DV_PAD)
    wo = w_out[l]
    wa = wo[:A_WIDTH].astype(BF16)
    wb = wo[A_WIDTH:A_WIDTH + B_WIDTH].astype(BF16)
    wc = wo[A_WIDTH + B_WIDTH:].reshape(C_HEADS, C_DV, D_MODEL)
    wc = jnp.pad(wc, ((0, 0), (0, C_DV_PAD - C_DV), (0, 0))).reshape(C_HEADS * C_DV_PAD, D_MODEL).astype(BF16)
    fpad = D_FF_PAD - D_FF
    wv = jnp.pad(w_up[l][:, :D_FF], ((0, 0), (0, fpad))).astype(BF16)
    wg = jnp.pad(w_up[l][:, D_FF:], ((0, 0), (0, fpad))).astype(BF16)
    cw = jnp.pad(conv_w[l], ((0, 0), (0, fpad)))
    cb = jnp.pad(conv_b[l], (0, fpad))
    wd = jnp.pad(w_down[l], ((0, fpad), (0, 0))).astype(BF16)
    return dict(w_main=w_main, w_small=w_small, w2p=w2p, cbp=cbp, gnp=gnp,
                wa=wa, wb=wb, wc=wc, wv=wv, wg=wg, cw=cw, cb=cb, wd=wd)


def hybrid_mixer(x2d, gain, p, f_bias, biases, *, bsz, seq):
    proj, small = norm_matmul(x2d, gain, p["w_main"], p["w_small"], tm=1024, tn=A_WIDTH, name="in_proj")
    proj3 = proj.reshape(bsz, seq, PROJ_PAD)
    small3 = small.reshape(bsz, seq, V7X_LANES)
    state = None
    for idx, (_, dil) in enumerate(A_PATTERNS):
        state = dilated_pattern(proj3, biases[idx], state, dilation=dil, first=idx == 0,
                                last=idx == len(A_PATTERNS) - 1, name=f"dilated_d{dil}")
    o_a = state.reshape(bsz * seq, A_WIDTH)
    o_b = fox_attention(*fox_prep(proj3, small3, f_bias)).reshape(bsz * seq, B_WIDTH)
    o_c = gla_mixer(proj3, small3, p["w2p"], p["cbp"], p["gnp"]).reshape(bsz * seq, C_HEADS * C_DV_PAD)
    return out_proj(x2d, o_a, o_b, o_c, p["wa"], p["wb"], p["wc"])


def kernel(x, mem, rel_table, mem_norm, norm_final, norm_mix, w_in, f_bias, c_gate_w2, c_gate_b,
           c_norm, w_out, norm_cross, w_cq, w_ckv, w_co, norm_ffn, w_up, conv_w, conv_b, w_down):
    bsz, seq, d = x.shape
    depth = w_in.shape[0]
    mem_len = mem.shape[1]
    assert d == D_MODEL and seq % (A_SPAN * A_PATTERNS[-1][1]) == 0 and seq % 1024 == 0
    biases = [_dilated_bias(rel_table, dil) for _, dil in A_PATTERNS]
    x2d = x.reshape(bsz * seq, d)
    mem2d = mem.reshape(bsz * mem_len, d)
    for l in range(depth):
        p = _layer_params(l, w_in, c_gate_w2, c_gate_b, c_norm, w_out, w_up, conv_w, conv_b, w_down)
        x2d = hybrid_mixer(x2d, norm_mix[l], p, f_bias[l], biases, bsz=bsz, seq=seq)
        kv = norm_matmul(mem2d, mem_norm, w_ckv[l].astype(BF16), tm=min(512, bsz * mem_len), tn=512,
                         name="mem_kv").reshape(bsz, mem_len, 2 * CROSS_WIDTH)
        x2d = cross_attention(x2d, norm_cross[l], w_cq[l].astype(BF16), kv, w_co[l].astype(BF16), seq=seq)
        x2d = conv_ffn(x2d, norm_ffn[l], p["wv"], p["wg"], p["cw"], p["cb"], p["wd"], norm_final,
                       seq=seq, final=l == depth - 1)
    return x2d.reshape(bsz, seq, d)
```

```python
import functools
import math

import numpy as np
import jax
import jax.numpy as jnp
from jax import lax
from jax.experimental import pallas as pl
from jax.experimental.pallas import tpu as pltpu

F32 = jnp.float32
BF16 = jnp.bfloat16

V7X_LANES = 128
V7X_VMEM_LIMIT_BYTES = 56 * 1024 * 1024

D_MODEL = 2048
HEAD_DIM = 64
A_WIDTH = 3 * D_MODEL // 8
A_HEADS = A_WIDTH // HEAD_DIM
A_PATTERNS = ((128, 1), (512, 4), (2048, 16))
A_SPAN = 128
B_WIDTH = D_MODEL // 4
B_HEADS = B_WIDTH // HEAD_DIM
C_HEADS = 4
C_VW = 3 * D_MODEL // 8
C_DV = C_VW // C_HEADS
C_DK = C_DV // 2
C_KW = C_HEADS * C_DK
C_GATE_RANK = 16
C_GATE_TAU = 16.0
C_CHUNK = 64
C_SUB = 16
C_DK_PAD = 128
C_DV_PAD = 256
CROSS_HEADS = 4
CROSS_DH = 128
CROSS_WIDTH = CROSS_HEADS * CROSS_DH
D_FF = ((8 * D_MODEL // 3 + 127) // 128) * 128
FF_TILE = 512
D_FF_PAD = ((D_FF + FF_TILE - 1) // FF_TILE) * FF_TILE
CONV_W = 3
REL_BUCKETS = 32
REL_MAX_DIST = 2048
EPS = 1e-6
NEG = -1e30

COL_QA = 0
COL_KA = COL_QA + A_WIDTH
COL_VA = COL_KA + A_WIDTH
PROJ_A = COL_VA + A_WIDTH
COL_QB = 0
COL_KB = COL_QB + B_WIDTH
COL_VB = COL_KB + B_WIDTH
COL_QC = COL_VB + B_WIDTH
COL_KC = COL_QC + C_HEADS * C_DK_PAD
COL_VC = COL_KC + C_HEADS * C_DK_PAD
COL_RC = COL_VC + C_HEADS * C_DV_PAD
PROJ_PAD = COL_RC + C_HEADS * C_DV_PAD
SMALL_FL = 0
SMALL_GL = B_HEADS


def _cparams(sem):
    return pltpu.CompilerParams(dimension_semantics=sem,
                                vmem_limit_bytes=V7X_VMEM_LIMIT_BYTES)


def _rms_rows(x, g):
    ms = jnp.mean(x * x, axis=-1, keepdims=True)
    return x * lax.rsqrt(ms + EPS) * g


def _split3(x):
    hi = x.astype(BF16)
    r1 = x - hi.astype(F32)
    mid = r1.astype(BF16)
    lo = (r1 - mid.astype(F32)).astype(BF16)
    return hi, mid, lo


def _log_sigmoid(x):
    return jnp.minimum(x, 0.0) - jnp.log(1.0 + jnp.exp(-jnp.abs(x)))


def _tril_ones(n, dtype):
    r = lax.broadcasted_iota(jnp.int32, (n, n), 0)
    c = lax.broadcasted_iota(jnp.int32, (n, n), 1)
    return jnp.where(r >= c, 1.0, 0.0).astype(dtype)


NORM_ROWS = 256


def _norm_to_scratch(x_ref, g_ref, h_scr):
    def body(r, c):
        sl = pl.ds(pl.multiple_of(r * NORM_ROWS, NORM_ROWS), NORM_ROWS)
        h_scr[sl, :] = _rms_rows(x_ref[sl, :], g_ref[...]).astype(BF16)
        return c
    lax.fori_loop(0, x_ref.shape[0] // NORM_ROWS, body, 0)


def _norm_matmul_kernel(x_ref, g_ref, w_ref, o_ref, h_scr):
    @pl.when(pl.program_id(1) == 0)
    def _():
        _norm_to_scratch(x_ref, g_ref, h_scr)
    o_ref[...] = jnp.dot(h_scr[...], w_ref[...],
                         preferred_element_type=F32).astype(o_ref.dtype)


def _norm_matmul_small_kernel(x_ref, g_ref, w_ref, ws_ref, o_ref, os_ref, h_scr):
    @pl.when(pl.program_id(1) == 0)
    def _():
        _norm_to_scratch(x_ref, g_ref, h_scr)
        os_ref[...] = jnp.dot(h_scr[...], ws_ref[...], preferred_element_type=F32)
    o_ref[...] = jnp.dot(h_scr[...], w_ref[...],
                         preferred_element_type=F32).astype(o_ref.dtype)


def norm_matmul(x2d, gain, w, w_small=None, *, tm, tn, name, out_dtype=BF16):
    m, k = x2d.shape
    n = w.shape[1]
    grid = (m // tm, n // tn)
    in_specs = [pl.BlockSpec((tm, k), lambda i, j: (i, 0)),
                pl.BlockSpec((1, k), lambda i, j: (0, 0)),
                pl.BlockSpec((k, tn), lambda i, j: (0, j))]
    out_specs = pl.BlockSpec((tm, tn), lambda i, j: (i, j))
    out_shape = jax.ShapeDtypeStruct((m, n), out_dtype)
    args = [x2d, gain.reshape(1, k), w]
    kern = _norm_matmul_kernel
    if w_small is not None:
        ns = w_small.shape[1]
        in_specs.append(pl.BlockSpec((k, ns), lambda i, j: (0, 0)))
        out_specs = [out_specs, pl.BlockSpec((tm, ns), lambda i, j: (i, 0))]
        out_shape = [out_shape, jax.ShapeDtypeStruct((m, ns), F32)]
        args.append(w_small)
        kern = _norm_matmul_small_kernel
    return pl.pallas_call(
        kern, grid=grid, in_specs=in_specs, out_specs=out_specs, out_shape=out_shape,
        scratch_shapes=[pltpu.VMEM((tm, k), BF16)],
        compiler_params=_cparams(("parallel", "arbitrary")), name=name)(*args)


def _t5_bucket(dist):
    max_exact = REL_BUCKETS // 2
    d = np.maximum(dist, 1).astype(np.float32)
    large = max_exact + (np.log(d / max_exact) / math.log(REL_MAX_DIST / max_exact)
                         * (REL_BUCKETS - max_exact)).astype(np.int32)
    return np.where(dist < max_exact, dist, np.minimum(large, REL_BUCKETS - 1)).astype(np.int32)


def _dilated_bias(rel_table, dilation):
    span = A_SPAN
    i = np.arange(span)[:, None]
    j = np.arange(2 * span)[None, :]
    step = i + span - j
    valid = (step >= 0) & (step <= span)
    bucket = _t5_bucket(np.clip(step, 0, span) * dilation)
    b = jnp.transpose(rel_table[bucket], (2, 0, 1)).astype(F32)
    later = jnp.where(valid[None], b, NEG)
    first = jnp.where((valid & (j >= span))[None], b, NEG)
    return jnp.stack([first, later], axis=0)


DIL_TILE = A_SPAN * max(d for _, d in A_PATTERNS)
DIL_UNITS = DIL_TILE // A_SPAN


def _dilated_kernel(q_ref, kp_ref, kc_ref, vp_ref, vc_ref, bias_ref, o_ref,
                    kk_scr, vv_scr, m_scr, n_scr, s_scr):
    tile, span = DIL_TILE, A_SPAN
    first_tile = pl.program_id(1) == 0
    kk_scr[0:tile, :] = kp_ref[...]
    kk_scr[tile:2 * tile, :] = kc_ref[...]
    vv_scr[0:tile, :] = vp_ref[...]
    vv_scr[tile:2 * tile, :] = vc_ref[...]
    lane_q = lax.broadcasted_iota(jnp.int32, (span, V7X_LANES), 1)
    lane_k = lax.broadcasted_iota(jnp.int32, (2 * span, V7X_LANES), 1)
    low_q = lane_q < HEAD_DIM
    low_k = lane_k < HEAD_DIM
    scale = HEAD_DIM ** -0.5
    qmask = (jnp.where(low_q, scale, 0.0), jnp.where(low_q, 0.0, scale))
    nt = (((1,), (1,)), ((), ()))

    for pat, (_, d) in enumerate(A_PATTERNS):
        for u in range(DIL_UNITS):
            n, r = divmod(u, d)
            q0 = r + span * d * n
            k0 = tile + q0 - span * d
            if d == 1:
                qrows, krows = pl.ds(q0, span), pl.ds(k0, 2 * span)
            else:
                qrows, krows = pl.ds(q0, span, stride=d), pl.ds(k0, 2 * span, stride=d)
            qf = q_ref[qrows, :]
            kb = kk_scr[krows, :].astype(BF16)
            vf = vv_scr[krows, :]
            sel = jnp.where(first_tile, 0, 1) if n == 0 else 1
            pvs, ms = [], []
            for e in range(2):
                qe = (qf * qmask[e]).astype(BF16)
                logits = lax.dot_general(qe, kb, nt, preferred_element_type=F32) + bias_ref[pat, sel, e]
                m = jnp.max(logits, axis=-1, keepdims=True)
                p = jnp.exp(logits - m)
                ve = (jnp.where(low_k, vf, 1.0) if e == 0 else jnp.where(low_k, 1.0, vf)).astype(BF16)
                pvs.append(jnp.dot(p.astype(BF16), ve, preferred_element_type=F32))
                ms.append(m)
            num = jnp.where(low_q, pvs[0], pvs[1])
            den = pltpu.roll(jnp.where(low_q, pvs[1], pvs[0]), HEAD_DIM, axis=1)
            mx = jnp.where(low_q, ms[0], ms[1])
            if pat > 0:
                m_old = m_scr[qrows, :]
                m_new = jnp.maximum(m_old, mx)
                a = jnp.exp(m_old - m_new)
                b = jnp.exp(mx - m_new)
                num = a * n_scr[qrows, :] + b * num
                den = a * s_scr[qrows, :] + b * den
                mx = m_new
            m_scr[qrows, :] = mx
            n_scr[qrows, :] = num
            s_scr[qrows, :] = den
    o_ref[...] = (n_scr[...] / s_scr[...]).astype(o_ref.dtype)


def dilated_mixture(qkv3, bias):
    bsz, s, _ = qkv3.shape
    pairs = A_HEADS // 2
    blk = (None, DIL_TILE, V7X_LANES)
    ck, cv = COL_KA // V7X_LANES, COL_VA // V7X_LANES

    def cur(col):
        return pl.BlockSpec(blk, lambda b, t, g: (b, t, col + g))

    def prev(col):
        return pl.BlockSpec(blk, lambda b, t, g: (b, jnp.maximum(t - 1, 0), col + g))

    tile_f32 = pltpu.VMEM((DIL_TILE, V7X_LANES), F32)
    both_f32 = pltpu.VMEM((2 * DIL_TILE, V7X_LANES), F32)
    return pl.pallas_call(
        _dilated_kernel, grid=(bsz, s // DIL_TILE, pairs),
        in_specs=[cur(0), prev(ck), cur(ck), prev(cv), cur(cv),
                  pl.BlockSpec((len(A_PATTERNS), 2, 2, A_SPAN, 2 * A_SPAN),
                               lambda b, t, g: (0, 0, g, 0, 0))],
        out_specs=pl.BlockSpec(blk, lambda b, t, g: (b, t, g)),
        out_shape=jax.ShapeDtypeStruct((bsz, s, A_WIDTH), BF16),
        scratch_shapes=[both_f32, both_f32, tile_f32, tile_f32, tile_f32],
        compiler_params=_cparams(("parallel", "parallel", "arbitrary")), name="dilated_mixture",
    )(qkv3, qkv3, qkv3, qkv3, qkv3, bias)


FOX_PREP_ROWS = 512
FOX_TQ = 512
FOX_TK = 512
FOX_HEADS_PER_STEP = 4


def _fox_prep_kernel(sm_ref, fb_ref, q0_ref, q1_ref, k0_ref, k1_ref, v0_ref, v1_ref,
                     qa_ref, ka_ref, va_ref, carry_scr):
    t = sm_ref.shape[0]

    @pl.when(pl.program_id(1) == 0)
    def _():
        carry_scr[...] = jnp.zeros_like(carry_scr)

    x = _log_sigmoid(sm_ref[...] + fb_ref[...])
    tri = _tril_ones(t, BF16)
    hi, mid, lo = _split3(x)
    c = (jnp.dot(tri, hi, preferred_element_type=F32)
         + jnp.dot(tri, mid, preferred_element_type=F32)
         + jnp.dot(tri, lo, preferred_element_type=F32)) + carry_scr[0:1, :]
    carry_scr[...] = jnp.broadcast_to(c[t - 1:t, :], carry_scr.shape)
    chi, cmid, clo = [p.astype(F32) for p in _split3(c)]

    lane = lax.broadcasted_iota(jnp.int32, (t, V7X_LANES), 1)
    low = lane < HEAD_DIM
    scale = HEAD_DIM ** -0.5
    q_refs, k_refs, v_refs = (q0_ref, q1_ref), (k0_ref, k1_ref), (v0_ref, v1_ref)
    for h in range(B_HEADS):
        src, pair = divmod(h // 2, 2)
        cs = slice(V7X_LANES * pair, V7X_LANES * (pair + 1))
        qp = q_refs[src][:, cs].astype(F32)
        kp = k_refs[src][:, cs].astype(F32)
        vp = v_refs[src][:, cs].astype(F32)
        if h % 2 == 1:
            qp = pltpu.roll(qp, HEAD_DIM, axis=1)
            kp = pltpu.roll(kp, HEAD_DIM, axis=1)
        c1, c2, c3 = chi[:, h:h + 1], cmid[:, h:h + 1], clo[:, h:h + 1]
        one = jnp.ones((t, 1), F32)
        zero = jnp.zeros((t, 1), F32)
        q_aug, k_aug = zero, zero
        for idx, (qv, kv) in enumerate(((c1, one), (c2, one), (c3, one),
                                        (one, -c1), (one, -c2), (one, -c3))):
            hit = lane == HEAD_DIM + idx
            q_aug = jnp.where(hit, qv, q_aug)
            k_aug = jnp.where(hit, kv, k_aug)
        qa_ref[h] = jnp.where(low, qp * scale, q_aug).astype(BF16)
        ka_ref[h] = jnp.where(low, kp, k_aug).astype(BF16)
        if h % 2 == 0:
            va_ref[h] = jnp.where(low, vp, 1.0).astype(BF16)
        else:
            va_ref[h] = jnp.where(low, 1.0, vp).astype(BF16)


def fox_prep(proj3, small3, f_bias):
    bsz, s, _ = proj3.shape
    t = FOX_PREP_ROWS
    half = B_WIDTH // 2
    fb = jnp.zeros((1, V7X_LANES), F32).at[0, SMALL_FL:SMALL_FL + B_HEADS].set(f_bias)

    def colspec(col):
        return pl.BlockSpec((None, t, half), lambda b, i, c=col // half: (b, i, c))

    aug_spec = pl.BlockSpec((None, B_HEADS, t, V7X_LANES), lambda b, i: (b, 0, i, 0))
    aug_shape = jax.ShapeDtypeStruct((bsz, B_HEADS, s, V7X_LANES), BF16)
    return pl.pallas_call(
        _fox_prep_kernel, grid=(bsz, s // t),
        in_specs=[pl.BlockSpec((None, t, V7X_LANES), lambda b, i: (b, i, 0)),
                  pl.BlockSpec((1, V7X_LANES), lambda b, i: (0, 0)),
                  colspec(COL_QB), colspec(COL_QB + half),
                  colspec(COL_KB), colspec(COL_KB + half),
                  colspec(COL_VB), colspec(COL_VB + half)],
        out_specs=[aug_spec, aug_spec, aug_spec],
        out_shape=[aug_shape, aug_shape, aug_shape],
        scratch_shapes=[pltpu.VMEM((8, V7X_LANES), F32)],
        compiler_params=_cparams(("parallel", "arbitrary")), name="fox_prep",
    )(small3, fb, proj3, proj3, proj3, proj3, proj3, proj3)


def _fox_kernel(q_ref, k_ref, v_ref, o_ref):
    tq, tk = FOX_TQ, FOX_TK
    qi = pl.program_id(2)
    nfull = qi * (tq // tk)
    row = lax.broadcasted_iota(jnp.int32, (tq, tk), 0)
    col = lax.broadcasted_iota(jnp.int32, (tq, tk), 1)
    lane = lax.broadcasted_iota(jnp.int32, (tq, V7X_LANES), 1)
    nh = q_ref.shape[0]
    qs = [q_ref[e] for e in range(nh)]

    def tile(e, kt, m, acc, mask):
        rows = pl.ds(pl.multiple_of(kt * tk, tk), tk)
        s = lax.dot_general(qs[e], k_ref[e, rows, :], (((1,), (1,)), ((), ())),
                            preferred_element_type=F32)
        if mask is not None:
            s = jnp.where(mask, s, NEG)
        m_new = jnp.maximum(m, jnp.max(s, axis=-1, keepdims=True))
        alpha = jnp.exp(m - m_new)
        p = jnp.exp(s - m_new)
        acc = alpha * acc + jnp.dot(p.astype(BF16), v_ref[e, rows, :], preferred_element_type=F32)
        return m_new, acc

    def heads(kt, carry, mask):
        out = []
        for e in range(nh):
            out += tile(e, kt, carry[2 * e], carry[2 * e + 1], mask)
        return tuple(out)

    m0 = jnp.full((tq, 1), NEG, F32)
    acc0 = jnp.zeros((tq, V7X_LANES), F32)
    carry = lax.fori_loop(0, nfull, lambda kt, c: heads(kt, c, None), (m0, acc0) * nh)
    for dd in range(tq // tk):
        carry = heads(nfull + dd, carry, col + dd * tk <= row)
    for g in range(nh // 2):
        even, odd = carry[4 * g + 1], carry[4 * g + 3]
        res = [acc / pltpu.roll(acc, HEAD_DIM, axis=1) for acc in (even, odd)]
        o_ref[:, V7X_LANES * g:V7X_LANES * (g + 1)] = jnp.where(
            lane < HEAD_DIM, res[0], res[1]).astype(o_ref.dtype)


def fox_attention(q_aug, k_aug, v_aug):
    bsz, nh, s, _ = q_aug.shape
    tq, hs = FOX_TQ, FOX_HEADS_PER_STEP
    return pl.pallas_call(
        _fox_kernel, grid=(bsz, nh // hs, s // tq),
        in_specs=[pl.BlockSpec((None, hs, tq, V7X_LANES), lambda b, g, i: (b, g, i, 0)),
                  pl.BlockSpec((None, hs, s, V7X_LANES), lambda b, g, i: (b, g, 0, 0)),
                  pl.BlockSpec((None, hs, s, V7X_LANES), lambda b, g, i: (b, g, 0, 0))],
        out_specs=pl.BlockSpec((None, tq, hs * HEAD_DIM), lambda b, g, i: (b, i, g)),
        out_shape=jax.ShapeDtypeStruct((bsz, s, B_WIDTH), BF16),
        compiler_params=_cparams(("parallel", "parallel", "arbitrary")), name="fox_attention",
    )(q_aug, k_aug, v_aug)


GLA_ROWS = 256


def _gla_kernel(q_ref, k_ref, v_ref, r_ref, sm_ref, w2_ref, cb_ref, gn_ref, o_ref,
                st_scr, b_scr, k_scr, v_scr):
    ck, sub = C_CHUNK, C_SUB

    @pl.when(pl.program_id(2) == 0)
    def _():
        st_scr[...] = jnp.zeros_like(st_scr)

    tri = _tril_ones(ck, BF16)
    row_sub = lax.broadcasted_iota(jnp.int32, (sub, 1), 0)
    row_ck = lax.broadcasted_iota(jnp.int32, (ck, 1), 0)
    nt = (((1,), (1,)), ((), ()))

    def chunk(c, carry):
        rows = pl.ds(pl.multiple_of(c * ck, ck), ck)
        g = jnp.dot(sm_ref[rows, :].astype(BF16), w2_ref[...],
                    preferred_element_type=F32) + cb_ref[...]
        la = _log_sigmoid(g) / C_GATE_TAU
        hi, mid, lo = _split3(la)
        b = (jnp.dot(tri, hi, preferred_element_type=F32)
             + jnp.dot(tri, mid, preferred_element_type=F32)
             + jnp.dot(tri, lo, preferred_element_type=F32))
        qf = q_ref[rows, :].astype(F32) * (C_DK ** -0.5)
        kf = k_ref[rows, :].astype(F32)
        vb = v_ref[rows, :]
        b_scr[...] = b
        k_scr[...] = kf
        v_scr[...] = vb.astype(F32)
        st = st_scr[...]
        o_inter = lax.dot_general((qf * jnp.exp(b)).astype(BF16), st.astype(BF16), nt,
                                  preferred_element_type=F32)
        outs = []
        for blk in range(ck // sub):
            r0 = blk * sub
            qb = qf[r0:r0 + sub, :]
            bb = b[r0:r0 + sub, :]
            ob = o_inter[r0:r0 + sub, :]
            if blk > 0:
                ref_row = b_scr[r0 - 1:r0, :]
                kd = jnp.where(row_ck < r0, kf * jnp.exp(jnp.minimum(ref_row - b, 0.0)), 0.0)
                a = lax.dot_general((qb * jnp.exp(bb - ref_row)).astype(BF16), kd.astype(BF16), nt,
                                    preferred_element_type=F32)
                ob = ob + jnp.dot(a.astype(BF16), vb, preferred_element_type=F32)
            for j in range(sub):
                bj = b_scr[r0 + j:r0 + j + 1, :]
                kj = k_scr[r0 + j:r0 + j + 1, :]
                vj = v_scr[r0 + j:r0 + j + 1, :]
                x = qb * kj * jnp.exp(jnp.minimum(bb - bj, 0.0))
                colv = jnp.sum(x, axis=-1, keepdims=True)
                colv = jnp.where(row_sub >= j, colv, 0.0)
                ob = ob + colv * vj
            outs.append(ob)
        o = jnp.concatenate(outs, axis=0)
        bl = b_scr[ck - 1:ck, :]
        kdec = (kf * jnp.exp(bl - b)).astype(BF16)
        st_scr[...] = st * jnp.exp(bl) + lax.dot_general(
            vb, kdec, (((0,), (0,)), ((), ())), preferred_element_type=F32)
        ms = jnp.sum(o * o, axis=-1, keepdims=True) * (1.0 / C_DV)
        on = o * lax.rsqrt(ms + EPS) * gn_ref[...]
        rr = r_ref[rows, :].astype(F32)
        o_ref[rows, :] = (on * (rr / (1.0 + jnp.exp(-rr)))).astype(o_ref.dtype)
        return carry

    lax.fori_loop(0, q_ref.shape[0] // ck, chunk, 0)


def gla_mixer(proj3, small3, w2p, cbp, gnp):
    bsz, s, _ = proj3.shape
    t = GLA_ROWS

    def col(base, width):
        return pl.BlockSpec((None, t, width), lambda b, h, i, c=base // width: (b, i, c + h))

    return pl.pallas_call(
        _gla_kernel, grid=(bsz, C_HEADS, s // t),
        in_specs=[col(COL_QC, C_DK_PAD), col(COL_KC, C_DK_PAD),
                  col(COL_VC, C_DV_PAD), col(COL_RC, C_DV_PAD),
                  pl.BlockSpec((None, t, V7X_LANES), lambda b, h, i: (b, i, 0)),
                  pl.BlockSpec((None, V7X_LANES, C_DK_PAD), lambda b, h, i: (h, 0, 0)),
                  pl.BlockSpec((None, 1, C_DK_PAD), lambda b, h, i: (h, 0, 0)),
                  pl.BlockSpec((1, C_DV_PAD), lambda b, h, i: (0, 0))],
        out_specs=pl.BlockSpec((None, t, C_DV_PAD), lambda b, h, i: (b, i, h)),
        out_shape=jax.ShapeDtypeStruct((bsz, s, C_HEADS * C_DV_PAD), BF16),
        scratch_shapes=[pltpu.VMEM((C_DV_PAD, C_DK_PAD), F32),
                        pltpu.VMEM((C_CHUNK, C_DK_PAD), F32),
                        pltpu.VMEM((C_CHUNK, C_DK_PAD), F32),
                        pltpu.VMEM((C_CHUNK, C_DV_PAD), F32)],
        compiler_params=_cparams(("parallel", "parallel", "arbitrary")), name="gla_mixer",
    )(proj3, proj3, proj3, proj3, small3, w2p, cbp, gnp)


def _out_proj_kernel(x_ref, a_ref, b_ref, c_ref, wa_ref, wb_ref, wc_ref, o_ref):
    acc = jnp.dot(a_ref[...], wa_ref[...], preferred_element_type=F32)
    acc += jnp.dot(b_ref[...], wb_ref[...], preferred_element_type=F32)
    acc += jnp.dot(c_ref[...], wc_ref[...], preferred_element_type=F32)
    o_ref[...] = x_ref[...] + acc


def out_proj(x2d, oa, ob, oc, wa, wb, wc, *, tm=1024, tn=512):
    m, n = x2d.shape

    def rows(a):
        return pl.BlockSpec((tm, a.shape[1]), lambda i, j: (i, 0))

    def cols(w):
        return pl.BlockSpec((w.shape[0], tn), lambda i, j: (0, j))

    xs = pl.BlockSpec((tm, tn), lambda i, j: (i, j))
    return pl.pallas_call(
        _out_proj_kernel, grid=(m // tm, n // tn),
        in_specs=[xs, rows(oa), rows(ob), rows(oc), cols(wa), cols(wb), cols(wc)],
        out_specs=xs, out_shape=jax.ShapeDtypeStruct((m, n), F32),
        compiler_params=_cparams(("parallel", "arbitrary")), name="out_proj",
    )(x2d, oa, ob, oc, wa, wb, wc)


CROSS_ROWS = 512


def _cross_kernel(x_ref, g_ref, wq_ref, k_ref, v_ref, wo_ref, o_ref):
    x = x_ref[...]
    h = _rms_rows(x, g_ref[...]).astype(BF16)
    q = jnp.dot(h, wq_ref[...], preferred_element_type=F32).astype(BF16)
    heads = []
    for hd in range(CROSS_HEADS):
        cs = slice(CROSS_DH * hd, CROSS_DH * (hd + 1))
        logits = lax.dot_general(q[:, cs], k_ref[:, cs], (((1,), (1,)), ((), ())),
                                 preferred_element_type=F32) * (CROSS_DH ** -0.5)
        m = jnp.max(logits, axis=-1, keepdims=True)
        p = jnp.exp(logits - m)
        ssum = jnp.sum(p, axis=-1, keepdims=True)
        pv = jnp.dot(p.astype(BF16), v_ref[:, cs], preferred_element_type=F32)
        heads.append((pv / ssum).astype(BF16))
    o = jnp.concatenate(heads, axis=1)
    o_ref[...] = x + jnp.dot(o, wo_ref[...], preferred_element_type=F32)


def cross_attention(x2d, gain, wq, kv, wo, *, seq):
    m, d = x2d.shape
    tm = CROSS_ROWS
    mem_len = kv.shape[1]
    per_batch = seq // tm
    return pl.pallas_call(
        _cross_kernel, grid=(m // tm,),
        in_specs=[pl.BlockSpec((tm, d), lambda i: (i, 0)),
                  pl.BlockSpec((1, d), lambda i: (0, 0)),
                  pl.BlockSpec((d, CROSS_WIDTH), lambda i: (0, 0)),
                  pl.BlockSpec((None, mem_len, CROSS_WIDTH), lambda i: (i // per_batch, 0, 0)),
                  pl.BlockSpec((None, mem_len, CROSS_WIDTH), lambda i: (i // per_batch, 0, 1)),
                  pl.BlockSpec((CROSS_WIDTH, d), lambda i: (0, 0))],
        out_specs=pl.BlockSpec((tm, d), lambda i: (i, 0)),
        out_shape=jax.ShapeDtypeStruct((m, d), F32),
        compiler_params=_cparams(("parallel",)), name="cross_attention",
    )(x2d, gain.reshape(1, d), wq, kv, kv, wo)


FFN_ROWS = 512
FFN_HALO = 16


def _ffn_kernel(x_ref, xh_ref, g_ref, wv_ref, wg_ref, cw_ref, cb_ref, wd_ref, gf_ref, o_ref,
                h_scr, hh_scr, *, final):
    j = pl.program_id(1)

    @pl.when(j == 0)
    def _():
        _norm_to_scratch(x_ref, g_ref, h_scr)
        hh_scr[...] = _rms_rows(xh_ref[...], g_ref[...]).astype(BF16)
        o_ref[...] = x_ref[...]

    h = h_scr[...]
    val = jnp.dot(h, wv_ref[...], preferred_element_type=F32)
    gate = jnp.dot(h, wg_ref[...], preferred_element_type=F32)
    gh = jnp.dot(hh_scr[...], wg_ref[...], preferred_element_type=F32)
    row = lax.broadcasted_iota(jnp.int32, gate.shape, 0)
    prev1 = gh[FFN_HALO - 1:FFN_HALO, :]
    prev2 = gh[FFN_HALO - 2:FFN_HALO - 1, :]
    g1 = jnp.where(row == 0, prev1, pltpu.roll(gate, 1, axis=0))
    g2 = jnp.where(row == 0, prev2, jnp.where(row == 1, prev1, pltpu.roll(gate, 2, axis=0)))
    gc = cb_ref[...] + cw_ref[0:1, :] * g2
    gc = gc + cw_ref[1:2, :] * g1
    gc = gc + cw_ref[2:3, :] * gate
    act = (gc / (1.0 + jnp.exp(-gc)) * val).astype(BF16)
    o_ref[...] += jnp.dot(act, wd_ref[...], preferred_element_type=F32)

    if final:
        @pl.when(j == pl.num_programs(1) - 1)
        def _():
            def body(r, c):
                sl = pl.ds(pl.multiple_of(r * NORM_ROWS, NORM_ROWS), NORM_ROWS)
                o_ref[sl, :] = _rms_rows(o_ref[sl, :], gf_ref[...])
                return c
            lax.fori_loop(0, o_ref.shape[0] // NORM_ROWS, body, 0)


def conv_ffn(x2d, gain, wv, wg, cw, cb, wd, gain_final, *, seq, final):
    m, d = x2d.shape
    tm, tf = FFN_ROWS, FF_TILE
    nt = m // tm
    tail = x2d.reshape(nt, tm, d)[:, tm - FFN_HALO:, :]
    halo = jnp.concatenate([jnp.zeros_like(tail[:1]), tail[:-1]], axis=0)
    starts = (jnp.arange(nt) % (seq // tm) == 0)[:, None, None]
    halo = jnp.where(starts, 0.0, halo)
    ff = wv.shape[1]
    return pl.pallas_call(
        functools.partial(_ffn_kernel, final=final), grid=(nt, ff // tf),
        in_specs=[pl.BlockSpec((tm, d), lambda i, j: (i, 0)),
                  pl.BlockSpec((None, FFN_HALO, d), lambda i, j: (i, 0, 0)),
                  pl.BlockSpec((1, d), lambda i, j: (0, 0)),
                  pl.BlockSpec((d, tf), lambda i, j: (0, j)),
                  pl.BlockSpec((d, tf), lambda i, j: (0, j)),
                  pl.BlockSpec((CONV_W, tf), lambda i, j: (0, j)),
                  pl.BlockSpec((1, tf), lambda i, j: (0, j)),
                  pl.BlockSpec((tf, d), lambda i, j: (j, 0)),
                  pl.BlockSpec((1, d), lambda i, j: (0, 0))],
        out_specs=pl.BlockSpec((tm, d), lambda i, j: (i, 0)),
        out_shape=jax.ShapeDtypeStruct((m, d), F32),
        scratch_shapes=[pltpu.VMEM((tm, d), BF16), pltpu.VMEM((FFN_HALO, d), BF16)],
        compiler_params=_cparams(("parallel", "arbitrary")), name="conv_ffn",
    )(x2d, halo, gain.reshape(1, d), wv, wg, cw, cb.reshape(1, ff), wd, gain_final.reshape(1, d))


def _pad_heads(w, heads, width, padded):
    lead = w.shape[:-1]
    w = w.reshape(lead + (heads, width))
    w = jnp.pad(w, [(0, 0)] * len(lead) + [(0, 0), (0, padded - width)])
    return w.reshape(lead + (heads * padded,))


def _layer_params(l, w_in, c_gate_w2, c_gate_b, c_norm, w_out, w_up, conv_w, conv_b, w_down):
    splits = np.cumsum([A_WIDTH] * 3 + [B_WIDTH] * 3 + [B_HEADS, C_KW, C_KW, C_VW, C_VW, C_GATE_RANK])
    w_a = w_in[l][:, :splits[2]].astype(BF16)
    fl, qc, kc, vc, rc, gl = jnp.split(w_in[l][:, splits[5]:], (splits[6:11] - splits[5]).tolist(), axis=1)
    w_main = jnp.concatenate([
        w_in[l][:, splits[2]:splits[5]],
        _pad_heads(qc, C_HEADS, C_DK, C_DK_PAD), _pad_heads(kc, C_HEADS, C_DK, C_DK_PAD),
        _pad_heads(vc, C_HEADS, C_DV, C_DV_PAD), _pad_heads(rc, C_HEADS, C_DV, C_DV_PAD)],
        axis=1).astype(BF16)
    w_small = jnp.zeros((D_MODEL, V7X_LANES), F32)
    w_small = w_small.at[:, SMALL_FL:SMALL_FL + B_HEADS].set(fl)
    w_small = w_small.at[:, SMALL_GL:SMALL_GL + C_GATE_RANK].set(gl).astype(BF16)
    w2 = _pad_heads(c_gate_w2[l], C_HEADS, C_DK, C_DK_PAD).reshape(C_GATE_RANK, C_HEADS, C_DK_PAD)
    w2p = jnp.zeros((C_HEADS, V7X_LANES, C_DK_PAD), F32)
    w2p = w2p.at[:, SMALL_GL:SMALL_GL + C_GATE_RANK, :].set(jnp.transpose(w2, (1, 0, 2))).astype(BF16)
    cbp = _pad_heads(c_gate_b[l], C_HEADS, C_DK, C_DK_PAD).reshape(C_HEADS, 1, C_DK_PAD)
    gnp = jnp.pad(c_norm[l], (0, C_DV_PAD - C_DV)).reshape(1, C_DV_PAD)
    wo = w_out[l]
    wa = wo[:A_WIDTH].astype(BF16)
    wb = wo[A_WIDTH:A_WIDTH + B_WIDTH].astype(BF16)
    wc = wo[A_WIDTH + B_WIDTH:].reshape(C_HEADS, C_DV, D_MODEL)
    wc = jnp.pad(wc, ((0, 0), (0, C_DV_PAD - C_DV), (0, 0))).reshape(C_HEADS * C_DV_PAD, D_MODEL).astype(BF16)
    fpad = D_FF_PAD - D_FF
    wv = jnp.pad(w_up[l][:, :D_FF], ((0, 0), (0, fpad))).astype(BF16)
    wg = jnp.pad(w_up[l][:, D_FF:], ((0, 0), (0, fpad))).astype(BF16)
    cw = jnp.pad(conv_w[l], ((0, 0), (0, fpad)))
    cb = jnp.pad(conv_b[l], (0, fpad))
    wd = jnp.pad(w_down[l], ((0, fpad), (0, 0))).astype(BF16)
    return dict(w_a=w_a, w_main=w_main, w_small=w_small, w2p=w2p, cbp=cbp, gnp=gnp,
                wa=wa, wb=wb, wc=wc, wv=wv, wg=wg, cw=cw, cb=cb, wd=wd)


def hybrid_mixer(x2d, gain, p, f_bias, bias_a, *, bsz, seq):
    qkv_a = norm_matmul(x2d, gain, p["w_a"], tm=1024, tn=A_WIDTH, name="in_proj_a", out_dtype=F32)
    proj, small = norm_matmul(x2d, gain, p["w_main"], p["w_small"], tm=1024, tn=A_WIDTH, name="in_proj")
    proj3 = proj.reshape(bsz, seq, PROJ_PAD)
    small3 = small.reshape(bsz, seq, V7X_LANES)
    o_a = dilated_mixture(qkv_a.reshape(bsz, seq, PROJ_A), bias_a).reshape(bsz * seq, A_WIDTH)
    o_b = fox_attention(*fox_prep(proj3, small3, f_bias)).reshape(bsz * seq, B_WIDTH)
    o_c = gla_mixer(proj3, small3, p["w2p"], p["cbp"], p["gnp"]).reshape(bsz * seq, C_HEADS * C_DV_PAD)
    return out_proj(x2d, o_a, o_b, o_c, p["wa"], p["wb"], p["wc"])


def kernel(x, mem, rel_table, mem_norm, norm_final, norm_mix, w_in, f_bias, c_gate_w2, c_gate_b,
           c_norm, w_out, norm_cross, w_cq, w_ckv, w_co, norm_ffn, w_up, conv_w, conv_b, w_down):
    bsz, seq, d = x.shape
    depth = w_in.shape[0]
    mem_len = mem.shape[1]
    assert d == D_MODEL and seq % (A_SPAN * A_PATTERNS[-1][1]) == 0 and seq % 1024 == 0
    bias_a = jnp.stack([_dilated_bias(rel_table, dil) for _, dil in A_PATTERNS], axis=0)
    x2d = x.reshape(bsz * seq, d)
    mem2d = mem.reshape(bsz * mem_len, d)
    for l in range(depth):
        p = _layer_params(l, w_in, c_gate_w2, c_gate_b, c_norm, w_out, w_up, conv_w, conv_b, w_down)
        x2d = hybrid_mixer(x2d, norm_mix[l], p, f_bias[l], bias_a, bsz=bsz, seq=seq)
        kv = norm_matmul(mem2d, mem_norm, w_ckv[l].astype(BF16), tm=min(512, bsz * mem_len), tn=512,
                         name="mem_kv").reshape(bsz, mem_len, 2 * CROSS_WIDTH)
        x2d = cross_attention(x2d, norm_cross[l], w_cq[l].astype(BF16), kv, w_co[l].astype(BF16), seq=seq)
        x2d = conv_ffn(x2d, norm_ffn[l], p["wv"], p["wg"], p["cw"], p["cb"], p["wd"], norm_final,
                       seq=seq, final=l == depth - 1)
    return x2d.reshape(bsz, seq, d)
```

```python
import functools
import math

import numpy as np
import jax
import jax.numpy as jnp
from jax import lax
from jax.experimental import pallas as pl
from jax.experimental.pallas import tpu as pltpu

F32 = jnp.float32
BF16 = jnp.bfloat16

V7X_LANES = 128
V7X_VMEM_LIMIT_BYTES = 56 * 1024 * 1024

D_MODEL = 2048
HEAD_DIM = 64
A_WIDTH = 3 * D_MODEL // 8
A_HEADS = A_WIDTH // HEAD_DIM
A_PATTERNS = ((128, 1), (512, 4), (2048, 16))
A_SPAN = 128
B_WIDTH = D_MODEL // 4
B_HEADS = B_WIDTH // HEAD_DIM
C_HEADS = 4
C_VW = 3 * D_MODEL // 8
C_DV = C_VW // C_HEADS
C_DK = C_DV // 2
C_KW = C_HEADS * C_DK
C_GATE_RANK = 16
C_GATE_TAU = 16.0
C_CHUNK = 64
C_SUB = 16
C_DK_PAD = 128
C_DV_PAD = 256
CROSS_HEADS = 4
CROSS_DH = 128
CROSS_WIDTH = CROSS_HEADS * CROSS_DH
D_FF = ((8 * D_MODEL // 3 + 127) // 128) * 128
FF_TILE = 512
D_FF_PAD = ((D_FF + FF_TILE - 1) // FF_TILE) * FF_TILE
CONV_W = 3
REL_BUCKETS = 32
REL_MAX_DIST = 2048
EPS = 1e-6
NEG = -1e30
LOG2E = 1.4426950408889634

COL_QA = 0
COL_KA = COL_QA + A_WIDTH
COL_VA = COL_KA + A_WIDTH
PROJ_A = COL_VA + A_WIDTH
COL_VC = 0
COL_RC = COL_VC + C_HEADS * C_DV_PAD
COL_QB = COL_RC + C_HEADS * C_DV_PAD
COL_KB = COL_QB + B_WIDTH
COL_VB = COL_KB + B_WIDTH
COL_QC = COL_VB + B_WIDTH
COL_KC = COL_QC + C_HEADS * C_DK_PAD
PROJ_PAD = COL_KC + C_HEADS * C_DK_PAD
SMALL_FL = 0
SMALL_GL = B_HEADS


def _cparams(sem):
    return pltpu.CompilerParams(dimension_semantics=sem,
                                vmem_limit_bytes=V7X_VMEM_LIMIT_BYTES)


def _rms_rows(x, g):
    ms = jnp.mean(x * x, axis=-1, keepdims=True)
    return x * lax.rsqrt(ms + EPS) * g


def _split3(x):
    hi = x.astype(BF16)
    r1 = x - hi.astype(F32)
    mid = r1.astype(BF16)
    lo = (r1 - mid.astype(F32)).astype(BF16)
    return hi, mid, lo


def _log_sigmoid(x):
    return jnp.minimum(x, 0.0) - jnp.log(1.0 + jnp.exp(-jnp.abs(x)))


def _tril_ones(n, dtype):
    r = lax.broadcasted_iota(jnp.int32, (n, n), 0)
    c = lax.broadcasted_iota(jnp.int32, (n, n), 1)
    return jnp.where(r >= c, 1.0, 0.0).astype(dtype)


NORM_ROWS = 256


def _norm_to_scratch(x_ref, g_ref, h_scr):
    def body(r, c):
        sl = pl.ds(pl.multiple_of(r * NORM_ROWS, NORM_ROWS), NORM_ROWS)
        h_scr[sl, :] = _rms_rows(x_ref[sl, :], g_ref[...]).astype(BF16)
        return c
    lax.fori_loop(0, x_ref.shape[0] // NORM_ROWS, body, 0)


def _norm_matmul_kernel(x_ref, g_ref, w_ref, o_ref, h_scr):
    @pl.when(pl.program_id(1) == 0)
    def _():
        _norm_to_scratch(x_ref, g_ref, h_scr)
    o_ref[...] = jnp.dot(h_scr[...], w_ref[...],
                         preferred_element_type=F32).astype(o_ref.dtype)


def _norm_matmul_small_kernel(x_ref, g_ref, w_ref, ws_ref, o_ref, os_ref, h_scr):
    @pl.when(pl.program_id(1) == 0)
    def _():
        _norm_to_scratch(x_ref, g_ref, h_scr)
        os_ref[...] = jnp.dot(h_scr[...], ws_ref[...], preferred_element_type=F32)
    o_ref[...] = jnp.dot(h_scr[...], w_ref[...],
                         preferred_element_type=F32).astype(o_ref.dtype)


def norm_matmul(x2d, gain, w, w_small=None, *, tm, tn, name, out_dtype=BF16):
    m, k = x2d.shape
    n = w.shape[1]
    grid = (m // tm, n // tn)
    in_specs = [pl.BlockSpec((tm, k), lambda i, j: (i, 0)),
                pl.BlockSpec((1, k), lambda i, j: (0, 0)),
                pl.BlockSpec((k, tn), lambda i, j: (0, j))]
    out_specs = pl.BlockSpec((tm, tn), lambda i, j: (i, j))
    out_shape = jax.ShapeDtypeStruct((m, n), out_dtype)
    args = [x2d, gain.reshape(1, k), w]
    kern = _norm_matmul_kernel
    if w_small is not None:
        ns = w_small.shape[1]
        in_specs.append(pl.BlockSpec((k, ns), lambda i, j: (0, 0)))
        out_specs = [out_specs, pl.BlockSpec((tm, ns), lambda i, j: (i, 0))]
        out_shape = [out_shape, jax.ShapeDtypeStruct((m, ns), F32)]
        args.append(w_small)
        kern = _norm_matmul_small_kernel
    return pl.pallas_call(
        kern, grid=grid, in_specs=in_specs, out_specs=out_specs, out_shape=out_shape,
        scratch_shapes=[pltpu.VMEM((tm, k), BF16)],
        compiler_params=_cparams(("parallel", "arbitrary")), name=name)(*args)


def _t5_bucket(dist):
    max_exact = REL_BUCKETS // 2
    d = np.maximum(dist, 1).astype(np.float32)
    large = max_exact + (np.log(d / max_exact) / math.log(REL_MAX_DIST / max_exact)
                         * (REL_BUCKETS - max_exact)).astype(np.int32)
    return np.where(dist < max_exact, dist, np.minimum(large, REL_BUCKETS - 1)).astype(np.int32)


def _dilated_bias(rel_table, dilation):
    span = A_SPAN
    i = np.arange(span)[:, None]
    j = np.arange(2 * span)[None, :]
    step = i + span - j
    valid = (step >= 0) & (step <= span)
    bucket = _t5_bucket(np.clip(step, 0, span) * dilation)
    b = jnp.transpose(rel_table[bucket], (2, 0, 1)).astype(F32)
    later = jnp.where(valid[None], b, NEG)
    first = jnp.where((valid & (j >= span))[None], b, NEG)
    return jnp.stack([first, later], axis=0)


DIL_TILE = A_SPAN * max(d for _, d in A_PATTERNS)
DIL_UNITS = DIL_TILE // A_SPAN


def _dilated_kernel(q_ref, kp_ref, kc_ref, vp_ref, vc_ref, bias_ref, o_ref,
                    kk_scr, vv_scr, m_scr, n_scr, s_scr):
    tile, span = DIL_TILE, A_SPAN
    first_tile = pl.program_id(1) == 0
    kk_scr[0:tile, :] = kp_ref[...]
    kk_scr[tile:2 * tile, :] = kc_ref[...]
    vv_scr[0:tile, :] = vp_ref[...]
    vv_scr[tile:2 * tile, :] = vc_ref[...]
    lane_q = lax.broadcasted_iota(jnp.int32, (span, V7X_LANES), 1)
    lane_k = lax.broadcasted_iota(jnp.int32, (2 * span, V7X_LANES), 1)
    low_q = lane_q < HEAD_DIM
    low_k = lane_k < HEAD_DIM
    scale = HEAD_DIM ** -0.5
    qmask = (jnp.where(low_q, scale, 0.0), jnp.where(low_q, 0.0, scale))
    nt = (((1,), (1,)), ((), ()))

    for pat, (_, d) in enumerate(A_PATTERNS):
        for u in range(DIL_UNITS):
            n, r = divmod(u, d)
            q0 = r + span * d * n
            k0 = tile + q0 - span * d
            if d == 1:
                qrows, krows = pl.ds(q0, span), pl.ds(k0, 2 * span)
            else:
                qrows, krows = pl.ds(q0, span, stride=d), pl.ds(k0, 2 * span, stride=d)
            qf = q_ref[qrows, :]
            kb = kk_scr[krows, :].astype(BF16)
            vf = vv_scr[krows, :]
            sel = jnp.where(first_tile, 0, 1) if n == 0 else 1
            pvs, ms = [], []
            for e in range(2):
                qe = (qf * qmask[e]).astype(BF16)
                logits = lax.dot_general(qe, kb, nt, preferred_element_type=F32) + bias_ref[pat, sel, e]
                m = jnp.max(logits, axis=-1, keepdims=True)
                p = jnp.exp(logits - m)
                ve = (jnp.where(low_k, vf, 1.0) if e == 0 else jnp.where(low_k, 1.0, vf)).astype(BF16)
                pvs.append(jnp.dot(p.astype(BF16), ve, preferred_element_type=F32))
                ms.append(m)
            num = jnp.where(low_q, pvs[0], pvs[1])
            den = pltpu.roll(jnp.where(low_q, pvs[1], pvs[0]), HEAD_DIM, axis=1)
            mx = jnp.where(low_q, ms[0], ms[1])
            if pat > 0:
                m_old = m_scr[qrows, :]
                m_new = jnp.maximum(m_old, mx)
                a = jnp.exp(m_old - m_new)
                b = jnp.exp(mx - m_new)
                num = a * n_scr[qrows, :] + b * num
                den = a * s_scr[qrows, :] + b * den
                mx = m_new
            m_scr[qrows, :] = mx
            n_scr[qrows, :] = num
            s_scr[qrows, :] = den
    o_ref[...] = (n_scr[...] / s_scr[...]).astype(o_ref.dtype)


def dilated_mixture(qkv3, bias):
    bsz, s, _ = qkv3.shape
    pairs = A_HEADS // 2
    blk = (None, DIL_TILE, V7X_LANES)
    ck, cv = COL_KA // V7X_LANES, COL_VA // V7X_LANES

    def cur(col):
        return pl.BlockSpec(blk, lambda b, t, g: (b, t, col + g))

    def prev(col):
        return pl.BlockSpec(blk, lambda b, t, g: (b, jnp.maximum(t - 1, 0), col + g))

    tile_f32 = pltpu.VMEM((DIL_TILE, V7X_LANES), F32)
    both_f32 = pltpu.VMEM((2 * DIL_TILE, V7X_LANES), F32)
    return pl.pallas_call(
        _dilated_kernel, grid=(bsz, s // DIL_TILE, pairs),
        in_specs=[cur(0), prev(ck), cur(ck), prev(cv), cur(cv),
                  pl.BlockSpec((len(A_PATTERNS), 2, 2, A_SPAN, 2 * A_SPAN),
                               lambda b, t, g: (0, 0, g, 0, 0))],
        out_specs=pl.BlockSpec(blk, lambda b, t, g: (b, t, g)),
        out_shape=jax.ShapeDtypeStruct((bsz, s, A_WIDTH), BF16),
        scratch_shapes=[both_f32, both_f32, tile_f32, tile_f32, tile_f32],
        compiler_params=_cparams(("parallel", "parallel", "arbitrary")), name="dilated_mixture",
    )(qkv3, qkv3, qkv3, qkv3, qkv3, bias)


FOX_PREP_ROWS = 512
FOX_TQ = 512
FOX_TK = 512
FOX_HEADS_PER_STEP = 4


def _fox_prep_kernel(sm_ref, fb_ref, q0_ref, q1_ref, k0_ref, k1_ref, v0_ref, v1_ref,
                     qa_ref, ka_ref, va_ref, carry_scr):
    t = sm_ref.shape[0]

    @pl.when(pl.program_id(1) == 0)
    def _():
        carry_scr[...] = jnp.zeros_like(carry_scr)

    x = _log_sigmoid(sm_ref[...] + fb_ref[...])
    tri = _tril_ones(t, BF16)
    hi, mid, lo = _split3(x)
    c = (jnp.dot(tri, hi, preferred_element_type=F32)
         + jnp.dot(tri, mid, preferred_element_type=F32)
         + jnp.dot(tri, lo, preferred_element_type=F32)) + carry_scr[0:1, :]
    carry_scr[...] = jnp.broadcast_to(c[t - 1:t, :], carry_scr.shape)
    chi, cmid, clo = [p.astype(F32) for p in _split3(c * LOG2E)]

    lane = lax.broadcasted_iota(jnp.int32, (t, V7X_LANES), 1)
    low = lane < HEAD_DIM
    scale = HEAD_DIM ** -0.5 * LOG2E
    q_refs, k_refs, v_refs = (q0_ref, q1_ref), (k0_ref, k1_ref), (v0_ref, v1_ref)
    for h in range(B_HEADS):
        src, pair = divmod(h // 2, 2)
        cs = slice(V7X_LANES * pair, V7X_LANES * (pair + 1))
        qp = q_refs[src][:, cs].astype(F32)
        kp = k_refs[src][:, cs].astype(F32)
        vp = v_refs[src][:, cs].astype(F32)
        if h % 2 == 1:
            qp = pltpu.roll(qp, HEAD_DIM, axis=1)
            kp = pltpu.roll(kp, HEAD_DIM, axis=1)
        c1, c2, c3 = chi[:, h:h + 1], cmid[:, h:h + 1], clo[:, h:h + 1]
        one = jnp.ones((t, 1), F32)
        zero = jnp.zeros((t, 1), F32)
        q_aug, k_aug = zero, zero
        for idx, (qv, kv) in enumerate(((c1, one), (c2, one), (c3, one),
                                        (one, -c1), (one, -c2), (one, -c3))):
            hit = lane == HEAD_DIM + idx
            q_aug = jnp.where(hit, qv, q_aug)
            k_aug = jnp.where(hit, kv, k_aug)
        qa_ref[h] = jnp.where(low, qp * scale, q_aug).astype(BF16)
        ka_ref[h] = jnp.where(low, kp, k_aug).astype(BF16)
        if h % 2 == 0:
            va_ref[h] = jnp.where(low, vp, 1.0).astype(BF16)
        else:
            va_ref[h] = jnp.where(low, 1.0, vp).astype(BF16)


def fox_prep(proj3, small3, f_bias):
    bsz, s, _ = proj3.shape
    t = FOX_PREP_ROWS
    half = B_WIDTH // 2
    fb = jnp.zeros((1, V7X_LANES), F32).at[0, SMALL_FL:SMALL_FL + B_HEADS].set(f_bias)

    def colspec(col):
        return pl.BlockSpec((None, t, half), lambda b, i, c=col // half: (b, i, c))

    aug_spec = pl.BlockSpec((None, B_HEADS, t, V7X_LANES), lambda b, i: (b, 0, i, 0))
    aug_shape = jax.ShapeDtypeStruct((bsz, B_HEADS, s, V7X_LANES), BF16)
    return pl.pallas_call(
        _fox_prep_kernel, grid=(bsz, s // t),
        in_specs=[pl.BlockSpec((None, t, V7X_LANES), lambda b, i: (b, i, 0)),
                  pl.BlockSpec((1, V7X_LANES), lambda b, i: (0, 0)),
                  colspec(COL_QB), colspec(COL_QB + half),
                  colspec(COL_KB), colspec(COL_KB + half),
                  colspec(COL_VB), colspec(COL_VB + half)],
        out_specs=[aug_spec, aug_spec, aug_spec],
        out_shape=[aug_shape, aug_shape, aug_shape],
        scratch_shapes=[pltpu.VMEM((8, V7X_LANES), F32)],
        compiler_params=_cparams(("parallel", "arbitrary")), name="fox_prep",
    )(small3, fb, proj3, proj3, proj3, proj3, proj3, proj3)


def _fox_kernel(q_ref, k_ref, v_ref, o_ref):
    tq, tk = FOX_TQ, FOX_TK
    qi = pl.program_id(2)
    nfull = qi * (tq // tk)
    row = lax.broadcasted_iota(jnp.int32, (tq, tk), 0)
    col = lax.broadcasted_iota(jnp.int32, (tq, tk), 1)
    lane = lax.broadcasted_iota(jnp.int32, (tq, V7X_LANES), 1)
    nh = q_ref.shape[0]
    qs = [q_ref[e] for e in range(nh)]

    def tile(e, kt, m, acc, mask):
        rows = pl.ds(pl.multiple_of(kt * tk, tk), tk)
        s = lax.dot_general(qs[e], k_ref[e, rows, :], (((1,), (1,)), ((), ())),
                            preferred_element_type=F32)
        if mask is not None:
            s = jnp.where(mask, s, NEG)
        m_new = jnp.maximum(m, jnp.max(s, axis=-1, keepdims=True))
        alpha = jnp.exp2(m - m_new)
        p = jnp.exp2(s - m_new)
        acc = alpha * acc + jnp.dot(p.astype(BF16), v_ref[e, rows, :], preferred_element_type=F32)
        return m_new, acc

    def heads(kt, carry, mask):
        out = []
        for e in range(nh):
            out += tile(e, kt, carry[2 * e], carry[2 * e + 1], mask)
        return tuple(out)

    m0 = jnp.full((tq, 1), NEG, F32)
    acc0 = jnp.zeros((tq, V7X_LANES), F32)
    carry = lax.fori_loop(0, nfull, lambda kt, c: heads(kt, c, None), (m0, acc0) * nh)
    for dd in range(tq // tk):
        carry = heads(nfull + dd, carry, col + dd * tk <= row)
    for g in range(nh // 2):
        even, odd = carry[4 * g + 1], carry[4 * g + 3]
        res = [acc / pltpu.roll(acc, HEAD_DIM, axis=1) for acc in (even, odd)]
        o_ref[:, V7X_LANES * g:V7X_LANES * (g + 1)] = jnp.where(
            lane < HEAD_DIM, res[0], res[1]).astype(o_ref.dtype)


def fox_attention(q_aug, k_aug, v_aug):
    bsz, nh, s, _ = q_aug.shape
    tq, hs = FOX_TQ, FOX_HEADS_PER_STEP
    return pl.pallas_call(
        _fox_kernel, grid=(bsz, nh // hs, s // tq),
        in_specs=[pl.BlockSpec((None, hs, tq, V7X_LANES), lambda b, g, i: (b, g, i, 0)),
                  pl.BlockSpec((None, hs, s, V7X_LANES), lambda b, g, i: (b, g, 0, 0)),
                  pl.BlockSpec((None, hs, s, V7X_LANES), lambda b, g, i: (b, g, 0, 0))],
        out_specs=pl.BlockSpec((None, tq, hs * HEAD_DIM), lambda b, g, i: (b, i, g)),
        out_shape=jax.ShapeDtypeStruct((bsz, s, B_WIDTH), BF16),
        compiler_params=_cparams(("parallel", "parallel", "arbitrary")), name="fox_attention",
    )(q_aug, k_aug, v_aug)


GLA_ROWS = 256


def _gla_head_chunk(h, rows, q_ref, k_ref, v_ref, r_ref, sm_bf, w2_ref, cb_ref, gn_ref, o_ref,
                    st_scr, b_scr, k_scr, v_scr, tri, row_sub, row_ck):
    ck, sub = C_CHUNK, C_SUB
    nt = (((1,), (1,)), ((), ()))
    kcols = slice(C_DK_PAD * h, C_DK_PAD * (h + 1))
    vcols = slice(C_DV_PAD * h, C_DV_PAD * (h + 1))
    g = jnp.dot(sm_bf, w2_ref[h], preferred_element_type=F32) + cb_ref[h]
    la = _log_sigmoid(g) / C_GATE_TAU
    hi, mid, lo = _split3(la)
    b = (jnp.dot(tri, hi, preferred_element_type=F32)
         + jnp.dot(tri, mid, preferred_element_type=F32)
         + jnp.dot(tri, lo, preferred_element_type=F32))
    qf = q_ref[rows, kcols].astype(F32) * (C_DK ** -0.5)
    kf = k_ref[rows, kcols].astype(F32)
    vb = v_ref[rows, vcols]
    b_scr[h] = b
    k_scr[h] = kf
    v_scr[h] = vb.astype(F32)
    st = st_scr[h]
    o_inter = lax.dot_general((qf * jnp.exp(b)).astype(BF16), st.astype(BF16), nt,
                              preferred_element_type=F32)
    outs = []
    for blk in range(ck // sub):
        r0 = blk * sub
        qb = qf[r0:r0 + sub, :]
        bb = b[r0:r0 + sub, :]
        ob = o_inter[r0:r0 + sub, :]
        if blk > 0:
            ref_row = b_scr[h, r0 - 1:r0, :]
            kd = jnp.where(row_ck < r0, kf * jnp.exp(jnp.minimum(ref_row - b, 0.0)), 0.0)
            a = lax.dot_general((qb * jnp.exp(bb - ref_row)).astype(BF16), kd.astype(BF16), nt,
                                preferred_element_type=F32)
            ob = ob + jnp.dot(a.astype(BF16), vb, preferred_element_type=F32)
        for j in range(sub):
            bj = b_scr[h, r0 + j:r0 + j + 1, :]
            kj = k_scr[h, r0 + j:r0 + j + 1, :]
            vj = v_scr[h, r0 + j:r0 + j + 1, :]
            x = qb * kj * jnp.exp(jnp.minimum(bb - bj, 0.0))
            colv = jnp.sum(x, axis=-1, keepdims=True)
            colv = jnp.where(row_sub >= j, colv, 0.0)
            ob = ob + colv * vj
        outs.append(ob)
    o = jnp.concatenate(outs, axis=0)
    bl = b_scr[h, ck - 1:ck, :]
    kdec = (kf * jnp.exp(bl - b)).astype(BF16)
    st_scr[h] = st * jnp.exp(bl) + lax.dot_general(
        vb, kdec, (((0,), (0,)), ((), ())), preferred_element_type=F32)
    ms = jnp.sum(o * o, axis=-1, keepdims=True) * (1.0 / C_DV)
    on = o * lax.rsqrt(ms + EPS) * gn_ref[...]
    rr = r_ref[rows, vcols].astype(F32)
    o_ref[rows, vcols] = (on * (rr / (1.0 + jnp.exp(-rr)))).astype(o_ref.dtype)


def _gla_kernel(q_ref, k_ref, v_ref, r_ref, sm_ref, w2_ref, cb_ref, gn_ref, o_ref,
                st_scr, b_scr, k_scr, v_scr):
    ck, sub = C_CHUNK, C_SUB

    @pl.when(pl.program_id(1) == 0)
    def _():
        st_scr[...] = jnp.zeros_like(st_scr)

    tri = _tril_ones(ck, BF16)
    row_sub = lax.broadcasted_iota(jnp.int32, (sub, 1), 0)
    row_ck = lax.broadcasted_iota(jnp.int32, (ck, 1), 0)

    def chunk(c, carry):
        rows = pl.ds(pl.multiple_of(c * ck, ck), ck)
        sm_bf = sm_ref[rows, :].astype(BF16)
        for h in range(C_HEADS):
            _gla_head_chunk(h, rows, q_ref, k_ref, v_ref, r_ref, sm_bf, w2_ref, cb_ref, gn_ref, o_ref,
                            st_scr, b_scr, k_scr, v_scr, tri, row_sub, row_ck)
        return carry

    lax.fori_loop(0, q_ref.shape[0] // ck, chunk, 0)


def gla_mixer(proj3, small3, w2p, cbp, gnp):
    bsz, s, _ = proj3.shape
    t = GLA_ROWS
    kw, vw = C_HEADS * C_DK_PAD, C_HEADS * C_DV_PAD

    def col(base, width):
        return pl.BlockSpec((None, t, width), lambda b, i, c=base // width: (b, i, c))

    return pl.pallas_call(
        _gla_kernel, grid=(bsz, s // t),
        in_specs=[col(COL_QC, kw), col(COL_KC, kw), col(COL_VC, vw), col(COL_RC, vw),
                  pl.BlockSpec((None, t, V7X_LANES), lambda b, i: (b, i, 0)),
                  pl.BlockSpec((C_HEADS, V7X_LANES, C_DK_PAD), lambda b, i: (0, 0, 0)),
                  pl.BlockSpec((C_HEADS, 1, C_DK_PAD), lambda b, i: (0, 0, 0)),
                  pl.BlockSpec((1, C_DV_PAD), lambda b, i: (0, 0))],
        out_specs=pl.BlockSpec((None, t, vw), lambda b, i: (b, i, 0)),
        out_shape=jax.ShapeDtypeStruct((bsz, s, vw), BF16),
        scratch_shapes=[pltpu.VMEM((C_HEADS, C_DV_PAD, C_DK_PAD), F32),
                        pltpu.VMEM((C_HEADS, C_CHUNK, C_DK_PAD), F32),
                        pltpu.VMEM((C_HEADS, C_CHUNK, C_DK_PAD), F32),
                        pltpu.VMEM((C_HEADS, C_CHUNK, C_DV_PAD), F32)],
        compiler_params=_cparams(("parallel", "arbitrary")), name="gla_mixer",
    )(proj3, proj3, proj3, proj3, small3, w2p, cbp, gnp)


def _out_proj_kernel(x_ref, a_ref, b_ref, c_ref, wa_ref, wb_ref, wc_ref, o_ref):
    acc = jnp.dot(a_ref[...], wa_ref[...], preferred_element_type=F32)
    acc += jnp.dot(b_ref[...], wb_ref[...], preferred_element_type=F32)
    acc += jnp.dot(c_ref[...], wc_ref[...], preferred_element_type=F32)
    o_ref[...] = x_ref[...] + acc


def out_proj(x2d, oa, ob, oc, wa, wb, wc, *, tm=1024, tn=512):
    m, n = x2d.shape

    def rows(a):
        return pl.BlockSpec((tm, a.shape[1]), lambda i, j: (i, 0))

    def cols(w):
        return pl.BlockSpec((w.shape[0], tn), lambda i, j: (0, j))

    xs = pl.BlockSpec((tm, tn), lambda i, j: (i, j))
    return pl.pallas_call(
        _out_proj_kernel, grid=(m // tm, n // tn),
        in_specs=[xs, rows(oa), rows(ob), rows(oc), cols(wa), cols(wb), cols(wc)],
        out_specs=xs, out_shape=jax.ShapeDtypeStruct((m, n), F32),
        compiler_params=_cparams(("parallel", "arbitrary")), name="out_proj",
    )(x2d, oa, ob, oc, wa, wb, wc)


CROSS_ROWS = 512


def _cross_kernel(x_ref, g_ref, wq_ref, k_ref, v_ref, wo_ref, o_ref):
    x = x_ref[...]
    h = _rms_rows(x, g_ref[...]).astype(BF16)
    q = jnp.dot(h, wq_ref[...], preferred_element_type=F32).astype(BF16)
    heads = []
    for hd in range(CROSS_HEADS):
        cs = slice(CROSS_DH * hd, CROSS_DH * (hd + 1))
        logits = lax.dot_general(q[:, cs], k_ref[:, cs], (((1,), (1,)), ((), ())),
                                 preferred_element_type=F32) * (CROSS_DH ** -0.5)
        m = jnp.max(logits, axis=-1, keepdims=True)
        p = jnp.exp(logits - m)
        ssum = jnp.sum(p, axis=-1, keepdims=True)
        pv = jnp.dot(p.astype(BF16), v_ref[:, cs], preferred_element_type=F32)
        heads.append((pv / ssum).astype(BF16))
    o = jnp.concatenate(heads, axis=1)
    o_ref[...] = x + jnp.dot(o, wo_ref[...], preferred_element_type=F32)


def cross_attention(x2d, gain, wq, kv, wo, *, seq):
    m, d = x2d.shape
    tm = CROSS_ROWS
    mem_len = kv.shape[1]
    per_batch = seq // tm
    return pl.pallas_call(
        _cross_kernel, grid=(m // tm,),
        in_specs=[pl.BlockSpec((tm, d), lambda i: (i, 0)),
                  pl.BlockSpec((1, d), lambda i: (0, 0)),
                  pl.BlockSpec((d, CROSS_WIDTH), lambda i: (0, 0)),
                  pl.BlockSpec((None, mem_len, CROSS_WIDTH), lambda i: (i // per_batch, 0, 0)),
                  pl.BlockSpec((None, mem_len, CROSS_WIDTH), lambda i: (i // per_batch, 0, 1)),
                  pl.BlockSpec((CROSS_WIDTH, d), lambda i: (0, 0))],
        out_specs=pl.BlockSpec((tm, d), lambda i: (i, 0)),
        out_shape=jax.ShapeDtypeStruct((m, d), F32),
        compiler_params=_cparams(("parallel",)), name="cross_attention",
    )(x2d, gain.reshape(1, d), wq, kv, kv, wo)


FFN_ROWS = 512
FFN_CHUNK = 128
FFN_HALO = 16


def _ffn_kernel(x_ref, xh_ref, g_ref, wv_ref, wg_ref, cw_ref, cb_ref, wd_ref, gf_ref, o_ref,
                h_scr, hh_scr, *, final):
    j = pl.program_id(1)

    @pl.when(j == 0)
    def _():
        _norm_to_scratch(x_ref, g_ref, h_scr)
        hh_scr[...] = _rms_rows(xh_ref[...], g_ref[...]).astype(BF16)
        o_ref[...] = x_ref[...]

    rc = FFN_CHUNK
    nchunk = h_scr.shape[0] // rc
    prev = jnp.dot(hh_scr[...], wg_ref[...], preferred_element_type=F32)
    row = lax.broadcasted_iota(jnp.int32, (rc, wg_ref.shape[1]), 0)
    vals, gates = [], []
    for c in range(nchunk):
        h = h_scr[c * rc:(c + 1) * rc, :]
        vals.append(jnp.dot(h, wv_ref[...], preferred_element_type=F32))
        gates.append(jnp.dot(h, wg_ref[...], preferred_element_type=F32))
    for c in range(nchunk):
        rows = slice(c * rc, (c + 1) * rc)
        val, gate = vals[c], gates[c]
        prev1 = prev[prev.shape[0] - 1:, :]
        prev2 = prev[prev.shape[0] - 2:prev.shape[0] - 1, :]
        g1 = jnp.where(row == 0, prev1, pltpu.roll(gate, 1, axis=0))
        g2 = jnp.where(row == 0, prev2, jnp.where(row == 1, prev1, pltpu.roll(gate, 2, axis=0)))
        gc = cb_ref[...] + cw_ref[0:1, :] * g2
        gc = gc + cw_ref[1:2, :] * g1
        gc = gc + cw_ref[2:3, :] * gate
        act = (gc / (1.0 + jnp.exp(-gc)) * val).astype(BF16)
        o_ref[rows, :] += jnp.dot(act, wd_ref[...], preferred_element_type=F32)
        prev = gate

    if final:
        @pl.when(j == pl.num_programs(1) - 1)
        def _():
            def body(r, c):
                sl = pl.ds(pl.multiple_of(r * NORM_ROWS, NORM_ROWS), NORM_ROWS)
                o_ref[sl, :] = _rms_rows(o_ref[sl, :], gf_ref[...])
                return c
            lax.fori_loop(0, o_ref.shape[0] // NORM_ROWS, body, 0)


def conv_ffn(x2d, gain, wv, wg, cw, cb, wd, gain_final, *, seq, final):
    m, d = x2d.shape
    tm, tf = FFN_ROWS, FF_TILE
    nt = m // tm
    tail = x2d.reshape(nt, tm, d)[:, tm - FFN_HALO:, :]
    halo = jnp.concatenate([jnp.zeros_like(tail[:1]), tail[:-1]], axis=0)
    starts = (jnp.arange(nt) % (seq // tm) == 0)[:, None, None]
    halo = jnp.where(starts, 0.0, halo)
    ff = wv.shape[1]
    return pl.pallas_call(
        functools.partial(_ffn_kernel, final=final), grid=(nt, ff // tf),
        in_specs=[pl.BlockSpec((tm, d), lambda i, j: (i, 0)),
                  pl.BlockSpec((None, FFN_HALO, d), lambda i, j: (i, 0, 0)),
                  pl.BlockSpec((1, d), lambda i, j: (0, 0)),
                  pl.BlockSpec((d, tf), lambda i, j: (0, j)),
                  pl.BlockSpec((d, tf), lambda i, j: (0, j)),
                  pl.BlockSpec((CONV_W, tf), lambda i, j: (0, j)),
                  pl.BlockSpec((1, tf), lambda i, j: (0, j)),
                  pl.BlockSpec((tf, d), lambda i, j: (j, 0)),
                  pl.BlockSpec((1, d), lambda i, j: (0, 0))],
        out_specs=pl.BlockSpec((tm, d), lambda i, j: (i, 0)),
        out_shape=jax.ShapeDtypeStruct((m, d), F32),
        scratch_shapes=[pltpu.VMEM((tm, d), BF16), pltpu.VMEM((FFN_HALO, d), BF16)],
        compiler_params=_cparams(("parallel", "arbitrary")), name="conv_ffn",
    )(x2d, halo, gain.reshape(1, d), wv, wg, cw, cb.reshape(1, ff), wd, gain_final.reshape(1, d))


def _pad_heads(w, heads, width, padded):
    lead = w.shape[:-1]
    w = w.reshape(lead + (heads, width))
    w = jnp.pad(w, [(0, 0)] * len(lead) + [(0, 0), (0, padded - width)])
    return w.reshape(lead + (heads * padded,))


def _layer_params(l, w_in, c_gate_w2, c_gate_b, c_norm, w_out, w_up, conv_w, conv_b, w_down):
    splits = np.cumsum([A_WIDTH] * 3 + [B_WIDTH] * 3 + [B_HEADS, C_KW, C_KW, C_VW, C_VW, C_GATE_RANK])
    w_a = w_in[l][:, :splits[2]].astype(BF16)
    fl, qc, kc, vc, rc, gl = jnp.split(w_in[l][:, splits[5]:], (splits[6:11] - splits[5]).tolist(), axis=1)
    w_main = jnp.concatenate([
        _pad_heads(vc, C_HEADS, C_DV, C_DV_PAD), _pad_heads(rc, C_HEADS, C_DV, C_DV_PAD),
        w_in[l][:, splits[2]:splits[5]],
        _pad_heads(qc, C_HEADS, C_DK, C_DK_PAD), _pad_heads(kc, C_HEADS, C_DK, C_DK_PAD)],
        axis=1).astype(BF16)
    w_small = jnp.zeros((D_MODEL, V7X_LANES), F32)
    w_small = w_small.at[:, SMALL_FL:SMALL_FL + B_HEADS].set(fl)
    w_small = w_small.at[:, SMALL_GL:SMALL_GL + C_GATE_RANK].set(gl).astype(BF16)
    w2 = _pad_heads(c_gate_w2[l], C_HEADS, C_DK, C_DK_PAD).reshape(C_GATE_RANK, C_HEADS, C_DK_PAD)
    w2p = jnp.zeros((C_HEADS, V7X_LANES, C_DK_PAD), F32)
    w2p = w2p.at[:, SMALL_GL:SMALL_GL + C_GATE_RANK, :].set(jnp.transpose(w2, (1, 0, 2))).astype(BF16)
    cbp = _pad_heads(c_gate_b[l], C_HEADS, C_DK, C_DK_PAD).reshape(C_HEADS, 1, C_DK_PAD)
    gnp = jnp.pad(c_norm[l], (0, C_DV_PAD - C_DV)).reshape(1, C_DV_PAD)
    wo = w_out[l]
    wa = wo[:A_WIDTH].astype(BF16)
    wb = wo[A_WIDTH:A_WIDTH + B_WIDTH].astype(BF16)
    wc = wo[A_WIDTH + B_WIDTH:].reshape(C_HEADS, C_DV, D_MODEL)
    wc = jnp.pad(wc, ((0, 0), (0, C_DV_PAD - C_DV), (0, 0))).reshape(C_HEADS * C_DV_PAD, D_MODEL).astype(BF16)
    fpad = D_FF_PAD - D_FF
    wv = jnp.pad(w_up[l][:, :D_FF], ((0, 0), (0, fpad))).astype(BF16)
    wg = jnp.pad(w_up[l][:, D_FF:], ((0, 0), (0, fpad))).astype(BF16)
    cw = jnp.pad(conv_w[l], ((0, 0), (0, fpad)))
    cb = jnp.pad(conv_b[l], (0, fpad))
    wd = jnp.pad(w_down[l], ((0, fpad), (0, 0))).astype(BF16)
    return dict(w_a=w_a, w_main=w_main, w_small=w_small, w2p=w2p, cbp=cbp, gnp=gnp,
                wa=wa, wb=wb, wc=wc, wv=wv, wg=wg, cw=cw, cb=cb, wd=wd)


def hybrid_mixer(x2d, gain, p, f_bias, bias_a, *, bsz, seq):
    qkv_a = norm_matmul(x2d, gain, p["w_a"], tm=1024, tn=A_WIDTH, name="in_proj_a", out_dtype=F32)
    proj, small = norm_matmul(x2d, gain, p["w_main"], p["w_small"], tm=1024, tn=A_WIDTH, name="in_proj")
    proj3 = proj.reshape(bsz, seq, PROJ_PAD)
    small3 = small.reshape(bsz, seq, V7X_LANES)
    o_a = dilated_mixture(qkv_a.reshape(bsz, seq, PROJ_A), bias_a).reshape(bsz * seq, A_WIDTH)
    o_b = fox_attention(*fox_prep(proj3, small3, f_bias)).reshape(bsz * seq, B_WIDTH)
    o_c = gla_mixer(proj3, small3, p["w2p"], p["cbp"], p["gnp"]).reshape(bsz * seq, C_HEADS * C_DV_PAD)
    return out_proj(x2d, o_a, o_b, o_c, p["wa"], p["wb"], p["wc"])


def kernel(x, mem, rel_table, mem_norm, norm_final, norm_mix, w_in, f_bias, c_gate_w2, c_gate_b,
           c_norm, w_out, norm_cross, w_cq, w_ckv, w_co, norm_ffn, w_up, conv_w, conv_b, w_down):
    bsz, seq, d = x.shape
    depth = w_in.shape[0]
    mem_len = mem.shape[1]
    assert d == D_MODEL and seq % (A_SPAN * A_PATTERNS[-1][1]) == 0 and seq % 1024 == 0
    bias_a = jnp.stack([_dilated_bias(rel_table, dil) for _, dil in A_PATTERNS], axis=0)
    x2d = x.reshape(bsz * seq, d)
    mem2d = mem.reshape(bsz * mem_len, d)
    for l in range(depth):
        p = _layer_params(l, w_in, c_gate_w2, c_gate_b, c_norm, w_out, w_up, conv_w, conv_b, w_down)
        x2d = hybrid_mixer(x2d, norm_mix[l], p, f_bias[l], bias_a, bsz=bsz, seq=seq)
        kv = norm_matmul(mem2d, mem_norm, w_ckv[l].astype(BF16), tm=min(512, bsz * mem_len), tn=512,
                         name="mem_kv").reshape(bsz, mem_len, 2 * CROSS_WIDTH)
        x2d = cross_attention(x2d, norm_cross[l], w_cq[l].astype(BF16), kv, w_co[l].astype(BF16), seq=seq)
        x2d = conv_ffn(x2d, norm_ffn[l], p["wv"], p["wg"], p["cw"], p["cb"], p["wd"], norm_final,
                       seq=seq, final=l == depth - 1)
    return x2d.reshape(bsz, seq, d)
```

```python
import functools
import math

import numpy as np
import jax
import jax.numpy as jnp
from jax import lax
from jax.experimental import pallas as pl
from jax.experimental.pallas import tpu as pltpu

F32 = jnp.float32
BF16 = jnp.bfloat16

V7X_LANES = 128
V7X_VMEM_LIMIT_BYTES = 56 * 1024 * 1024

D_MODEL = 2048
HEAD_DIM = 64
A_WIDTH = 3 * D_MODEL // 8
A_HEADS = A_WIDTH // HEAD_DIM
A_PATTERNS = ((128, 1), (512, 4), (2048, 16))
A_SPAN = 128
B_WIDTH = D_MODEL // 4
B_HEADS = B_WIDTH // HEAD_DIM
C_HEADS = 4
C_VW = 3 * D_MODEL // 8
C_DV = C_VW // C_HEADS
C_DK = C_DV // 2
C_KW = C_HEADS * C_DK
C_GATE_RANK = 16
C_GATE_TAU = 16.0
C_CHUNK = 64
C_SUB = 16
C_DK_PAD = 128
C_DV_PAD = 256
CROSS_HEADS = 4
CROSS_DH = 128
CROSS_WIDTH = CROSS_HEADS * CROSS_DH
D_FF = ((8 * D_MODEL // 3 + 127) // 128) * 128
FF_TILE = 512
D_FF_PAD = ((D_FF + FF_TILE - 1) // FF_TILE) * FF_TILE
CONV_W = 3
REL_BUCKETS = 32
REL_MAX_DIST = 2048
EPS = 1e-6
NEG = -1e30
LOG2E = 1.4426950408889634

COL_QA = 0
COL_KA = COL_QA + A_WIDTH
COL_VA = COL_KA + A_WIDTH
PROJ_A = COL_VA + A_WIDTH
COL_VC = 0
COL_RC = COL_VC + C_HEADS * C_DV_PAD
COL_QB = COL_RC + C_HEADS * C_DV_PAD
COL_KB = COL_QB + B_WIDTH
COL_VB = COL_KB + B_WIDTH
COL_QC = COL_VB + B_WIDTH
COL_KC = COL_QC + C_HEADS * C_DK_PAD
PROJ_PAD = COL_KC + C_HEADS * C_DK_PAD
SMALL_FL = 0
SMALL_GL = B_HEADS


def _cparams(sem):
    return pltpu.CompilerParams(dimension_semantics=sem,
                                vmem_limit_bytes=V7X_VMEM_LIMIT_BYTES)


def _rms_rows(x, g):
    ms = jnp.mean(x * x, axis=-1, keepdims=True)
    return x * lax.rsqrt(ms + EPS) * g


def _split3(x):
    hi = x.astype(BF16)
    r1 = x - hi.astype(F32)
    mid = r1.astype(BF16)
    lo = (r1 - mid.astype(F32)).astype(BF16)
    return hi, mid, lo


def _log_sigmoid(x):
    return jnp.minimum(x, 0.0) - jnp.log(1.0 + jnp.exp(-jnp.abs(x)))


def _tril_ones(n, dtype):
    r = lax.broadcasted_iota(jnp.int32, (n, n), 0)
    c = lax.broadcasted_iota(jnp.int32, (n, n), 1)
    return jnp.where(r >= c, 1.0, 0.0).astype(dtype)


NORM_ROWS = 256


def _norm_to_scratch(x_ref, g_ref, h_scr):
    def body(r, c):
        sl = pl.ds(pl.multiple_of(r * NORM_ROWS, NORM_ROWS), NORM_ROWS)
        h_scr[sl, :] = _rms_rows(x_ref[sl, :], g_ref[...]).astype(BF16)
        return c
    lax.fori_loop(0, x_ref.shape[0] // NORM_ROWS, body, 0)


def _norm_matmul_kernel(x_ref, g_ref, w_ref, o_ref, h_scr):
    @pl.when(pl.program_id(1) == 0)
    def _():
        _norm_to_scratch(x_ref, g_ref, h_scr)
    o_ref[...] = jnp.dot(h_scr[...], w_ref[...],
                         preferred_element_type=F32).astype(o_ref.dtype)


def _norm_matmul_small_kernel(x_ref, g_ref, w_ref, ws_ref, o_ref, os_ref, h_scr):
    @pl.when(pl.program_id(1) == 0)
    def _():
        _norm_to_scratch(x_ref, g_ref, h_scr)
        os_ref[...] = jnp.dot(h_scr[...], ws_ref[...], preferred_element_type=F32)
    o_ref[...] = jnp.dot(h_scr[...], w_ref[...],
                         preferred_element_type=F32).astype(o_ref.dtype)


def norm_matmul(x2d, gain, w, w_small=None, *, tm, tn, name, out_dtype=BF16):
    m, k = x2d.shape
    n = w.shape[1]
    grid = (m // tm, n // tn)
    in_specs = [pl.BlockSpec((tm, k), lambda i, j: (i, 0)),
                pl.BlockSpec((1, k), lambda i, j: (0, 0)),
                pl.BlockSpec((k, tn), lambda i, j: (0, j))]
    out_specs = pl.BlockSpec((tm, tn), lambda i, j: (i, j))
    out_shape = jax.ShapeDtypeStruct((m, n), out_dtype)
    args = [x2d, gain.reshape(1, k), w]
    kern = _norm_matmul_kernel
    if w_small is not None:
        ns = w_small.shape[1]
        in_specs.append(pl.BlockSpec((k, ns), lambda i, j: (0, 0)))
        out_specs = [out_specs, pl.BlockSpec((tm, ns), lambda i, j: (i, 0))]
        out_shape = [out_shape, jax.ShapeDtypeStruct((m, ns), F32)]
        args.append(w_small)
        kern = _norm_matmul_small_kernel
    return pl.pallas_call(
        kern, grid=grid, in_specs=in_specs, out_specs=out_specs, out_shape=out_shape,
        scratch_shapes=[pltpu.VMEM((tm, k), BF16)],
        compiler_params=_cparams(("parallel", "arbitrary")), name=name)(*args)


def _t5_bucket(dist):
    max_exact = REL_BUCKETS // 2
    d = np.maximum(dist, 1).astype(np.float32)
    large = max_exact + (np.log(d / max_exact) / math.log(REL_MAX_DIST / max_exact)
                         * (REL_BUCKETS - max_exact)).astype(np.int32)
    return np.where(dist < max_exact, dist, np.minimum(large, REL_BUCKETS - 1)).astype(np.int32)


def _dilated_bias(rel_table, dilation):
    span = A_SPAN
    i = np.arange(span)[:, None]
    j = np.arange(2 * span)[None, :]
    step = i + span - j
    valid = (step >= 0) & (step <= span)
    bucket = _t5_bucket(np.clip(step, 0, span) * dilation)
    b = jnp.transpose(rel_table[bucket], (2, 0, 1)).astype(F32)
    later = jnp.where(valid[None], b, NEG)
    first = jnp.where((valid & (j >= span))[None], b, NEG)
    return jnp.stack([first, later], axis=0)


DIL_TILE = A_SPAN * max(d for _, d in A_PATTERNS)
DIL_UNITS = DIL_TILE // A_SPAN
DIL_GROUP = 4


def _dilated_kernel(q_ref, kp_ref, kc_ref, vp_ref, vc_ref, bias_ref, o_ref,
                    kk_scr, vv_scr, m_scr, n_scr, s_scr):
    tile, span = DIL_TILE, A_SPAN
    first_tile = pl.program_id(1) == 0
    kk_scr[0:tile, :] = kp_ref[...]
    kk_scr[tile:2 * tile, :] = kc_ref[...]
    vv_scr[0:tile, :] = vp_ref[...]
    vv_scr[tile:2 * tile, :] = vc_ref[...]
    lane_q = lax.broadcasted_iota(jnp.int32, (span, V7X_LANES), 1)
    lane_k = lax.broadcasted_iota(jnp.int32, (2 * span, V7X_LANES), 1)
    low_q = lane_q < HEAD_DIM
    low_k = lane_k < HEAD_DIM
    scale = HEAD_DIM ** -0.5
    qmask = (jnp.where(low_q, scale, 0.0), jnp.where(low_q, 0.0, scale))
    nt = (((1,), (1,)), ((), ()))

    for pat, (_, d) in enumerate(A_PATTERNS):
        for u0 in range(0, DIL_UNITS, DIL_GROUP):
            units = []
            for u in range(u0, u0 + DIL_GROUP):
                n, r = divmod(u, d)
                q0 = r + span * d * n
                k0 = tile + q0 - span * d
                if d == 1:
                    qrows, krows = pl.ds(q0, span), pl.ds(k0, 2 * span)
                else:
                    qrows, krows = pl.ds(q0, span, stride=d), pl.ds(k0, 2 * span, stride=d)
                qf = q_ref[qrows, :]
                kb = kk_scr[krows, :].astype(BF16)
                sel = jnp.where(first_tile, 0, 1) if n == 0 else 1
                logits = [lax.dot_general((qf * qmask[e]).astype(BF16), kb, nt,
                                          preferred_element_type=F32) + bias_ref[pat, sel, e]
                          for e in range(2)]
                units.append((qrows, krows, logits))
            results = []
            for qrows, krows, logits in units:
                vf = vv_scr[krows, :]
                pvs, ms = [], []
                for e in range(2):
                    m = jnp.max(logits[e], axis=-1, keepdims=True)
                    p = jnp.exp(logits[e] - m)
                    ve = (jnp.where(low_k, vf, 1.0) if e == 0 else jnp.where(low_k, 1.0, vf)).astype(BF16)
                    pvs.append(jnp.dot(p.astype(BF16), ve, preferred_element_type=F32))
                    ms.append(m)
                results.append((qrows, pvs, ms))
            for qrows, pvs, ms in results:
                num = jnp.where(low_q, pvs[0], pvs[1])
                den = pltpu.roll(jnp.where(low_q, pvs[1], pvs[0]), HEAD_DIM, axis=1)
                mx = jnp.where(low_q, ms[0], ms[1])
                if pat > 0:
                    m_old = m_scr[qrows, :]
                    m_new = jnp.maximum(m_old, mx)
                    a = jnp.exp(m_old - m_new)
                    b = jnp.exp(mx - m_new)
                    num = a * n_scr[qrows, :] + b * num
                    den = a * s_scr[qrows, :] + b * den
                    mx = m_new
                m_scr[qrows, :] = mx
                n_scr[qrows, :] = num
                s_scr[qrows, :] = den
    o_ref[...] = (n_scr[...] / s_scr[...]).astype(o_ref.dtype)


def dilated_mixture(qkv3, bias):
    bsz, s, _ = qkv3.shape
    pairs = A_HEADS // 2
    blk = (None, DIL_TILE, V7X_LANES)
    ck, cv = COL_KA // V7X_LANES, COL_VA // V7X_LANES

    def cur(col):
        return pl.BlockSpec(blk, lambda b, t, g: (b, t, col + g))

    def prev(col):
        return pl.BlockSpec(blk, lambda b, t, g: (b, jnp.maximum(t - 1, 0), col + g))

    tile_f32 = pltpu.VMEM((DIL_TILE, V7X_LANES), F32)
    both_f32 = pltpu.VMEM((2 * DIL_TILE, V7X_LANES), F32)
    return pl.pallas_call(
        _dilated_kernel, grid=(bsz, s // DIL_TILE, pairs),
        in_specs=[cur(0), prev(ck), cur(ck), prev(cv), cur(cv),
                  pl.BlockSpec((len(A_PATTERNS), 2, 2, A_SPAN, 2 * A_SPAN),
                               lambda b, t, g: (0, 0, g, 0, 0))],
        out_specs=pl.BlockSpec(blk, lambda b, t, g: (b, t, g)),
        out_shape=jax.ShapeDtypeStruct((bsz, s, A_WIDTH), BF16),
        scratch_shapes=[both_f32, both_f32, tile_f32, tile_f32, tile_f32],
        compiler_params=_cparams(("parallel", "parallel", "arbitrary")), name="dilated_mixture",
    )(qkv3, qkv3, qkv3, qkv3, qkv3, bias)


FOX_PREP_ROWS = 512
FOX_TQ = 512
FOX_TK = 512
FOX_HEADS_PER_STEP = 4


def _fox_prep_kernel(sm_ref, fb_ref, q0_ref, q1_ref, k0_ref, k1_ref, v0_ref, v1_ref,
                     qa_ref, ka_ref, va_ref, carry_scr):
    t = sm_ref.shape[0]

    @pl.when(pl.program_id(1) == 0)
    def _():
        carry_scr[...] = jnp.zeros_like(carry_scr)

    x = _log_sigmoid(sm_ref[...] + fb_ref[...])
    tri = _tril_ones(t, BF16)
    hi, mid, lo = _split3(x)
    c = (jnp.dot(tri, hi, preferred_element_type=F32)
         + jnp.dot(tri, mid, preferred_element_type=F32)
         + jnp.dot(tri, lo, preferred_element_type=F32)) + carry_scr[0:1, :]
    carry_scr[...] = jnp.broadcast_to(c[t - 1:t, :], carry_scr.shape)
    chi, cmid, clo = [p.astype(F32) for p in _split3(c * LOG2E)]

    lane = lax.broadcasted_iota(jnp.int32, (t, V7X_LANES), 1)
    low = lane < HEAD_DIM
    scale = HEAD_DIM ** -0.5 * LOG2E
    q_refs, k_refs, v_refs = (q0_ref, q1_ref), (k0_ref, k1_ref), (v0_ref, v1_ref)
    for h in range(B_HEADS):
        src, pair = divmod(h // 2, 2)
        cs = slice(V7X_LANES * pair, V7X_LANES * (pair + 1))
        qp = q_refs[src][:, cs].astype(F32)
        kp = k_refs[src][:, cs].astype(F32)
        vp = v_refs[src][:, cs].astype(F32)
        if h % 2 == 1:
            qp = pltpu.roll(qp, HEAD_DIM, axis=1)
            kp = pltpu.roll(kp, HEAD_DIM, axis=1)
        c1, c2, c3 = chi[:, h:h + 1], cmid[:, h:h + 1], clo[:, h:h + 1]
        one = jnp.ones((t, 1), F32)
        zero = jnp.zeros((t, 1), F32)
        q_aug, k_aug = zero, zero
        for idx, (qv, kv) in enumerate(((c1, one), (c2, one), (c3, one),
                                        (one, -c1), (one, -c2), (one, -c3))):
            hit = lane == HEAD_DIM + idx
            q_aug = jnp.where(hit, qv, q_aug)
            k_aug = jnp.where(hit, kv, k_aug)
        qa_ref[h] = jnp.where(low, qp * scale, q_aug).T.astype(BF16)
        ka_ref[h] = jnp.where(low, kp, k_aug).astype(BF16)
        v_aug = jnp.where(low, vp, 1.0) if h % 2 == 0 else jnp.where(low, 1.0, vp)
        va_ref[h, 0] = v_aug.T.astype(BF16)


def fox_prep(proj3, small3, f_bias):
    bsz, s, _ = proj3.shape
    t = FOX_PREP_ROWS
    half = B_WIDTH // 2
    fb = jnp.zeros((1, V7X_LANES), F32).at[0, SMALL_FL:SMALL_FL + B_HEADS].set(f_bias)

    def colspec(col):
        return pl.BlockSpec((None, t, half), lambda b, i, c=col // half: (b, i, c))

    assert t == FOX_TK
    k_spec = pl.BlockSpec((None, B_HEADS, t, V7X_LANES), lambda b, i: (b, 0, i, 0))
    k_shape = jax.ShapeDtypeStruct((bsz, B_HEADS, s, V7X_LANES), BF16)
    q_spec = pl.BlockSpec((None, B_HEADS, V7X_LANES, t), lambda b, i: (b, 0, 0, i))
    q_shape = jax.ShapeDtypeStruct((bsz, B_HEADS, V7X_LANES, s), BF16)
    v_spec = pl.BlockSpec((None, B_HEADS, 1, V7X_LANES, t), lambda b, i: (b, 0, i, 0, 0))
    v_shape = jax.ShapeDtypeStruct((bsz, B_HEADS, s // t, V7X_LANES, t), BF16)
    return pl.pallas_call(
        _fox_prep_kernel, grid=(bsz, s // t),
        in_specs=[pl.BlockSpec((None, t, V7X_LANES), lambda b, i: (b, i, 0)),
                  pl.BlockSpec((1, V7X_LANES), lambda b, i: (0, 0)),
                  colspec(COL_QB), colspec(COL_QB + half),
                  colspec(COL_KB), colspec(COL_KB + half),
                  colspec(COL_VB), colspec(COL_VB + half)],
        out_specs=[q_spec, k_spec, v_spec],
        out_shape=[q_shape, k_shape, v_shape],
        scratch_shapes=[pltpu.VMEM((8, V7X_LANES), F32)],
        compiler_params=_cparams(("parallel", "arbitrary")), name="fox_prep",
    )(small3, fb, proj3, proj3, proj3, proj3, proj3, proj3)


def _fox_kernel(q_ref, k_ref, v_ref, o_ref):
    tq, tk = FOX_TQ, FOX_TK
    qi = pl.program_id(2)
    nfull = qi * (tq // tk)
    krow = lax.broadcasted_iota(jnp.int32, (tk, tq), 0)
    qcol = lax.broadcasted_iota(jnp.int32, (tk, tq), 1)
    frow = lax.broadcasted_iota(jnp.int32, (V7X_LANES, tq), 0)
    nh = q_ref.shape[0]
    qs = [q_ref[e] for e in range(nh)]

    def heads(kt, carry, mask):
        rows = pl.ds(pl.multiple_of(kt * tk, tk), tk)
        scores = [jnp.dot(k_ref[e, rows, :], qs[e], preferred_element_type=F32) for e in range(nh)]
        out = []
        for e in range(nh):
            m, acc = carry[2 * e], carry[2 * e + 1]
            s = scores[e] if mask is None else jnp.where(mask, scores[e], NEG)
            m_new = jnp.maximum(m, jnp.max(s, axis=0, keepdims=True))
            alpha = jnp.exp2(m - m_new)
            p = jnp.exp2(s - m_new)
            acc = alpha * acc + jnp.dot(v_ref[e, kt], p.astype(BF16), preferred_element_type=F32)
            out += [m_new, acc]
        return tuple(out)

    m0 = jnp.full((1, tq), NEG, F32)
    acc0 = jnp.zeros((V7X_LANES, tq), F32)
    carry = lax.fori_loop(0, nfull, lambda kt, c: heads(kt, c, None), (m0, acc0) * nh)
    for dd in range(tq // tk):
        carry = heads(nfull + dd, carry, krow + dd * tk <= qcol)
    for g in range(nh // 2):
        even, odd = carry[4 * g + 1], carry[4 * g + 3]
        res = [acc / pltpu.roll(acc, HEAD_DIM, axis=0) for acc in (even, odd)]
        pair = jnp.where(frow < HEAD_DIM, res[0], res[1])
        o_ref[:, V7X_LANES * g:V7X_LANES * (g + 1)] = pair.T.astype(o_ref.dtype)


def fox_attention(q_aug, k_aug, v_aug):
    bsz, nh, s, _ = k_aug.shape
    tq, tk, hs = FOX_TQ, FOX_TK, FOX_HEADS_PER_STEP
    return pl.pallas_call(
        _fox_kernel, grid=(bsz, nh // hs, s // tq),
        in_specs=[pl.BlockSpec((None, hs, V7X_LANES, tq), lambda b, g, i: (b, g, 0, i)),
                  pl.BlockSpec((None, hs, s, V7X_LANES), lambda b, g, i: (b, g, 0, 0)),
                  pl.BlockSpec((None, hs, s // tk, V7X_LANES, tk), lambda b, g, i: (b, g, 0, 0, 0))],
        out_specs=pl.BlockSpec((None, tq, hs * HEAD_DIM), lambda b, g, i: (b, i, g)),
        out_shape=jax.ShapeDtypeStruct((bsz, s, B_WIDTH), BF16),
        compiler_params=_cparams(("parallel", "parallel", "arbitrary")), name="fox_attention",
    )(q_aug, k_aug, v_aug)


GLA_ROWS = 256


def _gla_chunk(rows, q_ref, k_ref, v_ref, r_ref, sm_bf, w2_ref, cb_ref, gn_ref, o_ref,
               st_scr, b_scr, k_scr, v_scr, tri, row_sub, row_ck):
    ck, sub = C_CHUNK, C_SUB
    nblk = ck // sub
    heads = range(C_HEADS)
    nt = (((1,), (1,)), ((), ()))
    kcols = [slice(C_DK_PAD * h, C_DK_PAD * (h + 1)) for h in heads]
    vcols = [slice(C_DV_PAD * h, C_DV_PAD * (h + 1)) for h in heads]
    g = [jnp.dot(sm_bf, w2_ref[h], preferred_element_type=F32) + cb_ref[h] for h in heads]
    parts = [_split3(_log_sigmoid(g[h]) / C_GATE_TAU) for h in heads]
    b = [(jnp.dot(tri, parts[h][0], preferred_element_type=F32)
          + jnp.dot(tri, parts[h][1], preferred_element_type=F32)
          + jnp.dot(tri, parts[h][2], preferred_element_type=F32)) for h in heads]
    qf = [q_ref[rows, kcols[h]].astype(F32) * (C_DK ** -0.5) for h in heads]
    kf = [k_ref[rows, kcols[h]].astype(F32) for h in heads]
    vb = [v_ref[rows, vcols[h]] for h in heads]
    st = [st_scr[h] for h in heads]
    for h in heads:
        b_scr[h] = b[h]
        k_scr[h] = kf[h]
        v_scr[h] = vb[h].astype(F32)
    o_inter = [lax.dot_general((qf[h] * jnp.exp(b[h])).astype(BF16), st[h].astype(BF16), nt,
                               preferred_element_type=F32) for h in heads]
    a = {}
    for h in heads:
        for blk in range(1, nblk):
            r0 = blk * sub
            ref_row = b_scr[h, r0 - 1:r0, :]
            kd = jnp.where(row_ck < r0, kf[h] * jnp.exp(jnp.minimum(ref_row - b[h], 0.0)), 0.0)
            qd = qf[h][r0:r0 + sub, :] * jnp.exp(b[h][r0:r0 + sub, :] - ref_row)
            a[h, blk] = lax.dot_general(qd.astype(BF16), kd.astype(BF16), nt, preferred_element_type=F32)
    strips = {(h, blk): jnp.dot(a[h, blk].astype(BF16), vb[h], preferred_element_type=F32)
              for h in heads for blk in range(1, nblk)}
    kdec = [(kf[h] * jnp.exp(b_scr[h, ck - 1:ck, :] - b[h])).astype(BF16) for h in heads]
    upd = [lax.dot_general(vb[h], kdec[h], (((0,), (0,)), ((), ())), preferred_element_type=F32)
           for h in heads]
    for h in heads:
        st_scr[h] = st[h] * jnp.exp(b_scr[h, ck - 1:ck, :]) + upd[h]
    for h in heads:
        outs = []
        for blk in range(nblk):
            r0 = blk * sub
            qb = qf[h][r0:r0 + sub, :]
            bb = b[h][r0:r0 + sub, :]
            ob = o_inter[h][r0:r0 + sub, :]
            if blk > 0:
                ob = ob + strips[h, blk]
            for j in range(sub):
                bj = b_scr[h, r0 + j:r0 + j + 1, :]
                kj = k_scr[h, r0 + j:r0 + j + 1, :]
                vj = v_scr[h, r0 + j:r0 + j + 1, :]
                x = qb * kj * jnp.exp(jnp.minimum(bb - bj, 0.0))
                colv = jnp.sum(x, axis=-1, keepdims=True)
                colv = jnp.where(row_sub >= j, colv, 0.0)
                ob = ob + colv * vj
            outs.append(ob)
        o = jnp.concatenate(outs, axis=0)
        ms = jnp.sum(o * o, axis=-1, keepdims=True) * (1.0 / C_DV)
        on = o * lax.rsqrt(ms + EPS) * gn_ref[...]
        rr = r_ref[rows, vcols[h]].astype(F32)
        o_ref[rows, vcols[h]] = (on * (rr / (1.0 + jnp.exp(-rr)))).astype(o_ref.dtype)


def _gla_kernel(q_ref, k_ref, v_ref, r_ref, sm_ref, w2_ref, cb_ref, gn_ref, o_ref,
                st_scr, b_scr, k_scr, v_scr):
    ck, sub = C_CHUNK, C_SUB

    @pl.when(pl.program_id(1) == 0)
    def _():
        st_scr[...] = jnp.zeros_like(st_scr)

    tri = _tril_ones(ck, BF16)
    row_sub = lax.broadcasted_iota(jnp.int32, (sub, 1), 0)
    row_ck = lax.broadcasted_iota(jnp.int32, (ck, 1), 0)

    def chunk(c, carry):
        rows = pl.ds(pl.multiple_of(c * ck, ck), ck)
        sm_bf = sm_ref[rows, :].astype(BF16)
        _gla_chunk(rows, q_ref, k_ref, v_ref, r_ref, sm_bf, w2_ref, cb_ref, gn_ref, o_ref,
                   st_scr, b_scr, k_scr, v_scr, tri, row_sub, row_ck)
        return carry

    lax.fori_loop(0, q_ref.shape[0] // ck, chunk, 0)


def gla_mixer(proj3, small3, w2p, cbp, gnp):
    bsz, s, _ = proj3.shape
    t = GLA_ROWS
    kw, vw = C_HEADS * C_DK_PAD, C_HEADS * C_DV_PAD

    def col(base, width):
        return pl.BlockSpec((None, t, width), lambda b, i, c=base // width: (b, i, c))

    return pl.pallas_call(
        _gla_kernel, grid=(bsz, s // t),
        in_specs=[col(COL_QC, kw), col(COL_KC, kw), col(COL_VC, vw), col(COL_RC, vw),
                  pl.BlockSpec((None, t, V7X_LANES), lambda b, i: (b, i, 0)),
                  pl.BlockSpec((C_HEADS, V7X_LANES, C_DK_PAD), lambda b, i: (0, 0, 0)),
                  pl.BlockSpec((C_HEADS, 1, C_DK_PAD), lambda b, i: (0, 0, 0)),
                  pl.BlockSpec((1, C_DV_PAD), lambda b, i: (0, 0))],
        out_specs=pl.BlockSpec((None, t, vw), lambda b, i: (b, i, 0)),
        out_shape=jax.ShapeDtypeStruct((bsz, s, vw), BF16),
        scratch_shapes=[pltpu.VMEM((C_HEADS, C_DV_PAD, C_DK_PAD), F32),
                        pltpu.VMEM((C_HEADS, C_CHUNK, C_DK_PAD), F32),
                        pltpu.VMEM((C_HEADS, C_CHUNK, C_DK_PAD), F32),
                        pltpu.VMEM((C_HEADS, C_CHUNK, C_DV_PAD), F32)],
        compiler_params=_cparams(("parallel", "arbitrary")), name="gla_mixer",
    )(proj3, proj3, proj3, proj3, small3, w2p, cbp, gnp)


def _out_proj_kernel(x_ref, a_ref, b_ref, c_ref, wa_ref, wb_ref, wc_ref, o_ref):
    acc = jnp.dot(a_ref[...], wa_ref[...], preferred_element_type=F32)
    acc += jnp.dot(b_ref[...], wb_ref[...], preferred_element_type=F32)
    acc += jnp.dot(c_ref[...], wc_ref[...], preferred_element_type=F32)
    o_ref[...] = x_ref[...] + acc


def out_proj(x2d, oa, ob, oc, wa, wb, wc, *, tm=1024, tn=512):
    m, n = x2d.shape

    def rows(a):
        return pl.BlockSpec((tm, a.shape[1]), lambda i, j: (i, 0))

    def cols(w):
        return pl.BlockSpec((w.shape[0], tn), lambda i, j: (0, j))

    xs = pl.BlockSpec((tm, tn), lambda i, j: (i, j))
    return pl.pallas_call(
        _out_proj_kernel, grid=(m // tm, n // tn),
        in_specs=[xs, rows(oa), rows(ob), rows(oc), cols(wa), cols(wb), cols(wc)],
        out_specs=xs, out_shape=jax.ShapeDtypeStruct((m, n), F32),
        compiler_params=_cparams(("parallel", "arbitrary")), name="out_proj",
    )(x2d, oa, ob, oc, wa, wb, wc)


CROSS_ROWS = 512


def _cross_kernel(x_ref, g_ref, wq_ref, k_ref, v_ref, wo_ref, o_ref):
    x = x_ref[...]
    h = _rms_rows(x, g_ref[...]).astype(BF16)
    q = jnp.dot(h, wq_ref[...], preferred_element_type=F32).astype(BF16)
    heads = []
    for hd in range(CROSS_HEADS):
        cs = slice(CROSS_DH * hd, CROSS_DH * (hd + 1))
        logits = lax.dot_general(q[:, cs], k_ref[:, cs], (((1,), (1,)), ((), ())),
                                 preferred_element_type=F32) * (CROSS_DH ** -0.5)
        m = jnp.max(logits, axis=-1, keepdims=True)
        p = jnp.exp(logits - m)
        ssum = jnp.sum(p, axis=-1, keepdims=True)
        pv = jnp.dot(p.astype(BF16), v_ref[:, cs], preferred_element_type=F32)
        heads.append((pv / ssum).astype(BF16))
    o = jnp.concatenate(heads, axis=1)
    o_ref[...] = x + jnp.dot(o, wo_ref[...], preferred_element_type=F32)


def cross_attention(x2d, gain, wq, kv, wo, *, seq):
    m, d = x2d.shape
    tm = CROSS_ROWS
    mem_len = kv.shape[1]
    per_batch = seq // tm
    return pl.pallas_call(
        _cross_kernel, grid=(m // tm,),
        in_specs=[pl.BlockSpec((tm, d), lambda i: (i, 0)),
                  pl.BlockSpec((1, d), lambda i: (0, 0)),
                  pl.BlockSpec((d, CROSS_WIDTH), lambda i: (0, 0)),
                  pl.BlockSpec((None, mem_len, CROSS_WIDTH), lambda i: (i // per_batch, 0, 0)),
                  pl.BlockSpec((None, mem_len, CROSS_WIDTH), lambda i: (i // per_batch, 0, 1)),
                  pl.BlockSpec((CROSS_WIDTH, d), lambda i: (0, 0))],
        out_specs=pl.BlockSpec((tm, d), lambda i: (i, 0)),
        out_shape=jax.ShapeDtypeStruct((m, d), F32),
        compiler_params=_cparams(("parallel",)), name="cross_attention",
    )(x2d, gain.reshape(1, d), wq, kv, kv, wo)


FFN_ROWS = 512
FFN_CHUNK = FFN_ROWS
FFN_HALO = 16


def _ffn_kernel(x_ref, xh_ref, g_ref, wv_ref, wg_ref, cw_ref, cb_ref, wd_ref, gf_ref, o_ref,
                h_scr, hh_scr, *, final):
    j = pl.program_id(1)

    @pl.when(j == 0)
    def _():
        _norm_to_scratch(x_ref, g_ref, h_scr)
        hh_scr[...] = _rms_rows(xh_ref[...], g_ref[...]).astype(BF16)
        o_ref[...] = x_ref[...]

    rc = FFN_CHUNK
    nchunk = h_scr.shape[0] // rc
    prev = jnp.dot(hh_scr[...], wg_ref[...], preferred_element_type=F32)
    row = lax.broadcasted_iota(jnp.int32, (rc, wg_ref.shape[1]), 0)
    vals, gates = [], []
    for c in range(nchunk):
        h = h_scr[c * rc:(c + 1) * rc, :]
        vals.append(jnp.dot(h, wv_ref[...], preferred_element_type=F32))
        gates.append(jnp.dot(h, wg_ref[...], preferred_element_type=F32))
    for c in range(nchunk):
        rows = slice(c * rc, (c + 1) * rc)
        val, gate = vals[c], gates[c]
        prev1 = prev[prev.shape[0] - 1:, :]
        prev2 = prev[prev.shape[0] - 2:prev.shape[0] - 1, :]
        g1 = jnp.where(row == 0, prev1, pltpu.roll(gate, 1, axis=0))
        g2 = jnp.where(row == 0, prev2, jnp.where(row == 1, prev1, pltpu.roll(gate, 2, axis=0)))
        gc = cb_ref[...] + cw_ref[0:1, :] * g2
        gc = gc + cw_ref[1:2, :] * g1
        gc = gc + cw_ref[2:3, :] * gate
        act = (gc / (1.0 + jnp.exp(-gc)) * val).astype(BF16)
        o_ref[rows, :] += jnp.dot(act, wd_ref[...], preferred_element_type=F32)
        prev = gate

    if final:
        @pl.when(j == pl.num_programs(1) - 1)
        def _():
            def body(r, c):
                sl = pl.ds(pl.multiple_of(r * NORM_ROWS, NORM_ROWS), NORM_ROWS)
                o_ref[sl, :] = _rms_rows(o_ref[sl, :], gf_ref[...])
                return c
            lax.fori_loop(0, o_ref.shape[0] // NORM_ROWS, body, 0)


def conv_ffn(x2d, gain, wv, wg, cw, cb, wd, gain_final, *, seq, final):
    m, d = x2d.shape
    tm, tf = FFN_ROWS, FF_TILE
    nt = m // tm
    tail = x2d.reshape(nt, tm, d)[:, tm - FFN_HALO:, :]
    halo = jnp.concatenate([jnp.zeros_like(tail[:1]), tail[:-1]], axis=0)
    starts = (jnp.arange(nt) % (seq // tm) == 0)[:, None, None]
    halo = jnp.where(starts, 0.0, halo)
    ff = wv.shape[1]
    return pl.pallas_call(
        functools.partial(_ffn_kernel, final=final), grid=(nt, ff // tf),
        in_specs=[pl.BlockSpec((tm, d), lambda i, j: (i, 0)),
                  pl.BlockSpec((None, FFN_HALO, d), lambda i, j: (i, 0, 0)),
                  pl.BlockSpec((1, d), lambda i, j: (0, 0)),
                  pl.BlockSpec((d, tf), lambda i, j: (0, j)),
                  pl.BlockSpec((d, tf), lambda i, j: (0, j)),
                  pl.BlockSpec((CONV_W, tf), lambda i, j: (0, j)),
                  pl.BlockSpec((1, tf), lambda i, j: (0, j)),
                  pl.BlockSpec((tf, d), lambda i, j: (j, 0)),
                  pl.BlockSpec((1, d), lambda i, j: (0, 0))],
        out_specs=pl.BlockSpec((tm, d), lambda i, j: (i, 0)),
        out_shape=jax.ShapeDtypeStruct((m, d), F32),
        scratch_shapes=[pltpu.VMEM((tm, d), BF16), pltpu.VMEM((FFN_HALO, d), BF16)],
        compiler_params=_cparams(("parallel", "arbitrary")), name="conv_ffn",
    )(x2d, halo, gain.reshape(1, d), wv, wg, cw, cb.reshape(1, ff), wd, gain_final.reshape(1, d))


def _pad_heads(w, heads, width, padded):
    lead = w.shape[:-1]
    w = w.reshape(lead + (heads, width))
    w = jnp.pad(w, [(0, 0)] * len(lead) + [(0, 0), (0, padded - width)])
    return w.reshape(lead + (heads * padded,))


def _layer_params(l, w_in, c_gate_w2, c_gate_b, c_norm, w_out, w_up, conv_w, conv_b, w_down):
    splits = np.cumsum([A_WIDTH] * 3 + [B_WIDTH] * 3 + [B_HEADS, C_KW, C_KW, C_VW, C_VW, C_GATE_RANK])
    w_a = w_in[l][:, :splits[2]].astype(BF16)
    fl, qc, kc, vc, rc, gl = jnp.split(w_in[l][:, splits[5]:], (splits[6:11] - splits[5]).tolist(), axis=1)
    w_main = jnp.concatenate([
        _pad_heads(vc, C_HEADS, C_DV, C_DV_PAD), _pad_heads(rc, C_HEADS, C_DV, C_DV_PAD),
        w_in[l][:, splits[2]:splits[5]],
        _pad_heads(qc, C_HEADS, C_DK, C_DK_PAD), _pad_heads(kc, C_HEADS, C_DK, C_DK_PAD)],
        axis=1).astype(BF16)
    w_small = jnp.zeros((D_MODEL, V7X_LANES), F32)
    w_small = w_small.at[:, SMALL_FL:SMALL_FL + B_HEADS].set(fl)
    w_small = w_small.at[:, SMALL_GL:SMALL_GL + C_GATE_RANK].set(gl).astype(BF16)
    w2 = _pad_heads(c_gate_w2[l], C_HEADS, C_DK, C_DK_PAD).reshape(C_GATE_RANK, C_HEADS, C_DK_PAD)
    w2p = jnp.zeros((C_HEADS, V7X_LANES, C_DK_PAD), F32)
    w2p = w2p.at[:, SMALL_GL:SMALL_GL + C_GATE_RANK, :].set(jnp.transpose(w2, (1, 0, 2))).astype(BF16)
    cbp = _pad_heads(c_gate_b[l], C_HEADS, C_DK, C_DK_PAD).reshape(C_HEADS, 1, C_DK_PAD)
    gnp = jnp.pad(c_norm[l], (0, C_DV_PAD - C_DV)).reshape(1, C_DV_PAD)
    wo = w_out[l]
    wa = wo[:A_WIDTH].astype(BF16)
    wb = wo[A_WIDTH:A_WIDTH + B_WIDTH].astype(BF16)
    wc = wo[A_WIDTH + B_WIDTH:].reshape(C_HEADS, C_DV, D_MODEL)
    wc = jnp.pad(wc, ((0, 0), (0, C_DV_PAD - C_DV), (0, 0))).reshape(C_HEADS * C_DV_PAD, D_MODEL).astype(BF16)
    fpad = D_FF_PAD - D_FF
    wv = jnp.pad(w_up[l][:, :D_FF], ((0, 0), (0, fpad))).astype(BF16)
    wg = jnp.pad(w_up[l][:, D_FF:], ((0, 0), (0, fpad))).astype(BF16)
    cw = jnp.pad(conv_w[l], ((0, 0), (0, fpad)))
    cb = jnp.pad(conv_b[l], (0, fpad))
    wd = jnp.pad(w_down[l], ((0, fpad), (0, 0))).astype(BF16)
    return dict(w_a=w_a, w_main=w_main, w_small=w_small, w2p=w2p, cbp=cbp, gnp=gnp,
                wa=wa, wb=wb, wc=wc, wv=wv, wg=wg, cw=cw, cb=cb, wd=wd)


def hybrid_mixer(x2d, gain, p, f_bias, bias_a, *, bsz, seq):
    qkv_a = norm_matmul(x2d, gain, p["w_a"], tm=1024, tn=A_WIDTH, name="in_proj_a", out_dtype=F32)
    proj, small = norm_matmul(x2d, gain, p["w_main"], p["w_small"], tm=1024, tn=A_WIDTH, name="in_proj")
    proj3 = proj.reshape(bsz, seq, PROJ_PAD)
    small3 = small.reshape(bsz, seq, V7X_LANES)
    o_a = dilated_mixture(qkv_a.reshape(bsz, seq, PROJ_A), bias_a).reshape(bsz * seq, A_WIDTH)
    o_b = fox_attention(*fox_prep(proj3, small3, f_bias)).reshape(bsz * seq, B_WIDTH)
    o_c = gla_mixer(proj3, small3, p["w2p"], p["cbp"], p["gnp"]).reshape(bsz * seq, C_HEADS * C_DV_PAD)
    return out_proj(x2d, o_a, o_b, o_c, p["wa"], p["wb"], p["wc"])


def kernel(x, mem, rel_table, mem_norm, norm_final, norm_mix, w_in, f_bias, c_gate_w2, c_gate_b,
           c_norm, w_out, norm_cross, w_cq, w_ckv, w_co, norm_ffn, w_up, conv_w, conv_b, w_down):
    bsz, seq, d = x.shape
    depth = w_in.shape[0]
    mem_len = mem.shape[1]
    assert d == D_MODEL and seq % (A_SPAN * A_PATTERNS[-1][1]) == 0 and seq % 1024 == 0
    bias_a = jnp.stack([_dilated_bias(rel_table, dil) for _, dil in A_PATTERNS], axis=0)
    x2d = x.reshape(bsz * seq, d)
    mem2d = mem.reshape(bsz * mem_len, d)
    for l in range(depth):
        p = _layer_params(l, w_in, c_gate_w2, c_gate_b, c_norm, w_out, w_up, conv_w, conv_b, w_down)
        x2d = hybrid_mixer(x2d, norm_mix[l], p, f_bias[l], bias_a, bsz=bsz, seq=seq)
        kv = norm_matmul(mem2d, mem_norm, w_ckv[l].astype(BF16), tm=min(512, bsz * mem_len), tn=512,
                         name="mem_kv").reshape(bsz, mem_len, 2 * CROSS_WIDTH)
        x2d = cross_attention(x2d, norm_cross[l], w_cq[l].astype(BF16), kv, w_co[l].astype(BF16), seq=seq)
        x2d = conv_ffn(x2d, norm_ffn[l], p["wv"], p["wg"], p["cw"], p["cb"], p["wd"], norm_final,
                       seq=seq, final=l == depth - 1)
    return x2d.reshape(bsz, seq, d)
```

```python
import functools
import math

import numpy as np
import jax
import jax.numpy as jnp
from jax import lax
from jax.experimental import pallas as pl
from jax.experimental.pallas import tpu as pltpu

F32 = jnp.float32
BF16 = jnp.bfloat16

V7X_LANES = 128
V7X_VMEM_LIMIT_BYTES = 56 * 1024 * 1024

D_MODEL = 2048
HEAD_DIM = 64
A_WIDTH = 3 * D_MODEL // 8
A_HEADS = A_WIDTH // HEAD_DIM
A_PATTERNS = ((128, 1), (512, 4), (2048, 16))
A_SPAN = 128
B_WIDTH = D_MODEL // 4
B_HEADS = B_WIDTH // HEAD_DIM
C_HEADS = 4
C_VW = 3 * D_MODEL // 8
C_DV = C_VW // C_HEADS
C_DK = C_DV // 2
C_KW = C_HEADS * C_DK
C_GATE_RANK = 16
C_GATE_TAU = 16.0
C_CHUNK = 64
C_SUB = 16
C_DK_PAD = 128
C_DV_PAD = 256
CROSS_HEADS = 4
CROSS_DH = 128
CROSS_WIDTH = CROSS_HEADS * CROSS_DH
D_FF = ((8 * D_MODEL // 3 + 127) // 128) * 128
FF_TILE = 512
D_FF_PAD = ((D_FF + FF_TILE - 1) // FF_TILE) * FF_TILE
CONV_W = 3
REL_BUCKETS = 32
REL_MAX_DIST = 2048
EPS = 1e-6
NEG = -1e30
LOG2E = 1.4426950408889634

COL_QA = 0
COL_KA = COL_QA + A_WIDTH
COL_VA = COL_KA + A_WIDTH
PROJ_A = COL_VA + A_WIDTH
COL_VC = 0
COL_RC = COL_VC + C_HEADS * C_DV_PAD
COL_QB = COL_RC + C_HEADS * C_DV_PAD
COL_KB = COL_QB + B_WIDTH
COL_VB = COL_KB + B_WIDTH
COL_QC = COL_VB + B_WIDTH
COL_KC = COL_QC + C_HEADS * C_DK_PAD
PROJ_PAD = COL_KC + C_HEADS * C_DK_PAD
SMALL_FL = 0
SMALL_GL = B_HEADS


def _cparams(sem):
    return pltpu.CompilerParams(dimension_semantics=sem,
                                vmem_limit_bytes=V7X_VMEM_LIMIT_BYTES)


def _rms_rows(x, g):
    ms = jnp.mean(x * x, axis=-1, keepdims=True)
    return x * lax.rsqrt(ms + EPS) * g


def _split3(x):
    hi = x.astype(BF16)
    r1 = x - hi.astype(F32)
    mid = r1.astype(BF16)
    lo = (r1 - mid.astype(F32)).astype(BF16)
    return hi, mid, lo


def _log_sigmoid(x):
    return jnp.minimum(x, 0.0) - jnp.log(1.0 + jnp.exp(-jnp.abs(x)))


def _tril_ones(n, dtype):
    r = lax.broadcasted_iota(jnp.int32, (n, n), 0)
    c = lax.broadcasted_iota(jnp.int32, (n, n), 1)
    return jnp.where(r >= c, 1.0, 0.0).astype(dtype)


NORM_ROWS = 256


def _norm_to_scratch(x_ref, g_ref, h_scr):
    def body(r, c):
        sl = pl.ds(pl.multiple_of(r * NORM_ROWS, NORM_ROWS), NORM_ROWS)
        h_scr[sl, :] = _rms_rows(x_ref[sl, :], g_ref[...]).astype(BF16)
        return c
    lax.fori_loop(0, x_ref.shape[0] // NORM_ROWS, body, 0)


def _norm_matmul_kernel(x_ref, g_ref, w_ref, o_ref, h_scr):
    @pl.when(pl.program_id(1) == 0)
    def _():
        _norm_to_scratch(x_ref, g_ref, h_scr)
    o_ref[...] = jnp.dot(h_scr[...], w_ref[...],
                         preferred_element_type=F32).astype(o_ref.dtype)


def norm_matmul(x2d, gain, w, *, tm, tn, name):
    m, k = x2d.shape
    n = w.shape[1]
    return pl.pallas_call(
        _norm_matmul_kernel, grid=(m // tm, n // tn),
        in_specs=[pl.BlockSpec((tm, k), lambda i, j: (i, 0)),
                  pl.BlockSpec((1, k), lambda i, j: (0, 0)),
                  pl.BlockSpec((k, tn), lambda i, j: (0, j))],
        out_specs=pl.BlockSpec((tm, tn), lambda i, j: (i, j)),
        out_shape=jax.ShapeDtypeStruct((m, n), BF16),
        scratch_shapes=[pltpu.VMEM((tm, k), BF16)],
        compiler_params=_cparams(("parallel", "arbitrary")), name=name)(x2d, gain.reshape(1, k), w)


IN_PROJ_ROWS = 1024
IN_PROJ_TILE = A_WIDTH
IN_PROJ_A_TILES = PROJ_A // IN_PROJ_TILE


def _in_proj_kernel(x_ref, g_ref, w_ref, ws_ref, oa_ref, om_ref, os_ref, h_scr):
    j = pl.program_id(1)

    @pl.when(j == 0)
    def _():
        _norm_to_scratch(x_ref, g_ref, h_scr)
        os_ref[...] = jnp.dot(h_scr[...], ws_ref[...], preferred_element_type=F32)

    acc = jnp.dot(h_scr[...], w_ref[...], preferred_element_type=F32)

    @pl.when(j < IN_PROJ_A_TILES)
    def _():
        oa_ref[...] = acc

    @pl.when(j >= IN_PROJ_A_TILES)
    def _():
        om_ref[...] = acc.astype(om_ref.dtype)


def in_proj(x2d, gain, w_all, w_small):
    m, k = x2d.shape
    tm, tn, na = IN_PROJ_ROWS, IN_PROJ_TILE, IN_PROJ_A_TILES
    ns = w_small.shape[1]
    return pl.pallas_call(
        _in_proj_kernel, grid=(m // tm, (PROJ_A + PROJ_PAD) // tn),
        in_specs=[pl.BlockSpec((tm, k), lambda i, j: (i, 0)),
                  pl.BlockSpec((1, k), lambda i, j: (0, 0)),
                  pl.BlockSpec((k, tn), lambda i, j: (0, j)),
                  pl.BlockSpec((k, ns), lambda i, j: (0, 0))],
        out_specs=[pl.BlockSpec((tm, tn), lambda i, j: (i, jnp.minimum(j, na - 1))),
                   pl.BlockSpec((tm, tn), lambda i, j: (i, jnp.maximum(j - na, 0))),
                   pl.BlockSpec((tm, ns), lambda i, j: (i, 0))],
        out_shape=[jax.ShapeDtypeStruct((m, PROJ_A), F32),
                   jax.ShapeDtypeStruct((m, PROJ_PAD), BF16),
                   jax.ShapeDtypeStruct((m, ns), F32)],
        scratch_shapes=[pltpu.VMEM((tm, k), BF16)],
        compiler_params=_cparams(("parallel", "arbitrary")), name="in_proj",
    )(x2d, gain.reshape(1, k), w_all, w_small)


def _t5_bucket(dist):
    max_exact = REL_BUCKETS // 2
    d = np.maximum(dist, 1).astype(np.float32)
    large = max_exact + (np.log(d / max_exact) / math.log(REL_MAX_DIST / max_exact)
                         * (REL_BUCKETS - max_exact)).astype(np.int32)
    return np.where(dist < max_exact, dist, np.minimum(large, REL_BUCKETS - 1)).astype(np.int32)


def _dilated_bias(rel_table, dilation):
    span = A_SPAN
    steps = np.arange(2 * span, -span, -1)
    valid = (steps >= 0) & (steps <= span)
    per_step = rel_table[_t5_bucket(np.clip(steps, 0, span) * dilation)].astype(F32)
    per_step = jnp.where(valid[:, None], per_step, NEG).T
    later = jnp.stack([per_step[:, 2 * span - i - span: 4 * span - i - span] for i in range(span)],
                      axis=1)
    first = jnp.where((np.arange(2 * span) >= span)[None, None, :], later, NEG)
    return jnp.stack([first, later], axis=0)


DIL_TILE = A_SPAN * max(d for _, d in A_PATTERNS)
DIL_UNITS = DIL_TILE // A_SPAN
DIL_GROUP = 4


def _dilated_kernel(q_ref, kp_ref, kc_ref, vp_ref, vc_ref, bias_ref, o_ref,
                    kk_scr, vv_scr, m_scr, n_scr, s_scr):
    tile, span = DIL_TILE, A_SPAN
    first_tile = pl.program_id(1) == 0
    kk_scr[0:tile, :] = kp_ref[...]
    kk_scr[tile:2 * tile, :] = kc_ref[...]
    vv_scr[0:tile, :] = vp_ref[...]
    vv_scr[tile:2 * tile, :] = vc_ref[...]
    lane_q = lax.broadcasted_iota(jnp.int32, (span, V7X_LANES), 1)
    lane_k = lax.broadcasted_iota(jnp.int32, (2 * span, V7X_LANES), 1)
    low_q = lane_q < HEAD_DIM
    low_k = lane_k < HEAD_DIM
    scale = HEAD_DIM ** -0.5
    qmask = (jnp.where(low_q, scale, 0.0), jnp.where(low_q, 0.0, scale))
    nt = (((1,), (1,)), ((), ()))

    for pat, (_, d) in enumerate(A_PATTERNS):
        for u0 in range(0, DIL_UNITS, DIL_GROUP):
            units = []
            for u in range(u0, u0 + DIL_GROUP):
                n, r = divmod(u, d)
                q0 = r + span * d * n
                k0 = tile + q0 - span * d
                if d == 1:
                    qrows, krows = pl.ds(q0, span), pl.ds(k0, 2 * span)
                else:
                    qrows, krows = pl.ds(q0, span, stride=d), pl.ds(k0, 2 * span, stride=d)
                qf = q_ref[qrows, :]
                kb = kk_scr[krows, :].astype(BF16)
                sel = jnp.where(first_tile, 0, 1) if n == 0 else 1
                logits = [lax.dot_general((qf * qmask[e]).astype(BF16), kb, nt,
                                          preferred_element_type=F32) + bias_ref[pat, sel, e]
                          for e in range(2)]
                units.append((qrows, krows, logits))
            results = []
            for qrows, krows, logits in units:
                vf = vv_scr[krows, :]
                pvs, ms = [], []
                for e in range(2):
                    m = jnp.max(logits[e], axis=-1, keepdims=True)
                    p = jnp.exp(logits[e] - m)
                    ve = (jnp.where(low_k, vf, 1.0) if e == 0 else jnp.where(low_k, 1.0, vf)).astype(BF16)
                    pvs.append(jnp.dot(p.astype(BF16), ve, preferred_element_type=F32))
                    ms.append(m)
                results.append((qrows, pvs, ms))
            for qrows, pvs, ms in results:
                num = jnp.where(low_q, pvs[0], pvs[1])
                den = pltpu.roll(jnp.where(low_q, pvs[1], pvs[0]), HEAD_DIM, axis=1)
                mx = jnp.where(low_q, ms[0], ms[1])
                if pat > 0:
                    m_old = m_scr[qrows, :]
                    m_new = jnp.maximum(m_old, mx)
                    a = jnp.exp(m_old - m_new)
                    b = jnp.exp(mx - m_new)
                    num = a * n_scr[qrows, :] + b * num
                    den = a * s_scr[qrows, :] + b * den
                    mx = m_new
                m_scr[qrows, :] = mx
                n_scr[qrows, :] = num
                s_scr[qrows, :] = den
    o_ref[...] = (n_scr[...] / s_scr[...]).astype(o_ref.dtype)


def dilated_mixture(qkv3, bias):
    bsz, s, _ = qkv3.shape
    pairs = A_HEADS // 2
    blk = (None, DIL_TILE, V7X_LANES)
    ck, cv = COL_KA // V7X_LANES, COL_VA // V7X_LANES

    def cur(col):
        return pl.BlockSpec(blk, lambda b, t, g: (b, t, col + g))

    def prev(col):
        return pl.BlockSpec(blk, lambda b, t, g: (b, jnp.maximum(t - 1, 0), col + g))

    tile_f32 = pltpu.VMEM((DIL_TILE, V7X_LANES), F32)
    both_f32 = pltpu.VMEM((2 * DIL_TILE, V7X_LANES), F32)
    return pl.pallas_call(
        _dilated_kernel, grid=(bsz, s // DIL_TILE, pairs),
        in_specs=[cur(0), prev(ck), cur(ck), prev(cv), cur(cv),
                  pl.BlockSpec((len(A_PATTERNS), 2, 2, A_SPAN, 2 * A_SPAN),
                               lambda b, t, g: (0, 0, g, 0, 0))],
        out_specs=pl.BlockSpec(blk, lambda b, t, g: (b, t, g)),
        out_shape=jax.ShapeDtypeStruct((bsz, s, A_WIDTH), BF16),
        scratch_shapes=[both_f32, both_f32, tile_f32, tile_f32, tile_f32],
        compiler_params=_cparams(("parallel", "parallel", "arbitrary")), name="dilated_mixture",
    )(qkv3, qkv3, qkv3, qkv3, qkv3, bias)


FOX_PREP_ROWS = 512
FOX_TQ = 512
FOX_TK = 512
FOX_HEADS_PER_STEP = 4


def _fox_prep_kernel(sm_ref, fb_ref, q0_ref, q1_ref, k0_ref, k1_ref, v0_ref, v1_ref,
                     qa_ref, ka_ref, va_ref, carry_scr):
    t = sm_ref.shape[0]

    @pl.when(pl.program_id(1) == 0)
    def _():
        carry_scr[...] = jnp.zeros_like(carry_scr)

    x = _log_sigmoid(sm_ref[...] + fb_ref[...])
    tri = _tril_ones(t, BF16)
    hi, mid, lo = _split3(x)
    c = (jnp.dot(tri, hi, preferred_element_type=F32)
         + jnp.dot(tri, mid, preferred_element_type=F32)
         + jnp.dot(tri, lo, preferred_element_type=F32)) + carry_scr[0:1, :]
    carry_scr[...] = jnp.broadcast_to(c[t - 1:t, :], carry_scr.shape)
    chi, cmid, clo = [p.astype(F32) for p in _split3(c * LOG2E)]

    lane = lax.broadcasted_iota(jnp.int32, (t, V7X_LANES), 1)
    low = lane < HEAD_DIM
    scale = HEAD_DIM ** -0.5 * LOG2E
    q_refs, k_refs, v_refs = (q0_ref, q1_ref), (k0_ref, k1_ref), (v0_ref, v1_ref)
    for h in range(B_HEADS):
        src, pair = divmod(h // 2, 2)
        cs = slice(V7X_LANES * pair, V7X_LANES * (pair + 1))
        qp = q_refs[src][:, cs].astype(F32)
        kp = k_refs[src][:, cs].astype(F32)
        vp = v_refs[src][:, cs].astype(F32)
        if h % 2 == 1:
            qp = pltpu.roll(qp, HEAD_DIM, axis=1)
            kp = pltpu.roll(kp, HEAD_DIM, axis=1)
        c1, c2, c3 = chi[:, h:h + 1], cmid[:, h:h + 1], clo[:, h:h + 1]
        one = jnp.ones((t, 1), F32)
        zero = jnp.zeros((t, 1), F32)
        q_aug, k_aug = zero, zero
        for idx, (qv, kv) in enumerate(((c1, one), (c2, one), (c3, one),
                                        (one, -c1), (one, -c2), (one, -c3))):
            hit = lane == HEAD_DIM + idx
            q_aug = jnp.where(hit, qv, q_aug)
            k_aug = jnp.where(hit, kv, k_aug)
        qa_ref[h] = jnp.where(low, qp * scale, q_aug).T.astype(BF16)
        ka_ref[h] = jnp.where(low, kp, k_aug).astype(BF16)
        v_aug = jnp.where(low, vp, 1.0) if h % 2 == 0 else jnp.where(low, 1.0, vp)
        va_ref[h, 0] = v_aug.T.astype(BF16)


def fox_prep(proj3, small3, f_bias):
    bsz, s, _ = proj3.shape
    t = FOX_PREP_ROWS
    half = B_WIDTH // 2
    fb = jnp.zeros((1, V7X_LANES), F32).at[0, SMALL_FL:SMALL_FL + B_HEADS].set(f_bias)

    def colspec(col):
        return pl.BlockSpec((None, t, half), lambda b, i, c=col // half: (b, i, c))

    assert t == FOX_TK
    k_spec = pl.BlockSpec((None, B_HEADS, t, V7X_LANES), lambda b, i: (b, 0, i, 0))
    k_shape = jax.ShapeDtypeStruct((bsz, B_HEADS, s, V7X_LANES), BF16)
    q_spec = pl.BlockSpec((None, B_HEADS, V7X_LANES, t), lambda b, i: (b, 0, 0, i))
    q_shape = jax.ShapeDtypeStruct((bsz, B_HEADS, V7X_LANES, s), BF16)
    v_spec = pl.BlockSpec((None, B_HEADS, 1, V7X_LANES, t), lambda b, i: (b, 0, i, 0, 0))
    v_shape = jax.ShapeDtypeStruct((bsz, B_HEADS, s // t, V7X_LANES, t), BF16)
    return pl.pallas_call(
        _fox_prep_kernel, grid=(bsz, s // t),
        in_specs=[pl.BlockSpec((None, t, V7X_LANES), lambda b, i: (b, i, 0)),
                  pl.BlockSpec((1, V7X_LANES), lambda b, i: (0, 0)),
                  colspec(COL_QB), colspec(COL_QB + half),
                  colspec(COL_KB), colspec(COL_KB + half),
                  colspec(COL_VB), colspec(COL_VB + half)],
        out_specs=[q_spec, k_spec, v_spec],
        out_shape=[q_shape, k_shape, v_shape],
        scratch_shapes=[pltpu.VMEM((8, V7X_LANES), F32)],
        compiler_params=_cparams(("parallel", "arbitrary")), name="fox_prep",
    )(small3, fb, proj3, proj3, proj3, proj3, proj3, proj3)


def _fox_kernel(q_ref, k_ref, v_ref, o_ref):
    tq, tk = FOX_TQ, FOX_TK
    qi = pl.program_id(2)
    nfull = qi * (tq // tk)
    krow = lax.broadcasted_iota(jnp.int32, (tk, tq), 0)
    qcol = lax.broadcasted_iota(jnp.int32, (tk, tq), 1)
    frow = lax.broadcasted_iota(jnp.int32, (V7X_LANES, tq), 0)
    nh = q_ref.shape[0]
    qs = [q_ref[e] for e in range(nh)]

    def heads(kt, carry, mask):
        rows = pl.ds(pl.multiple_of(kt * tk, tk), tk)
        scores = [jnp.dot(k_ref[e, rows, :], qs[e], preferred_element_type=F32) for e in range(nh)]
        out = []
        for e in range(nh):
            m, acc = carry[2 * e], carry[2 * e + 1]
            s = scores[e] if mask is None else jnp.where(mask, scores[e], NEG)
            m_new = jnp.maximum(m, jnp.max(s, axis=0, keepdims=True))
            alpha = jnp.exp2(m - m_new)
            p = jnp.exp2(s - m_new)
            acc = alpha * acc + jnp.dot(v_ref[e, kt], p.astype(BF16), preferred_element_type=F32)
            out += [m_new, acc]
        return tuple(out)

    m0 = jnp.full((1, tq), NEG, F32)
    acc0 = jnp.zeros((V7X_LANES, tq), F32)
    carry = lax.fori_loop(0, nfull, lambda kt, c: heads(kt, c, None), (m0, acc0) * nh)
    for dd in range(tq // tk):
        carry = heads(nfull + dd, carry, krow + dd * tk <= qcol)
    for g in range(nh // 2):
        even, odd = carry[4 * g + 1], carry[4 * g + 3]
        res = [acc / pltpu.roll(acc, HEAD_DIM, axis=0) for acc in (even, odd)]
        pair = jnp.where(frow < HEAD_DIM, res[0], res[1])
        o_ref[:, V7X_LANES * g:V7X_LANES * (g + 1)] = pair.T.astype(o_ref.dtype)


def fox_attention(q_aug, k_aug, v_aug):
    bsz, nh, s, _ = k_aug.shape
    tq, tk, hs = FOX_TQ, FOX_TK, FOX_HEADS_PER_STEP
    return pl.pallas_call(
        _fox_kernel, grid=(bsz, nh // hs, s // tq),
        in_specs=[pl.BlockSpec((None, hs, V7X_LANES, tq), lambda b, g, i: (b, g, 0, i)),
                  pl.BlockSpec((None, hs, s, V7X_LANES), lambda b, g, i: (b, g, 0, 0)),
                  pl.BlockSpec((None, hs, s // tk, V7X_LANES, tk), lambda b, g, i: (b, g, 0, 0, 0))],
        out_specs=pl.BlockSpec((None, tq, hs * HEAD_DIM), lambda b, g, i: (b, i, g)),
        out_shape=jax.ShapeDtypeStruct((bsz, s, B_WIDTH), BF16),
        compiler_params=_cparams(("parallel", "parallel", "arbitrary")), name="fox_attention",
    )(q_aug, k_aug, v_aug)


GLA_ROWS = 256


def _gla_chunk(rows, q_ref, k_ref, v_ref, r_ref, sm_bf, w2_ref, cb_ref, gn_ref, o_ref,
               st_scr, b_scr, k_scr, v_scr, tri, row_sub, row_ck):
    ck, sub = C_CHUNK, C_SUB
    nblk = ck // sub
    heads = range(C_HEADS)
    nt = (((1,), (1,)), ((), ()))
    kcols = [slice(C_DK_PAD * h, C_DK_PAD * (h + 1)) for h in heads]
    vcols = [slice(C_DV_PAD * h, C_DV_PAD * (h + 1)) for h in heads]
    g = [jnp.dot(sm_bf, w2_ref[h], preferred_element_type=F32) + cb_ref[h] for h in heads]
    parts = [_split3(_log_sigmoid(g[h]) / C_GATE_TAU) for h in heads]
    b = [(jnp.dot(tri, parts[h][0], preferred_element_type=F32)
          + jnp.dot(tri, parts[h][1], preferred_element_type=F32)
          + jnp.dot(tri, parts[h][2], preferred_element_type=F32)) for h in heads]
    qf = [q_ref[rows, kcols[h]].astype(F32) * (C_DK ** -0.5) for h in heads]
    kf = [k_ref[rows, kcols[h]].astype(F32) for h in heads]
    vb = [v_ref[rows, vcols[h]] for h in heads]
    st = [st_scr[h] for h in heads]
    for h in heads:
        b_scr[h] = b[h]
        k_scr[h] = kf[h]
        v_scr[h] = vb[h].astype(F32)
    o_inter = [lax.dot_general((qf[h] * jnp.exp(b[h])).astype(BF16), st[h].astype(BF16), nt,
                               preferred_element_type=F32) for h in heads]
    a = {}
    for h in heads:
        for blk in range(1, nblk):
            r0 = blk * sub
            ref_row = b_scr[h, r0 - 1:r0, :]
            kd = jnp.where(row_ck < r0, kf[h] * jnp.exp(jnp.minimum(ref_row - b[h], 0.0)), 0.0)
            qd = qf[h][r0:r0 + sub, :] * jnp.exp(b[h][r0:r0 + sub, :] - ref_row)
            a[h, blk] = lax.dot_general(qd.astype(BF16), kd.astype(BF16), nt, preferred_element_type=F32)
    strips = {(h, blk): jnp.dot(a[h, blk].astype(BF16), vb[h], preferred_element_type=F32)
              for h in heads for blk in range(1, nblk)}
    kdec = [(kf[h] * jnp.exp(b_scr[h, ck - 1:ck, :] - b[h])).astype(BF16) for h in heads]
    upd = [lax.dot_general(vb[h], kdec[h], (((0,), (0,)), ((), ())), preferred_element_type=F32)
           for h in heads]
    for h in heads:
        st_scr[h] = st[h] * jnp.exp(b_scr[h, ck - 1:ck, :]) + upd[h]
    for h in heads:
        outs = []
        for blk in range(nblk):
            r0 = blk * sub
            qb = qf[h][r0:r0 + sub, :]
            bb = b[h][r0:r0 + sub, :]
            ob = o_inter[h][r0:r0 + sub, :]
            if blk > 0:
                ob = ob + strips[h, blk]
            for j in range(sub):
                bj = b_scr[h, r0 + j:r0 + j + 1, :]
                kj = k_scr[h, r0 + j:r0 + j + 1, :]
                vj = v_scr[h, r0 + j:r0 + j + 1, :]
                x = qb * kj * jnp.exp(jnp.minimum(bb - bj, 0.0))
                colv = jnp.sum(x, axis=-1, keepdims=True)
                colv = jnp.where(row_sub >= j, colv, 0.0)
                ob = ob + colv * vj
            outs.append(ob)
        o = jnp.concatenate(outs, axis=0)
        ms = jnp.sum(o * o, axis=-1, keepdims=True) * (1.0 / C_DV)
        on = o * lax.rsqrt(ms + EPS) * gn_ref[...]
        rr = r_ref[rows, vcols[h]].astype(F32)
        o_ref[rows, vcols[h]] = (on * (rr / (1.0 + jnp.exp(-rr)))).astype(o_ref.dtype)


def _gla_kernel(q_ref, k_ref, v_ref, r_ref, sm_ref, w2_ref, cb_ref, gn_ref, o_ref,
                st_scr, b_scr, k_scr, v_scr):
    ck, sub = C_CHUNK, C_SUB

    @pl.when(pl.program_id(1) == 0)
    def _():
        st_scr[...] = jnp.zeros_like(st_scr)

    tri = _tril_ones(ck, BF16)
    row_sub = lax.broadcasted_iota(jnp.int32, (sub, 1), 0)
    row_ck = lax.broadcasted_iota(jnp.int32, (ck, 1), 0)

    def chunk(c, carry):
        rows = pl.ds(pl.multiple_of(c * ck, ck), ck)
        sm_bf = sm_ref[rows, :].astype(BF16)
        _gla_chunk(rows, q_ref, k_ref, v_ref, r_ref, sm_bf, w2_ref, cb_ref, gn_ref, o_ref,
                   st_scr, b_scr, k_scr, v_scr, tri, row_sub, row_ck)
        return carry

    lax.fori_loop(0, q_ref.shape[0] // ck, chunk, 0)


def gla_mixer(proj3, small3, w2p, cbp, gnp):
    bsz, s, _ = proj3.shape
    t = GLA_ROWS
    kw, vw = C_HEADS * C_DK_PAD, C_HEADS * C_DV_PAD

    def col(base, width):
        return pl.BlockSpec((None, t, width), lambda b, i, c=base // width: (b, i, c))

    return pl.pallas_call(
        _gla_kernel, grid=(bsz, s // t),
        in_specs=[col(COL_QC, kw), col(COL_KC, kw), col(COL_VC, vw), col(COL_RC, vw),
                  pl.BlockSpec((None, t, V7X_LANES), lambda b, i: (b, i, 0)),
                  pl.BlockSpec((C_HEADS, V7X_LANES, C_DK_PAD), lambda b, i: (0, 0, 0)),
                  pl.BlockSpec((C_HEADS, 1, C_DK_PAD), lambda b, i: (0, 0, 0)),
                  pl.BlockSpec((1, C_DV_PAD), lambda b, i: (0, 0))],
        out_specs=pl.BlockSpec((None, t, vw), lambda b, i: (b, i, 0)),
        out_shape=jax.ShapeDtypeStruct((bsz, s, vw), BF16),
        scratch_shapes=[pltpu.VMEM((C_HEADS, C_DV_PAD, C_DK_PAD), F32),
                        pltpu.VMEM((C_HEADS, C_CHUNK, C_DK_PAD), F32),
                        pltpu.VMEM((C_HEADS, C_CHUNK, C_DK_PAD), F32),
                        pltpu.VMEM((C_HEADS, C_CHUNK, C_DV_PAD), F32)],
        compiler_params=_cparams(("parallel", "arbitrary")), name="gla_mixer",
    )(proj3, proj3, proj3, proj3, small3, w2p, cbp, gnp)


def _out_proj_kernel(x_ref, a_ref, b_ref, c_ref, wa_ref, wb_ref, wc_ref, o_ref):
    acc = jnp.dot(a_ref[...], wa_ref[...], preferred_element_type=F32)
    acc += jnp.dot(b_ref[...], wb_ref[...], preferred_element_type=F32)
    acc += jnp.dot(c_ref[...], wc_ref[...], preferred_element_type=F32)
    o_ref[...] = x_ref[...] + acc


def out_proj(x2d, oa, ob, oc, wa, wb, wc, *, tm=1024, tn=512):
    m, n = x2d.shape

    def rows(a):
        return pl.BlockSpec((tm, a.shape[1]), lambda i, j: (i, 0))

    def cols(w):
        return pl.BlockSpec((w.shape[0], tn), lambda i, j: (0, j))

    xs = pl.BlockSpec((tm, tn), lambda i, j: (i, j))
    return pl.pallas_call(
        _out_proj_kernel, grid=(m // tm, n // tn),
        in_specs=[xs, rows(oa), rows(ob), rows(oc), cols(wa), cols(wb), cols(wc)],
        out_specs=xs, out_shape=jax.ShapeDtypeStruct((m, n), F32),
        compiler_params=_cparams(("parallel", "arbitrary")), name="out_proj",
    )(x2d, oa, ob, oc, wa, wb, wc)


CROSS_ROWS = 512


def _cross_kernel(x_ref, g_ref, wq_ref, k_ref, v_ref, wo_ref, o_ref):
    x = x_ref[...]
    h = _rms_rows(x, g_ref[...]).astype(BF16)
    q = jnp.dot(h, wq_ref[...], preferred_element_type=F32).astype(BF16)
    cols = [slice(CROSS_DH * hd, CROSS_DH * (hd + 1)) for hd in range(CROSS_HEADS)]
    scores = [lax.dot_general(q[:, cs], k_ref[:, cs], (((1,), (1,)), ((), ())),
                              preferred_element_type=F32) * (CROSS_DH ** -0.5) for cs in cols]
    heads = []
    for cs, logits in zip(cols, scores):
        m = jnp.max(logits, axis=-1, keepdims=True)
        p = jnp.exp(logits - m)
        ssum = jnp.sum(p, axis=-1, keepdims=True)
        pv = jnp.dot(p.astype(BF16), v_ref[:, cs], preferred_element_type=F32)
        heads.append((pv / ssum).astype(BF16))
    o = jnp.concatenate(heads, axis=1)
    o_ref[...] = x + jnp.dot(o, wo_ref[...], preferred_element_type=F32)


def cross_attention(x2d, gain, wq, kv, wo, *, seq):
    m, d = x2d.shape
    tm = CROSS_ROWS
    mem_len = kv.shape[1]
    per_batch = seq // tm
    return pl.pallas_call(
        _cross_kernel, grid=(m // tm,),
        in_specs=[pl.BlockSpec((tm, d), lambda i: (i, 0)),
                  pl.BlockSpec((1, d), lambda i: (0, 0)),
                  pl.BlockSpec((d, CROSS_WIDTH), lambda i: (0, 0)),
                  pl.BlockSpec((None, mem_len, CROSS_WIDTH), lambda i: (i // per_batch, 0, 0)),
                  pl.BlockSpec((None, mem_len, CROSS_WIDTH), lambda i: (i // per_batch, 0, 1)),
                  pl.BlockSpec((CROSS_WIDTH, d), lambda i: (0, 0))],
        out_specs=pl.BlockSpec((tm, d), lambda i: (i, 0)),
        out_shape=jax.ShapeDtypeStruct((m, d), F32),
        compiler_params=_cparams(("parallel",)), name="cross_attention",
    )(x2d, gain.reshape(1, d), wq, kv, kv, wo)


FFN_ROWS = 512
FFN_CHUNK = FFN_ROWS
FFN_HALO = 16


def _ffn_kernel(x_ref, xh_ref, g_ref, wv_ref, wg_ref, cw_ref, cb_ref, wd_ref, gf_ref, o_ref,
                h_scr, hh_scr, *, final):
    j = pl.program_id(1)

    @pl.when(j == 0)
    def _():
        _norm_to_scratch(x_ref, g_ref, h_scr)
        hh_scr[...] = _rms_rows(xh_ref[...], g_ref[...]).astype(BF16)
        o_ref[...] = x_ref[...]

    rc = FFN_CHUNK
    nchunk = h_scr.shape[0] // rc
    prev = jnp.dot(hh_scr[...], wg_ref[...], preferred_element_type=F32)
    row = lax.broadcasted_iota(jnp.int32, (rc, wg_ref.shape[1]), 0)
    vals, gates = [], []
    for c in range(nchunk):
        h = h_scr[c * rc:(c + 1) * rc, :]
        gates.append(jnp.dot(h, wg_ref[...], preferred_element_type=F32))
        vals.append(jnp.dot(h, wv_ref[...], preferred_element_type=F32))
    for c in range(nchunk):
        rows = slice(c * rc, (c + 1) * rc)
        val, gate = vals[c], gates[c]
        prev1 = prev[prev.shape[0] - 1:, :]
        prev2 = prev[prev.shape[0] - 2:prev.shape[0] - 1, :]
        g1 = jnp.where(row == 0, prev1, pltpu.roll(gate, 1, axis=0))
        g2 = jnp.where(row == 0, prev2, jnp.where(row == 1, prev1, pltpu.roll(gate, 2, axis=0)))
        gc = cb_ref[...] + cw_ref[0:1, :] * g2
        gc = gc + cw_ref[1:2, :] * g1
        gc = gc + cw_ref[2:3, :] * gate
        act = (gc / (1.0 + jnp.exp(-gc)) * val).astype(BF16)
        o_ref[rows, :] += jnp.dot(act, wd_ref[...], preferred_element_type=F32)
        prev = gate

    if final:
        @pl.when(j == pl.num_programs(1) - 1)
        def _():
            def body(r, c):
                sl = pl.ds(pl.multiple_of(r * NORM_ROWS, NORM_ROWS), NORM_ROWS)
                o_ref[sl, :] = _rms_rows(o_ref[sl, :], gf_ref[...])
                return c
            lax.fori_loop(0, o_ref.shape[0] // NORM_ROWS, body, 0)


def conv_ffn(x2d, gain, wv, wg, cw, cb, wd, gain_final, *, seq, final):
    m, d = x2d.shape
    tm, tf = FFN_ROWS, FF_TILE
    nt = m // tm
    tail = x2d.reshape(nt, tm, d)[:, tm - FFN_HALO:, :]
    halo = jnp.concatenate([jnp.zeros_like(tail[:1]), tail[:-1]], axis=0)
    starts = (jnp.arange(nt) % (seq // tm) == 0)[:, None, None]
    halo = jnp.where(starts, 0.0, halo)
    ff = wv.shape[1]
    return pl.pallas_call(
        functools.partial(_ffn_kernel, final=final), grid=(nt, ff // tf),
        in_specs=[pl.BlockSpec((tm, d), lambda i, j: (i, 0)),
                  pl.BlockSpec((None, FFN_HALO, d), lambda i, j: (i, 0, 0)),
                  pl.BlockSpec((1, d), lambda i, j: (0, 0)),
                  pl.BlockSpec((d, tf), lambda i, j: (0, j)),
                  pl.BlockSpec((d, tf), lambda i, j: (0, j)),
                  pl.BlockSpec((CONV_W, tf), lambda i, j: (0, j)),
                  pl.BlockSpec((1, tf), lambda i, j: (0, j)),
                  pl.BlockSpec((tf, d), lambda i, j: (j, 0)),
                  pl.BlockSpec((1, d), lambda i, j: (0, 0))],
        out_specs=pl.BlockSpec((tm, d), lambda i, j: (i, 0)),
        out_shape=jax.ShapeDtypeStruct((m, d), F32),
        scratch_shapes=[pltpu.VMEM((tm, d), BF16), pltpu.VMEM((FFN_HALO, d), BF16)],
        compiler_params=_cparams(("parallel", "arbitrary")), name="conv_ffn",
    )(x2d, halo, gain.reshape(1, d), wv, wg, cw, cb.reshape(1, ff), wd, gain_final.reshape(1, d))


def _pad_heads(w, heads, width, padded):
    lead = w.shape[:-1]
    w = w.reshape(lead + (heads, width))
    w = jnp.pad(w, [(0, 0)] * len(lead) + [(0, 0), (0, padded - width)])
    return w.reshape(lead + (heads * padded,))


def _layer_params(l, w_in, c_gate_w2, c_gate_b, c_norm, w_out, w_up, conv_w, conv_b, w_down):
    splits = np.cumsum([A_WIDTH] * 3 + [B_WIDTH] * 3 + [B_HEADS, C_KW, C_KW, C_VW, C_VW, C_GATE_RANK])
    fl, qc, kc, vc, rc, gl = jnp.split(w_in[l][:, splits[5]:], (splits[6:11] - splits[5]).tolist(), axis=1)
    w_all = jnp.concatenate([
        w_in[l][:, :splits[2]],
        _pad_heads(vc, C_HEADS, C_DV, C_DV_PAD), _pad_heads(rc, C_HEADS, C_DV, C_DV_PAD),
        w_in[l][:, splits[2]:splits[5]],
        _pad_heads(qc, C_HEADS, C_DK, C_DK_PAD), _pad_heads(kc, C_HEADS, C_DK, C_DK_PAD)],
        axis=1).astype(BF16)
    w_small = jnp.zeros((D_MODEL, V7X_LANES), F32)
    w_small = w_small.at[:, SMALL_FL:SMALL_FL + B_HEADS].set(fl)
    w_small = w_small.at[:, SMALL_GL:SMALL_GL + C_GATE_RANK].set(gl).astype(BF16)
    w2 = _pad_heads(c_gate_w2[l], C_HEADS, C_DK, C_DK_PAD).reshape(C_GATE_RANK, C_HEADS, C_DK_PAD)
    w2p = jnp.zeros((C_HEADS, V7X_LANES, C_DK_PAD), F32)
    w2p = w2p.at[:, SMALL_GL:SMALL_GL + C_GATE_RANK, :].set(jnp.transpose(w2, (1, 0, 2))).astype(BF16)
    cbp = _pad_heads(c_gate_b[l], C_HEADS, C_DK, C_DK_PAD).reshape(C_HEADS, 1, C_DK_PAD)
    gnp = jnp.pad(c_norm[l], (0, C_DV_PAD - C_DV)).reshape(1, C_DV_PAD)
    wo = w_out[l]
    wa = wo[:A_WIDTH].astype(BF16)
    wb = wo[A_WIDTH:A_WIDTH + B_WIDTH].astype(BF16)
    wc = wo[A_WIDTH + B_WIDTH:].reshape(C_HEADS, C_DV, D_MODEL)
    wc = jnp.pad(wc, ((0, 0), (0, C_DV_PAD - C_DV), (0, 0))).reshape(C_HEADS * C_DV_PAD, D_MODEL).astype(BF16)
    fpad = D_FF_PAD - D_FF
    wv = jnp.pad(w_up[l][:, :D_FF], ((0, 0), (0, fpad))).astype(BF16)
    wg = jnp.pad(w_up[l][:, D_FF:], ((0, 0), (0, fpad))).astype(BF16)
    cw = jnp.pad(conv_w[l], ((0, 0), (0, fpad)))
    cb = jnp.pad(conv_b[l], (0, fpad))
    wd = jnp.pad(w_down[l], ((0, fpad), (0, 0))).astype(BF16)
    return dict(w_all=w_all, w_small=w_small, w2p=w2p, cbp=cbp, gnp=gnp,
                wa=wa, wb=wb, wc=wc, wv=wv, wg=wg, cw=cw, cb=cb, wd=wd)


def hybrid_mixer(x2d, gain, p, f_bias, bias_a, *, bsz, seq):
    qkv_a, proj, small = in_proj(x2d, gain, p["w_all"], p["w_small"])
    proj3 = proj.reshape(bsz, seq, PROJ_PAD)
    small3 = small.reshape(bsz, seq, V7X_LANES)
    o_a = dilated_mixture(qkv_a.reshape(bsz, seq, PROJ_A), bias_a).reshape(bsz * seq, A_WIDTH)
    o_b = fox_attention(*fox_prep(proj3, small3, f_bias)).reshape(bsz * seq, B_WIDTH)
    o_c = gla_mixer(proj3, small3, p["w2p"], p["cbp"], p["gnp"]).reshape(bsz * seq, C_HEADS * C_DV_PAD)
    return out_proj(x2d, o_a, o_b, o_c, p["wa"], p["wb"], p["wc"])


def kernel(x, mem, rel_table, mem_norm, norm_final, norm_mix, w_in, f_bias, c_gate_w2, c_gate_b,
           c_norm, w_out, norm_cross, w_cq, w_ckv, w_co, norm_ffn, w_up, conv_w, conv_b, w_down):
    bsz, seq, d = x.shape
    depth = w_in.shape[0]
    mem_len = mem.shape[1]
    assert d == D_MODEL and seq % (A_SPAN * A_PATTERNS[-1][1]) == 0 and seq % 1024 == 0
    bias_a = jnp.stack([_dilated_bias(rel_table, dil) for _, dil in A_PATTERNS], axis=0)
    x2d = x.reshape(bsz * seq, d)
    mem2d = mem.reshape(bsz * mem_len, d)
    for l in range(depth):
        p = _layer_params(l, w_in, c_gate_w2, c_gate_b, c_norm, w_out, w_up, conv_w, conv_b, w_down)
        x2d = hybrid_mixer(x2d, norm_mix[l], p, f_bias[l], bias_a, bsz=bsz, seq=seq)
        kv = norm_matmul(mem2d, mem_norm, w_ckv[l].astype(BF16), tm=min(512, bsz * mem_len), tn=512,
                         name="mem_kv").reshape(bsz, mem_len, 2 * CROSS_WIDTH)
        x2d = cross_attention(x2d, norm_cross[l], w_cq[l].astype(BF16), kv, w_co[l].astype(BF16), seq=seq)
        x2d = conv_ffn(x2d, norm_ffn[l], p["wv"], p["wg"], p["cw"], p["cb"], p["wd"], norm_final,
                       seq=seq, final=l == depth - 1)
    return x2d.reshape(bsz, seq, d)
```

```python
import functools
import math

import numpy as np
import jax
import jax.numpy as jnp
from jax import lax
from jax.experimental import pallas as pl
from jax.experimental.pallas import tpu as pltpu

F32 = jnp.float32
BF16 = jnp.bfloat16

V7X_LANES = 128
V7X_VMEM_LIMIT_BYTES = 56 * 1024 * 1024

D_MODEL = 2048
HEAD_DIM = 64
A_WIDTH = 3 * D_MODEL // 8
A_HEADS = A_WIDTH // HEAD_DIM
A_PATTERNS = ((128, 1), (512, 4), (2048, 16))
A_SPAN = 128
B_WIDTH = D_MODEL // 4
B_HEADS = B_WIDTH // HEAD_DIM
C_HEADS = 4
C_VW = 3 * D_MODEL // 8
C_DV = C_VW // C_HEADS
C_DK = C_DV // 2
C_KW = C_HEADS * C_DK
C_GATE_RANK = 16
C_GATE_TAU = 16.0
C_CHUNK = 64
C_SUB = 8
C_DK_PAD = 128
C_DV_PAD = 256
CROSS_HEADS = 4
CROSS_DH = 128
CROSS_WIDTH = CROSS_HEADS * CROSS_DH
D_FF = ((8 * D_MODEL // 3 + 127) // 128) * 128
FF_TILE = 512
D_FF_PAD = ((D_FF + FF_TILE - 1) // FF_TILE) * FF_TILE
CONV_W = 3
REL_BUCKETS = 32
REL_MAX_DIST = 2048
EPS = 1e-6
NEG = -1e30
LOG2E = 1.4426950408889634

COL_VC = 0
COL_RC = COL_VC + C_HEADS * C_DV_PAD
COL_QC = COL_RC + C_HEADS * C_DV_PAD
COL_KC = COL_QC + C_HEADS * C_DK_PAD
COL_QA = COL_KC + C_HEADS * C_DK_PAD
COL_KA = COL_QA + A_WIDTH
COL_VA = COL_KA + A_WIDTH
COL_QB = COL_VA + A_WIDTH
COL_KB = COL_QB + B_WIDTH
COL_VB = COL_KB + B_WIDTH
PROJ_PAD = COL_VB + B_WIDTH
SMALL_FL = 0
SMALL_GL = B_HEADS


def _cparams(sem):
    return pltpu.CompilerParams(dimension_semantics=sem,
                                vmem_limit_bytes=V7X_VMEM_LIMIT_BYTES)


def _rms_rows(x, g):
    ms = jnp.mean(x * x, axis=-1, keepdims=True)
    return x * lax.rsqrt(ms + EPS) * g


def _split3(x):
    hi = x.astype(BF16)
    r1 = x - hi.astype(F32)
    mid = r1.astype(BF16)
    lo = (r1 - mid.astype(F32)).astype(BF16)
    return hi, mid, lo


def _log_sigmoid(x):
    return jnp.minimum(x, 0.0) - jnp.log(1.0 + jnp.exp(-jnp.abs(x)))


def _tril_ones(n, dtype):
    r = lax.broadcasted_iota(jnp.int32, (n, n), 0)
    c = lax.broadcasted_iota(jnp.int32, (n, n), 1)
    return jnp.where(r >= c, 1.0, 0.0).astype(dtype)


NORM_ROWS = 256


def _norm_to_scratch(x_ref, g_ref, h_scr):
    def body(r, c):
        sl = pl.ds(pl.multiple_of(r * NORM_ROWS, NORM_ROWS), NORM_ROWS)
        h_scr[sl, :] = _rms_rows(x_ref[sl, :], g_ref[...]).astype(BF16)
        return c
    lax.fori_loop(0, x_ref.shape[0] // NORM_ROWS, body, 0)


def _norm_matmul_kernel(x_ref, g_ref, w_ref, o_ref, h_scr):
    @pl.when(pl.program_id(1) == 0)
    def _():
        _norm_to_scratch(x_ref, g_ref, h_scr)
    o_ref[...] = jnp.dot(h_scr[...], w_ref[...],
                         preferred_element_type=F32).astype(o_ref.dtype)


def norm_matmul(x2d, gain, w, *, tm, tn, name):
    m, k = x2d.shape
    n = w.shape[1]
    return pl.pallas_call(
        _norm_matmul_kernel, grid=(m // tm, n // tn),
        in_specs=[pl.BlockSpec((tm, k), lambda i, j: (i, 0)),
                  pl.BlockSpec((1, k), lambda i, j: (0, 0)),
                  pl.BlockSpec((k, tn), lambda i, j: (0, j))],
        out_specs=pl.BlockSpec((tm, tn), lambda i, j: (i, j)),
        out_shape=jax.ShapeDtypeStruct((m, n), BF16),
        scratch_shapes=[pltpu.VMEM((tm, k), BF16)],
        compiler_params=_cparams(("parallel", "arbitrary")), name=name)(x2d, gain.reshape(1, k), w)


IN_PROJ_ROWS = 1024
IN_PROJ_TILE = A_WIDTH


def _in_proj_kernel(x_ref, g_ref, w_ref, ws_ref, o_ref, os_ref, h_scr):
    @pl.when(pl.program_id(1) == 0)
    def _():
        _norm_to_scratch(x_ref, g_ref, h_scr)
        os_ref[...] = jnp.dot(h_scr[...], ws_ref[...], preferred_element_type=F32)

    o_ref[...] = jnp.dot(h_scr[...], w_ref[...], preferred_element_type=F32).astype(o_ref.dtype)


def in_proj(x2d, gain, w_all, w_small):
    m, k = x2d.shape
    tm, tn = IN_PROJ_ROWS, IN_PROJ_TILE
    ns = w_small.shape[1]
    return pl.pallas_call(
        _in_proj_kernel, grid=(m // tm, PROJ_PAD // tn),
        in_specs=[pl.BlockSpec((tm, k), lambda i, j: (i, 0)),
                  pl.BlockSpec((1, k), lambda i, j: (0, 0)),
                  pl.BlockSpec((k, tn), lambda i, j: (0, j)),
                  pl.BlockSpec((k, ns), lambda i, j: (0, 0))],
        out_specs=[pl.BlockSpec((tm, tn), lambda i, j: (i, j)),
                   pl.BlockSpec((tm, ns), lambda i, j: (i, 0))],
        out_shape=[jax.ShapeDtypeStruct((m, PROJ_PAD), BF16),
                   jax.ShapeDtypeStruct((m, ns), F32)],
        scratch_shapes=[pltpu.VMEM((tm, k), BF16)],
        compiler_params=_cparams(("parallel", "arbitrary")), name="in_proj",
    )(x2d, gain.reshape(1, k), w_all, w_small)


def _t5_bucket(dist):
    max_exact = REL_BUCKETS // 2
    d = np.maximum(dist, 1).astype(np.float32)
    large = max_exact + (np.log(d / max_exact) / math.log(REL_MAX_DIST / max_exact)
                         * (REL_BUCKETS - max_exact)).astype(np.int32)
    return np.where(dist < max_exact, dist, np.minimum(large, REL_BUCKETS - 1)).astype(np.int32)


def _dilated_bias(rel_table, dilation):
    span = A_SPAN
    steps = np.arange(2 * span, -span, -1)
    valid = (steps >= 0) & (steps <= span)
    per_step = rel_table[_t5_bucket(np.clip(steps, 0, span) * dilation)].astype(F32)
    per_step = jnp.where(valid[:, None], per_step * LOG2E, NEG).T
    later = jnp.stack([per_step[:, 2 * span - i - span: 4 * span - i - span] for i in range(span)],
                      axis=1)
    first = jnp.where((np.arange(2 * span) >= span)[None, None, :], later, NEG)
    return jnp.stack([first, later], axis=0)


DIL_TILE = A_SPAN * max(d for _, d in A_PATTERNS)
DIL_UNITS = DIL_TILE // A_SPAN
DIL_GROUP = 4


def _dilated_kernel(qb_ref, kp_ref, kc_ref, vp_ref, vc_ref, bias_ref, o_ref,
                    q_ref, kk_scr, vv_scr, m_scr, n_scr, s_scr):
    tile, span = DIL_TILE, A_SPAN
    first_tile = pl.program_id(1) == 0
    q_ref[...] = qb_ref[...].astype(F32)
    kk_scr[0:tile, :] = kp_ref[...].astype(F32)
    kk_scr[tile:2 * tile, :] = kc_ref[...].astype(F32)
    vv_scr[0:tile, :] = vp_ref[...].astype(F32)
    vv_scr[tile:2 * tile, :] = vc_ref[...].astype(F32)
    lane_q = lax.broadcasted_iota(jnp.int32, (span, V7X_LANES), 1)
    lane_k = lax.broadcasted_iota(jnp.int32, (2 * span, V7X_LANES), 1)
    low_q = lane_q < HEAD_DIM
    low_k = lane_k < HEAD_DIM
    scale = HEAD_DIM ** -0.5 * LOG2E
    qmask = (jnp.where(low_q, scale, 0.0), jnp.where(low_q, 0.0, scale))
    nt = (((1,), (1,)), ((), ()))

    for pat, (_, d) in enumerate(A_PATTERNS):
        for u0 in range(0, DIL_UNITS, DIL_GROUP):
            units = []
            for u in range(u0, u0 + DIL_GROUP):
                n, r = divmod(u, d)
                q0 = r + span * d * n
                k0 = tile + q0 - span * d
                if d == 1:
                    qrows, krows = pl.ds(q0, span), pl.ds(k0, 2 * span)
                else:
                    qrows, krows = pl.ds(q0, span, stride=d), pl.ds(k0, 2 * span, stride=d)
                qf = q_ref[qrows, :]
                kb = kk_scr[krows, :].astype(BF16)
                sel = jnp.where(first_tile, 0, 1) if n == 0 else 1
                logits = [lax.dot_general((qf * qmask[e]).astype(BF16), kb, nt,
                                          preferred_element_type=F32) + bias_ref[pat, sel, e]
                          for e in range(2)]
                units.append((qrows, krows, logits))
            results = []
            for qrows, krows, logits in units:
                vf = vv_scr[krows, :]
                pvs, ms = [], []
                for e in range(2):
                    m = jnp.max(logits[e], axis=-1, keepdims=True)
                    p = jnp.exp2(logits[e] - m)
                    ve = (jnp.where(low_k, vf, 1.0) if e == 0 else jnp.where(low_k, 1.0, vf)).astype(BF16)
                    pvs.append(jnp.dot(p.astype(BF16), ve, preferred_element_type=F32))
                    ms.append(m)
                results.append((qrows, pvs, ms))
            for qrows, pvs, ms in results:
                num = jnp.where(low_q, pvs[0], pvs[1])
                den = pltpu.roll(jnp.where(low_q, pvs[1], pvs[0]), HEAD_DIM, axis=1)
                mx = jnp.where(low_q, ms[0], ms[1])
                if pat > 0:
                    m_old = m_scr[qrows, :]
                    m_new = jnp.maximum(m_old, mx)
                    a = jnp.exp2(m_old - m_new)
                    b = jnp.exp2(mx - m_new)
                    num = a * n_scr[qrows, :] + b * num
                    den = a * s_scr[qrows, :] + b * den
                    mx = m_new
                m_scr[qrows, :] = mx
                n_scr[qrows, :] = num
                s_scr[qrows, :] = den
    o_ref[...] = (n_scr[...] / s_scr[...]).astype(o_ref.dtype)


def dilated_mixture(proj3, bias):
    bsz, s, _ = proj3.shape
    pairs = A_HEADS // 2
    blk = (None, DIL_TILE, V7X_LANES)
    cq, ck, cv = COL_QA // V7X_LANES, COL_KA // V7X_LANES, COL_VA // V7X_LANES

    def cur(col):
        return pl.BlockSpec(blk, lambda b, t, g: (b, t, col + g))

    def prev(col):
        return pl.BlockSpec(blk, lambda b, t, g: (b, jnp.maximum(t - 1, 0), col + g))

    tile_f32 = pltpu.VMEM((DIL_TILE, V7X_LANES), F32)
    both_f32 = pltpu.VMEM((2 * DIL_TILE, V7X_LANES), F32)
    return pl.pallas_call(
        _dilated_kernel, grid=(bsz, s // DIL_TILE, pairs),
        in_specs=[cur(cq), prev(ck), cur(ck), prev(cv), cur(cv),
                  pl.BlockSpec((len(A_PATTERNS), 2, 2, A_SPAN, 2 * A_SPAN),
                               lambda b, t, g: (0, 0, g, 0, 0))],
        out_specs=pl.BlockSpec(blk, lambda b, t, g: (b, t, g)),
        out_shape=jax.ShapeDtypeStruct((bsz, s, A_WIDTH), BF16),
        scratch_shapes=[tile_f32, both_f32, both_f32, tile_f32, tile_f32, tile_f32],
        compiler_params=_cparams(("parallel", "parallel", "arbitrary")), name="dilated_mixture",
    )(proj3, proj3, proj3, proj3, proj3, bias)


FOX_PREP_ROWS = 512
FOX_TQ = 512
FOX_TK = 512
FOX_HEADS_PER_STEP = 4


def _fox_prep_kernel(sm_ref, fb_ref, q0_ref, q1_ref, k0_ref, k1_ref, v0_ref, v1_ref,
                     qa_ref, ka_ref, va_ref, carry_scr):
    t = sm_ref.shape[0]

    @pl.when(pl.program_id(1) == 0)
    def _():
        carry_scr[...] = jnp.zeros_like(carry_scr)

    x = _log_sigmoid(sm_ref[...] + fb_ref[...])
    tri = _tril_ones(t, BF16)
    hi, mid, lo = _split3(x)
    c = (jnp.dot(tri, hi, preferred_element_type=F32)
         + jnp.dot(tri, mid, preferred_element_type=F32)
         + jnp.dot(tri, lo, preferred_element_type=F32)) + carry_scr[0:1, :]
    carry_scr[...] = jnp.broadcast_to(c[t - 1:t, :], carry_scr.shape)
    chi, cmid, clo = [p.astype(F32) for p in _split3(c * LOG2E)]

    lane = lax.broadcasted_iota(jnp.int32, (t, V7X_LANES), 1)
    low = lane < HEAD_DIM
    scale = HEAD_DIM ** -0.5 * LOG2E
    q_refs, k_refs, v_refs = (q0_ref, q1_ref), (k0_ref, k1_ref), (v0_ref, v1_ref)
    for h in range(B_HEADS):
        src, pair = divmod(h // 2, 2)
        cs = slice(V7X_LANES * pair, V7X_LANES * (pair + 1))
        qp = q_refs[src][:, cs].astype(F32)
        kp = k_refs[src][:, cs].astype(F32)
        vp = v_refs[src][:, cs].astype(F32)
        if h % 2 == 1:
            qp = pltpu.roll(qp, HEAD_DIM, axis=1)
            kp = pltpu.roll(kp, HEAD_DIM, axis=1)
        c1, c2, c3 = chi[:, h:h + 1], cmid[:, h:h + 1], clo[:, h:h + 1]
        one = jnp.ones((t, 1), F32)
        zero = jnp.zeros((t, 1), F32)
        q_aug, k_aug = zero, zero
        for idx, (qv, kv) in enumerate(((c1, one), (c2, one), (c3, one),
                                        (one, -c1), (one, -c2), (one, -c3))):
            hit = lane == HEAD_DIM + idx
            q_aug = jnp.where(hit, qv, q_aug)
            k_aug = jnp.where(hit, kv, k_aug)
        qa_ref[h] = jnp.where(low, qp * scale, q_aug).T.astype(BF16)
        ka_ref[h] = jnp.where(low, kp, k_aug).astype(BF16)
        v_aug = jnp.where(low, vp, 1.0) if h % 2 == 0 else jnp.where(low, 1.0, vp)
        va_ref[h, 0] = v_aug.T.astype(BF16)


def fox_prep(proj3, small3, f_bias):
    bsz, s, _ = proj3.shape
    t = FOX_PREP_ROWS
    half = B_WIDTH // 2
    fb = jnp.zeros((1, V7X_LANES), F32).at[0, SMALL_FL:SMALL_FL + B_HEADS].set(f_bias)

    def colspec(col):
        return pl.BlockSpec((None, t, half), lambda b, i, c=col // half: (b, i, c))

    assert t == FOX_TK
    k_spec = pl.BlockSpec((None, B_HEADS, t, V7X_LANES), lambda b, i: (b, 0, i, 0))
    k_shape = jax.ShapeDtypeStruct((bsz, B_HEADS, s, V7X_LANES), BF16)
    q_spec = pl.BlockSpec((None, B_HEADS, V7X_LANES, t), lambda b, i: (b, 0, 0, i))
    q_shape = jax.ShapeDtypeStruct((bsz, B_HEADS, V7X_LANES, s), BF16)
    v_spec = pl.BlockSpec((None, B_HEADS, 1, V7X_LANES, t), lambda b, i: (b, 0, i, 0, 0))
    v_shape = jax.ShapeDtypeStruct((bsz, B_HEADS, s // t, V7X_LANES, t), BF16)
    return pl.pallas_call(
        _fox_prep_kernel, grid=(bsz, s // t),
        in_specs=[pl.BlockSpec((None, t, V7X_LANES), lambda b, i: (b, i, 0)),
                  pl.BlockSpec((1, V7X_LANES), lambda b, i: (0, 0)),
                  colspec(COL_QB), colspec(COL_QB + half),
                  colspec(COL_KB), colspec(COL_KB + half),
                  colspec(COL_VB), colspec(COL_VB + half)],
        out_specs=[q_spec, k_spec, v_spec],
        out_shape=[q_shape, k_shape, v_shape],
        scratch_shapes=[pltpu.VMEM((8, V7X_LANES), F32)],
        compiler_params=_cparams(("parallel", "arbitrary")), name="fox_prep",
    )(small3, fb, proj3, proj3, proj3, proj3, proj3, proj3)


def _fox_kernel(q_ref, k_ref, v_ref, o_ref, m_scr, acc_scr):
    tq, tk = FOX_TQ, FOX_TK
    qi = pl.program_id(2)
    nfull = qi * (tq // tk)
    krow = lax.broadcasted_iota(jnp.int32, (tk, tq), 0)
    qcol = lax.broadcasted_iota(jnp.int32, (tk, tq), 1)
    frow = lax.broadcasted_iota(jnp.int32, (V7X_LANES, tq), 0)
    nh = q_ref.shape[0]
    qs = [q_ref[e] for e in range(nh)]
    m_scr[...] = jnp.full(m_scr.shape, NEG, F32)
    acc_scr[...] = jnp.zeros(acc_scr.shape, F32)

    def heads(kt, mask):
        rows = pl.ds(pl.multiple_of(kt * tk, tk), tk)
        scores = [jnp.dot(k_ref[e, rows, :], qs[e], preferred_element_type=F32) for e in range(nh)]
        for e in range(nh):
            m = m_scr[e, 0:1, :]
            s = scores[e] if mask is None else jnp.where(mask, scores[e], NEG)
            m_new = jnp.maximum(m, jnp.max(s, axis=0, keepdims=True))
            alpha = jnp.exp2(m - m_new)
            p = jnp.exp2(s - m_new)
            pv = jnp.dot(v_ref[e, kt], p.astype(BF16), preferred_element_type=F32)
            acc_scr[e] = alpha * acc_scr[e] + pv
            m_scr[e] = jnp.broadcast_to(m_new, (8, tq))

    def body(kt, c):
        heads(kt, None)
        return c

    lax.fori_loop(0, nfull, body, 0)
    for dd in range(tq // tk):
        heads(nfull + dd, krow + dd * tk <= qcol)
    for g in range(nh // 2):
        even, odd = acc_scr[2 * g], acc_scr[2 * g + 1]
        res = [acc / pltpu.roll(acc, HEAD_DIM, axis=0) for acc in (even, odd)]
        pair = jnp.where(frow < HEAD_DIM, res[0], res[1])
        o_ref[:, V7X_LANES * g:V7X_LANES * (g + 1)] = pair.T.astype(o_ref.dtype)


def fox_attention(q_aug, k_aug, v_aug):
    bsz, nh, s, _ = k_aug.shape
    tq, tk, hs = FOX_TQ, FOX_TK, FOX_HEADS_PER_STEP
    return pl.pallas_call(
        _fox_kernel, grid=(bsz, nh // hs, s // tq),
        in_specs=[pl.BlockSpec((None, hs, V7X_LANES, tq), lambda b, g, i: (b, g, 0, i)),
                  pl.BlockSpec((None, hs, s, V7X_LANES), lambda b, g, i: (b, g, 0, 0)),
                  pl.BlockSpec((None, hs, s // tk, V7X_LANES, tk), lambda b, g, i: (b, g, 0, 0, 0))],
        out_specs=pl.BlockSpec((None, tq, hs * HEAD_DIM), lambda b, g, i: (b, i, g)),
        out_shape=jax.ShapeDtypeStruct((bsz, s, B_WIDTH), BF16),
        scratch_shapes=[pltpu.VMEM((hs, 8, tq), F32), pltpu.VMEM((hs, V7X_LANES, tq), F32)],
        compiler_params=_cparams(("parallel", "parallel", "arbitrary")), name="fox_attention",
    )(q_aug, k_aug, v_aug)


GLA_ROWS = 512


def _gla_chunk(rows, q_ref, k_ref, v_ref, r_ref, sm_bf, w2_ref, cb_ref, gn_ref, o_ref,
               st_scr, b_scr, k_scr, tri, row_sub, row_ck, col_ck):
    ck, sub = C_CHUNK, C_SUB
    nblk = ck // sub
    heads = range(C_HEADS)
    nt = (((1,), (1,)), ((), ()))
    kcols = [slice(C_DK_PAD * h, C_DK_PAD * (h + 1)) for h in heads]
    vcols = [slice(C_DV_PAD * h, C_DV_PAD * (h + 1)) for h in heads]
    g = [jnp.dot(sm_bf, w2_ref[h], preferred_element_type=F32) + cb_ref[h] for h in heads]
    parts = [_split3(_log_sigmoid(g[h]) / C_GATE_TAU) for h in heads]
    b = [(jnp.dot(tri, parts[h][0], preferred_element_type=F32)
          + jnp.dot(tri, parts[h][1], preferred_element_type=F32)
          + jnp.dot(tri, parts[h][2], preferred_element_type=F32)) for h in heads]
    qf = [q_ref[rows, kcols[h]].astype(F32) * (C_DK ** -0.5) for h in heads]
    kf = [k_ref[rows, kcols[h]].astype(F32) for h in heads]
    vb = [v_ref[rows, vcols[h]] for h in heads]
    st = [st_scr[h] for h in heads]
    for h in heads:
        b_scr[h] = b[h]
        k_scr[h] = kf[h]
    o_inter = [lax.dot_general((qf[h] * jnp.exp(b[h])).astype(BF16), st[h].astype(BF16), nt,
                               preferred_element_type=F32) for h in heads]
    a = {}
    for h in heads:
        for blk in range(1, nblk):
            r0 = blk * sub
            ref_row = b_scr[h, r0 - 1:r0, :]
            kd = jnp.where(row_ck < r0, kf[h] * jnp.exp(jnp.minimum(ref_row - b[h], 0.0)), 0.0)
            qd = qf[h][r0:r0 + sub, :] * jnp.exp(b[h][r0:r0 + sub, :] - ref_row)
            a[h, blk] = lax.dot_general(qd.astype(BF16), kd.astype(BF16), nt, preferred_element_type=F32)
    kdec = [(kf[h] * jnp.exp(b_scr[h, ck - 1:ck, :] - b[h])).astype(BF16) for h in heads]
    upd = [lax.dot_general(vb[h], kdec[h], (((0,), (0,)), ((), ())), preferred_element_type=F32)
           for h in heads]
    for h in heads:
        st_scr[h] = st[h] * jnp.exp(b_scr[h, ck - 1:ck, :]) + upd[h]
    for h in heads:
        b2 = b[h] * LOG2E
        outs = []
        for blk in range(nblk):
            r0 = blk * sub
            qb = qf[h][r0:r0 + sub, :]
            bb = b2[r0:r0 + sub, :]
            scores = a[h, blk] if blk > 0 else jnp.zeros((sub, ck), F32)
            for j in range(sub):
                bj = b_scr[h, r0 + j:r0 + j + 1, :] * LOG2E
                kj = k_scr[h, r0 + j:r0 + j + 1, :]
                x = qb * kj * jnp.exp2(bb - bj)
                colv = jnp.sum(x, axis=-1, keepdims=True)
                scores = jnp.where(jnp.logical_and(col_ck == r0 + j, row_sub >= j), colv, scores)
            outs.append(o_inter[h][r0:r0 + sub, :]
                        + jnp.dot(scores.astype(BF16), vb[h], preferred_element_type=F32))
        o = jnp.concatenate(outs, axis=0)
        ms = jnp.sum(o * o, axis=-1, keepdims=True) * (1.0 / C_DV)
        on = o * lax.rsqrt(ms + EPS) * gn_ref[...]
        rr = r_ref[rows, vcols[h]].astype(F32)
        o_ref[rows, vcols[h]] = (on * (rr / (1.0 + jnp.exp(-rr)))).astype(o_ref.dtype)


def _gla_kernel(q_ref, k_ref, v_ref, r_ref, sm_ref, w2_ref, cb_ref, gn_ref, o_ref,
                st_scr, b_scr, k_scr):
    ck, sub = C_CHUNK, C_SUB

    @pl.when(pl.program_id(1) == 0)
    def _():
        st_scr[...] = jnp.zeros_like(st_scr)

    tri = _tril_ones(ck, BF16)
    row_sub = lax.broadcasted_iota(jnp.int32, (sub, 1), 0)
    row_ck = lax.broadcasted_iota(jnp.int32, (ck, 1), 0)
    col_ck = lax.broadcasted_iota(jnp.int32, (sub, ck), 1)

    def chunk(c, carry):
        rows = pl.ds(pl.multiple_of(c * ck, ck), ck)
        sm_bf = sm_ref[rows, :].astype(BF16)
        _gla_chunk(rows, q_ref, k_ref, v_ref, r_ref, sm_bf, w2_ref, cb_ref, gn_ref, o_ref,
                   st_scr, b_scr, k_scr, tri, row_sub, row_ck, col_ck)
        return carry

    lax.fori_loop(0, q_ref.shape[0] // ck, chunk, 0)


def gla_mixer(proj3, small3, w2p, cbp, gnp):
    bsz, s, _ = proj3.shape
    t = GLA_ROWS
    kw, vw = C_HEADS * C_DK_PAD, C_HEADS * C_DV_PAD

    def col(base, width):
        return pl.BlockSpec((None, t, width), lambda b, i, c=base // width: (b, i, c))

    return pl.pallas_call(
        _gla_kernel, grid=(bsz, s // t),
        in_specs=[col(COL_QC, kw), col(COL_KC, kw), col(COL_VC, vw), col(COL_RC, vw),
                  pl.BlockSpec((None, t, V7X_LANES), lambda b, i: (b, i, 0)),
                  pl.BlockSpec((C_HEADS, V7X_LANES, C_DK_PAD), lambda b, i: (0, 0, 0)),
                  pl.BlockSpec((C_HEADS, 1, C_DK_PAD), lambda b, i: (0, 0, 0)),
                  pl.BlockSpec((1, C_DV_PAD), lambda b, i: (0, 0))],
        out_specs=pl.BlockSpec((None, t, vw), lambda b, i: (b, i, 0)),
        out_shape=jax.ShapeDtypeStruct((bsz, s, vw), BF16),
        scratch_shapes=[pltpu.VMEM((C_HEADS, C_DV_PAD, C_DK_PAD), F32),
                        pltpu.VMEM((C_HEADS, C_CHUNK, C_DK_PAD), F32),
                        pltpu.VMEM((C_HEADS, C_CHUNK, C_DK_PAD), F32)],
        compiler_params=_cparams(("parallel", "arbitrary")), name="gla_mixer",
    )(proj3, proj3, proj3, proj3, small3, w2p, cbp, gnp)


def _out_proj_kernel(x_ref, a_ref, b_ref, c_ref, wa_ref, wb_ref, wc_ref, o_ref):
    acc = jnp.dot(a_ref[...], wa_ref[...], preferred_element_type=F32)
    acc += jnp.dot(b_ref[...], wb_ref[...], preferred_element_type=F32)
    acc += jnp.dot(c_ref[...], wc_ref[...], preferred_element_type=F32)
    o_ref[...] = x_ref[...] + acc


def out_proj(x2d, oa, ob, oc, wa, wb, wc, *, tm=512, tn=D_MODEL):
    m, n = x2d.shape

    def rows(a):
        return pl.BlockSpec((tm, a.shape[1]), lambda i, j: (i, 0))

    def cols(w):
        return pl.BlockSpec((w.shape[0], tn), lambda i, j: (0, j))

    xs = pl.BlockSpec((tm, tn), lambda i, j: (i, j))
    return pl.pallas_call(
        _out_proj_kernel, grid=(m // tm, n // tn),
        in_specs=[xs, rows(oa), rows(ob), rows(oc), cols(wa), cols(wb), cols(wc)],
        out_specs=xs, out_shape=jax.ShapeDtypeStruct((m, n), F32),
        compiler_params=_cparams(("parallel", "arbitrary")), name="out_proj",
    )(x2d, oa, ob, oc, wa, wb, wc)


CROSS_ROWS = 512


def _cross_kernel(x_ref, g_ref, wq_ref, k_ref, v_ref, wo_ref, o_ref):
    x = x_ref[...]
    h = _rms_rows(x, g_ref[...]).astype(BF16)
    q = jnp.dot(h, wq_ref[...], preferred_element_type=F32).astype(BF16)
    cols = [slice(CROSS_DH * hd, CROSS_DH * (hd + 1)) for hd in range(CROSS_HEADS)]
    scores = [lax.dot_general(q[:, cs], k_ref[:, cs], (((1,), (1,)), ((), ())),
                              preferred_element_type=F32) * (CROSS_DH ** -0.5) for cs in cols]
    heads = []
    for cs, logits in zip(cols, scores):
        m = jnp.max(logits, axis=-1, keepdims=True)
        p = jnp.exp(logits - m)
        ssum = jnp.sum(p, axis=-1, keepdims=True)
        pv = jnp.dot(p.astype(BF16), v_ref[:, cs], preferred_element_type=F32)
        heads.append((pv / ssum).astype(BF16))
    o = jnp.concatenate(heads, axis=1)
    o_ref[...] = x + jnp.dot(o, wo_ref[...], preferred_element_type=F32)


def cross_attention(x2d, gain, wq, kv, wo, *, seq):
    m, d = x2d.shape
    tm = CROSS_ROWS
    mem_len = kv.shape[1]
    per_batch = seq // tm
    return pl.pallas_call(
        _cross_kernel, grid=(m // tm,),
        in_specs=[pl.BlockSpec((tm, d), lambda i: (i, 0)),
                  pl.BlockSpec((1, d), lambda i: (0, 0)),
                  pl.BlockSpec((d, CROSS_WIDTH), lambda i: (0, 0)),
                  pl.BlockSpec((None, mem_len, CROSS_WIDTH), lambda i: (i // per_batch, 0, 0)),
                  pl.BlockSpec((None, mem_len, CROSS_WIDTH), lambda i: (i // per_batch, 0, 1)),
                  pl.BlockSpec((CROSS_WIDTH, d), lambda i: (0, 0))],
        out_specs=pl.BlockSpec((tm, d), lambda i: (i, 0)),
        out_shape=jax.ShapeDtypeStruct((m, d), F32),
        compiler_params=_cparams(("parallel",)), name="cross_attention",
    )(x2d, gain.reshape(1, d), wq, kv, kv, wo)


FFN_ROWS = 1024
FFN_CHUNK = FFN_ROWS
FFN_HALO = 16


def _ffn_kernel(x_ref, xh_ref, g_ref, wv_ref, wg_ref, cw_ref, cb_ref, wd_ref, gf_ref, o_ref,
                h_scr, hh_scr, *, final):
    j = pl.program_id(1)

    @pl.when(j == 0)
    def _():
        _norm_to_scratch(x_ref, g_ref, h_scr)
        hh_scr[...] = _rms_rows(xh_ref[...], g_ref[...]).astype(BF16)
        o_ref[...] = x_ref[...]

    rc = FFN_CHUNK
    nchunk = h_scr.shape[0] // rc
    prev = jnp.dot(hh_scr[...], wg_ref[...], preferred_element_type=F32)
    row = lax.broadcasted_iota(jnp.int32, (rc, wg_ref.shape[1]), 0)
    vals, gates = [], []
    for c in range(nchunk):
        h = h_scr[c * rc:(c + 1) * rc, :]
        gates.append(jnp.dot(h, wg_ref[...], preferred_element_type=F32))
        vals.append(jnp.dot(h, wv_ref[...], preferred_element_type=F32))
    for c in range(nchunk):
        rows = slice(c * rc, (c + 1) * rc)
        val, gate = vals[c], gates[c]
        prev1 = prev[prev.shape[0] - 1:, :]
        prev2 = prev[prev.shape[0] - 2:prev.shape[0] - 1, :]
        g1 = jnp.where(row == 0, prev1, pltpu.roll(gate, 1, axis=0))
        g2 = jnp.where(row == 0, prev2, jnp.where(row == 1, prev1, pltpu.roll(gate, 2, axis=0)))
        gc = cb_ref[...] + cw_ref[0:1, :] * g2
        gc = gc + cw_ref[1:2, :] * g1
        gc = gc + cw_ref[2:3, :] * gate
        act = (gc / (1.0 + jnp.exp(-gc)) * val).astype(BF16)
        o_ref[rows, :] += jnp.dot(act, wd_ref[...], preferred_element_type=F32)
        prev = gate

    if final:
        @pl.when(j == pl.num_programs(1) - 1)
        def _():
            def body(r, c):
                sl = pl.ds(pl.multiple_of(r * NORM_ROWS, NORM_ROWS), NORM_ROWS)
                o_ref[sl, :] = _rms_rows(o_ref[sl, :], gf_ref[...])
                return c
            lax.fori_loop(0, o_ref.shape[0] // NORM_ROWS, body, 0)


def conv_ffn(x2d, gain, wv, wg, cw, cb, wd, gain_final, *, seq, final):
    m, d = x2d.shape
    tm, tf = FFN_ROWS, FF_TILE
    nt = m // tm
    tail = x2d.reshape(nt, tm, d)[:, tm - FFN_HALO:, :]
    halo = jnp.concatenate([jnp.zeros_like(tail[:1]), tail[:-1]], axis=0)
    starts = (jnp.arange(nt) % (seq // tm) == 0)[:, None, None]
    halo = jnp.where(starts, 0.0, halo)
    ff = wv.shape[1]
    return pl.pallas_call(
        functools.partial(_ffn_kernel, final=final), grid=(nt, ff // tf),
        in_specs=[pl.BlockSpec((tm, d), lambda i, j: (i, 0)),
                  pl.BlockSpec((None, FFN_HALO, d), lambda i, j: (i, 0, 0)),
                  pl.BlockSpec((1, d), lambda i, j: (0, 0)),
                  pl.BlockSpec((d, tf), lambda i, j: (0, j)),
                  pl.BlockSpec((d, tf), lambda i, j: (0, j)),
                  pl.BlockSpec((CONV_W, tf), lambda i, j: (0, j)),
                  pl.BlockSpec((1, tf), lambda i, j: (0, j)),
                  pl.BlockSpec((tf, d), lambda i, j: (j, 0)),
                  pl.BlockSpec((1, d), lambda i, j: (0, 0))],
        out_specs=pl.BlockSpec((tm, d), lambda i, j: (i, 0)),
        out_shape=jax.ShapeDtypeStruct((m, d), F32),
        scratch_shapes=[pltpu.VMEM((tm, d), BF16), pltpu.VMEM((FFN_HALO, d), BF16)],
        compiler_params=_cparams(("parallel", "arbitrary")), name="conv_ffn",
    )(x2d, halo, gain.reshape(1, d), wv, wg, cw, cb.reshape(1, ff), wd, gain_final.reshape(1, d))


def _pad_heads(w, heads, width, padded):
    lead = w.shape[:-1]
    w = w.reshape(lead + (heads, width))
    w = jnp.pad(w, [(0, 0)] * len(lead) + [(0, 0), (0, padded - width)])
    return w.reshape(lead + (heads * padded,))


def _layer_params(l, w_in, c_gate_w2, c_gate_b, c_norm, w_out, w_up, conv_w, conv_b, w_down):
    splits = np.cumsum([A_WIDTH] * 3 + [B_WIDTH] * 3 + [B_HEADS, C_KW, C_KW, C_VW, C_VW, C_GATE_RANK])
    fl, qc, kc, vc, rc, gl = jnp.split(w_in[l][:, splits[5]:], (splits[6:11] - splits[5]).tolist(), axis=1)
    w_all = jnp.concatenate([
        _pad_heads(vc, C_HEADS, C_DV, C_DV_PAD), _pad_heads(rc, C_HEADS, C_DV, C_DV_PAD),
        _pad_heads(qc, C_HEADS, C_DK, C_DK_PAD), _pad_heads(kc, C_HEADS, C_DK, C_DK_PAD),
        w_in[l][:, :splits[5]]],
        axis=1).astype(BF16)
    w_small = jnp.zeros((D_MODEL, V7X_LANES), F32)
    w_small = w_small.at[:, SMALL_FL:SMALL_FL + B_HEADS].set(fl)
    w_small = w_small.at[:, SMALL_GL:SMALL_GL + C_GATE_RANK].set(gl).astype(BF16)
    w2 = _pad_heads(c_gate_w2[l], C_HEADS, C_DK, C_DK_PAD).reshape(C_GATE_RANK, C_HEADS, C_DK_PAD)
    w2p = jnp.zeros((C_HEADS, V7X_LANES, C_DK_PAD), F32)
    w2p = w2p.at[:, SMALL_GL:SMALL_GL + C_GATE_RANK, :].set(jnp.transpose(w2, (1, 0, 2))).astype(BF16)
    cbp = _pad_heads(c_gate_b[l], C_HEADS, C_DK, C_DK_PAD).reshape(C_HEADS, 1, C_DK_PAD)
    gnp = jnp.pad(c_norm[l], (0, C_DV_PAD - C_DV)).reshape(1, C_DV_PAD)
    wo = w_out[l]
    wa = wo[:A_WIDTH].astype(BF16)
    wb = wo[A_WIDTH:A_WIDTH + B_WIDTH].astype(BF16)
    wc = wo[A_WIDTH + B_WIDTH:].reshape(C_HEADS, C_DV, D_MODEL)
    wc = jnp.pad(wc, ((0, 0), (0, C_DV_PAD - C_DV), (0, 0))).reshape(C_HEADS * C_DV_PAD, D_MODEL).astype(BF16)
    fpad = D_FF_PAD - D_FF
    wv = jnp.pad(w_up[l][:, :D_FF], ((0, 0), (0, fpad))).astype(BF16)
    wg = jnp.pad(w_up[l][:, D_FF:], ((0, 0), (0, fpad))).astype(BF16)
    cw = jnp.pad(conv_w[l], ((0, 0), (0, fpad)))
    cb = jnp.pad(conv_b[l], (0, fpad))
    wd = jnp.pad(w_down[l], ((0, fpad), (0, 0))).astype(BF16)
    return dict(w_all=w_all, w_small=w_small, w2p=w2p, cbp=cbp, gnp=gnp,
                wa=wa, wb=wb, wc=wc, wv=wv, wg=wg, cw=cw, cb=cb, wd=wd)


def hybrid_mixer(x2d, gain, p, f_bias, bias_a, *, bsz, seq):
    proj, small = in_proj(x2d, gain, p["w_all"], p["w_small"])
    proj3 = proj.reshape(bsz, seq, PROJ_PAD)
    small3 = small.reshape(bsz, seq, V7X_LANES)
    o_a = dilated_mixture(proj3, bias_a).reshape(bsz * seq, A_WIDTH)
    o_b = fox_attention(*fox_prep(proj3, small3, f_bias)).reshape(bsz * seq, B_WIDTH)
    o_c = gla_mixer(proj3, small3, p["w2p"], p["cbp"], p["gnp"]).reshape(bsz * seq, C_HEADS * C_DV_PAD)
    return out_proj(x2d, o_a, o_b, o_c, p["wa"], p["wb"], p["wc"])


def kernel(x, mem, rel_table, mem_norm, norm_final, norm_mix, w_in, f_bias, c_gate_w2, c_gate_b,
           c_norm, w_out, norm_cross, w_cq, w_ckv, w_co, norm_ffn, w_up, conv_w, conv_b, w_down):
    bsz, seq, d = x.shape
    depth = w_in.shape[0]
    mem_len = mem.shape[1]
    assert d == D_MODEL and seq % (A_SPAN * A_PATTERNS[-1][1]) == 0 and seq % 1024 == 0
    bias_a = jnp.stack([_dilated_bias(rel_table, dil) for _, dil in A_PATTERNS], axis=0)
    x2d = x.reshape(bsz * seq, d)
    mem2d = mem.reshape(bsz * mem_len, d)
    for l in range(depth):
        p = _layer_params(l, w_in, c_gate_w2, c_gate_b, c_norm, w_out, w_up, conv_w, conv_b, w_down)
        x2d = hybrid_mixer(x2d, norm_mix[l], p, f_bias[l], bias_a, bsz=bsz, seq=seq)
        kv = norm_matmul(mem2d, mem_norm, w_ckv[l].astype(BF16), tm=min(512, bsz * mem_len), tn=512,
                         name="mem_kv").reshape(bsz, mem_len, 2 * CROSS_WIDTH)
        x2d = cross_attention(x2d, norm_cross[l], w_cq[l].astype(BF16), kv, w_co[l].astype(BF16), seq=seq)
        x2d = conv_ffn(x2d, norm_ffn[l], p["wv"], p["wg"], p["cw"], p["cb"], p["wd"], norm_final,
                       seq=seq, final=l == depth - 1)
    return x2d.reshape(bsz, seq, d)
```

```python
import functools
import math

import numpy as np
import jax
import jax.numpy as jnp
from jax import lax
from jax.experimental import pallas as pl
from jax.experimental.pallas import tpu as pltpu

F32 = jnp.float32
BF16 = jnp.bfloat16

V7X_LANES = 128
V7X_VMEM_LIMIT_BYTES = 56 * 1024 * 1024

D_MODEL = 2048
HEAD_DIM = 64
A_WIDTH = 3 * D_MODEL // 8
A_HEADS = A_WIDTH // HEAD_DIM
A_PATTERNS = ((128, 1), (512, 4), (2048, 16))
A_SPAN = 128
B_WIDTH = D_MODEL // 4
B_HEADS = B_WIDTH // HEAD_DIM
C_HEADS = 4
C_VW = 3 * D_MODEL // 8
C_DV = C_VW // C_HEADS
C_DK = C_DV // 2
C_KW = C_HEADS * C_DK
C_GATE_RANK = 16
C_GATE_TAU = 16.0
C_CHUNK = 64
C_SUB = 8
C_DK_PAD = 128
C_DV_PAD = 256
CROSS_HEADS = 4
CROSS_DH = 128
CROSS_WIDTH = CROSS_HEADS * CROSS_DH
D_FF = ((8 * D_MODEL // 3 + 127) // 128) * 128
FF_TILE = 512
D_FF_PAD = ((D_FF + FF_TILE - 1) // FF_TILE) * FF_TILE
CONV_W = 3
REL_BUCKETS = 32
REL_MAX_DIST = 2048
EPS = 1e-6
NEG = -1e30
LOG2E = 1.4426950408889634

COL_VC = 0
COL_RC = COL_VC + C_HEADS * C_DV_PAD
COL_QC = COL_RC + C_HEADS * C_DV_PAD
COL_KC = COL_QC + C_HEADS * C_DK_PAD
COL_QA = COL_KC + C_HEADS * C_DK_PAD
COL_KA = COL_QA + A_WIDTH
COL_VA = COL_KA + A_WIDTH
COL_QB = COL_VA + A_WIDTH
COL_KB = COL_QB + B_WIDTH
COL_VB = COL_KB + B_WIDTH
PROJ_PAD = COL_VB + B_WIDTH
SMALL_FL = 0
SMALL_GL = B_HEADS


def _cparams(sem):
    return pltpu.CompilerParams(dimension_semantics=sem,
                                vmem_limit_bytes=V7X_VMEM_LIMIT_BYTES)


def _rms_rows(x, g):
    ms = jnp.mean(x * x, axis=-1, keepdims=True)
    return x * lax.rsqrt(ms + EPS) * g


def _split3(x):
    hi = x.astype(BF16)
    r1 = x - hi.astype(F32)
    mid = r1.astype(BF16)
    lo = (r1 - mid.astype(F32)).astype(BF16)
    return hi, mid, lo


def _log_sigmoid(x):
    return jnp.minimum(x, 0.0) - jnp.log(1.0 + jnp.exp(-jnp.abs(x)))


def _tril_ones(n, dtype):
    r = lax.broadcasted_iota(jnp.int32, (n, n), 0)
    c = lax.broadcasted_iota(jnp.int32, (n, n), 1)
    return jnp.where(r >= c, 1.0, 0.0).astype(dtype)


NORM_ROWS = 256


def _norm_to_scratch(x_ref, g_ref, h_scr):
    def body(r, c):
        sl = pl.ds(pl.multiple_of(r * NORM_ROWS, NORM_ROWS), NORM_ROWS)
        h_scr[sl, :] = _rms_rows(x_ref[sl, :], g_ref[...]).astype(BF16)
        return c
    lax.fori_loop(0, x_ref.shape[0] // NORM_ROWS, body, 0)


def _norm_matmul_kernel(x_ref, g_ref, w_ref, o_ref, h_scr):
    @pl.when(pl.program_id(1) == 0)
    def _():
        _norm_to_scratch(x_ref, g_ref, h_scr)
    o_ref[...] = jnp.dot(h_scr[...], w_ref[...],
                         preferred_element_type=F32).astype(o_ref.dtype)


def norm_matmul(x2d, gain, w, *, tm, tn, name):
    m, k = x2d.shape
    n = w.shape[1]
    return pl.pallas_call(
        _norm_matmul_kernel, grid=(m // tm, n // tn),
        in_specs=[pl.BlockSpec((tm, k), lambda i, j: (i, 0)),
                  pl.BlockSpec((1, k), lambda i, j: (0, 0)),
                  pl.BlockSpec((k, tn), lambda i, j: (0, j))],
        out_specs=pl.BlockSpec((tm, tn), lambda i, j: (i, j)),
        out_shape=jax.ShapeDtypeStruct((m, n), BF16),
        scratch_shapes=[pltpu.VMEM((tm, k), BF16)],
        compiler_params=_cparams(("parallel", "arbitrary")), name=name)(x2d, gain.reshape(1, k), w)


IN_PROJ_ROWS = 1024
IN_PROJ_TILE = 3 * A_WIDTH


def _in_proj_kernel(x_ref, g_ref, w_ref, ws_ref, o_ref, os_ref, h_scr):
    @pl.when(pl.program_id(1) == 0)
    def _():
        _norm_to_scratch(x_ref, g_ref, h_scr)
        os_ref[...] = jnp.dot(h_scr[...], ws_ref[...], preferred_element_type=F32)

    o_ref[...] = jnp.dot(h_scr[...], w_ref[...], preferred_element_type=F32).astype(o_ref.dtype)


def in_proj(x2d, gain, w_all, w_small):
    m, k = x2d.shape
    tm, tn = IN_PROJ_ROWS, IN_PROJ_TILE
    ns = w_small.shape[1]
    return pl.pallas_call(
        _in_proj_kernel, grid=(m // tm, PROJ_PAD // tn),
        in_specs=[pl.BlockSpec((tm, k), lambda i, j: (i, 0)),
                  pl.BlockSpec((1, k), lambda i, j: (0, 0)),
                  pl.BlockSpec((k, tn), lambda i, j: (0, j)),
                  pl.BlockSpec((k, ns), lambda i, j: (0, 0))],
        out_specs=[pl.BlockSpec((tm, tn), lambda i, j: (i, j)),
                   pl.BlockSpec((tm, ns), lambda i, j: (i, 0))],
        out_shape=[jax.ShapeDtypeStruct((m, PROJ_PAD), BF16),
                   jax.ShapeDtypeStruct((m, ns), F32)],
        scratch_shapes=[pltpu.VMEM((tm, k), BF16)],
        compiler_params=_cparams(("parallel", "arbitrary")), name="in_proj",
    )(x2d, gain.reshape(1, k), w_all, w_small)


def _t5_bucket(dist):
    max_exact = REL_BUCKETS // 2
    d = np.maximum(dist, 1).astype(np.float32)
    large = max_exact + (np.log(d / max_exact) / math.log(REL_MAX_DIST / max_exact)
                         * (REL_BUCKETS - max_exact)).astype(np.int32)
    return np.where(dist < max_exact, dist, np.minimum(large, REL_BUCKETS - 1)).astype(np.int32)


def _dilated_bias(rel_table, dilation):
    span = A_SPAN
    steps = np.arange(2 * span, -span, -1)
    valid = (steps >= 0) & (steps <= span)
    per_step = rel_table[_t5_bucket(np.clip(steps, 0, span) * dilation)].astype(F32)
    per_step = jnp.where(valid[:, None], per_step * LOG2E, NEG).T
    later = jnp.stack([per_step[:, 2 * span - i - span: 4 * span - i - span] for i in range(span)],
                      axis=1)
    first = jnp.where((np.arange(2 * span) >= span)[None, None, :], later, NEG)
    return jnp.stack([first, later], axis=0)


DIL_TILE = A_SPAN * max(d for _, d in A_PATTERNS)
DIL_UNITS = DIL_TILE // A_SPAN
DIL_GROUP = 4
DIL_COMBINE_ROWS = 256


def _dilated_kernel(qb_ref, kp_ref, kc_ref, vp_ref, vc_ref, bias_ref, o_ref,
                    q_ref, kk_scr, vv_scr, m_scr, n_scr, s_scr):
    tile, span = DIL_TILE, A_SPAN
    first_tile = pl.program_id(1) == 0
    q_ref[...] = qb_ref[...].astype(F32)
    kk_scr[0:tile, :] = kp_ref[...].astype(F32)
    kk_scr[tile:2 * tile, :] = kc_ref[...].astype(F32)
    vv_scr[0:tile, :] = vp_ref[...].astype(F32)
    vv_scr[tile:2 * tile, :] = vc_ref[...].astype(F32)
    lane_q = lax.broadcasted_iota(jnp.int32, (span, V7X_LANES), 1)
    lane_k = lax.broadcasted_iota(jnp.int32, (2 * span, V7X_LANES), 1)
    low_q = lane_q < HEAD_DIM
    low_k = lane_k < HEAD_DIM
    scale = HEAD_DIM ** -0.5 * LOG2E
    qmask = (jnp.where(low_q, scale, 0.0), jnp.where(low_q, 0.0, scale))
    nt = (((1,), (1,)), ((), ()))

    for pat, (_, d) in enumerate(A_PATTERNS):
        for u0 in range(0, DIL_UNITS, DIL_GROUP):
            units = []
            for u in range(u0, u0 + DIL_GROUP):
                n, r = divmod(u, d)
                q0 = r + span * d * n
                k0 = tile + q0 - span * d
                if d == 1:
                    qrows, krows = pl.ds(q0, span), pl.ds(k0, 2 * span)
                else:
                    qrows, krows = pl.ds(q0, span, stride=d), pl.ds(k0, 2 * span, stride=d)
                qf = q_ref[qrows, :]
                kb = kk_scr[krows, :].astype(BF16)
                sel = jnp.where(first_tile, 0, 1) if n == 0 else 1
                logits = [lax.dot_general((qf * qmask[e]).astype(BF16), kb, nt,
                                          preferred_element_type=F32) + bias_ref[pat, sel, e]
                          for e in range(2)]
                units.append((qrows, krows, logits))
            results = []
            for qrows, krows, logits in units:
                vf = vv_scr[krows, :]
                pvs, ms = [], []
                for e in range(2):
                    m = jnp.max(logits[e], axis=-1, keepdims=True)
                    p = jnp.exp2(logits[e] - m)
                    ve = (jnp.where(low_k, vf, 1.0) if e == 0 else jnp.where(low_k, 1.0, vf)).astype(BF16)
                    pvs.append(jnp.dot(p.astype(BF16), ve, preferred_element_type=F32))
                    ms.append(m)
                results.append((qrows, pvs, ms))
            for qrows, pvs, ms in results:
                m_scr[pat, qrows, :] = jnp.where(low_q, ms[0], ms[1])
                n_scr[pat, qrows, :] = jnp.where(low_q, pvs[0], pvs[1])
                s_scr[pat, qrows, :] = pltpu.roll(jnp.where(low_q, pvs[1], pvs[0]), HEAD_DIM, axis=1)

    def combine(c, carry):
        rows = pl.ds(pl.multiple_of(c * DIL_COMBINE_ROWS, DIL_COMBINE_ROWS), DIL_COMBINE_ROWS)
        ms = [m_scr[pat, rows, :] for pat in range(len(A_PATTERNS))]
        top = functools.reduce(jnp.maximum, ms)
        ws = [jnp.exp2(m - top) for m in ms]
        num = sum(w * n_scr[pat, rows, :] for pat, w in enumerate(ws))
        den = sum(w * s_scr[pat, rows, :] for pat, w in enumerate(ws))
        o_ref[rows, :] = (num / den).astype(o_ref.dtype)
        return carry

    lax.fori_loop(0, tile // DIL_COMBINE_ROWS, combine, 0)


def dilated_mixture(proj3, bias):
    bsz, s, _ = proj3.shape
    pairs = A_HEADS // 2
    blk = (None, DIL_TILE, V7X_LANES)
    cq, ck, cv = COL_QA // V7X_LANES, COL_KA // V7X_LANES, COL_VA // V7X_LANES

    def cur(col):
        return pl.BlockSpec(blk, lambda b, t, g: (b, t, col + g))

    def prev(col):
        return pl.BlockSpec(blk, lambda b, t, g: (b, jnp.maximum(t - 1, 0), col + g))

    tile_f32 = pltpu.VMEM((DIL_TILE, V7X_LANES), F32)
    both_f32 = pltpu.VMEM((2 * DIL_TILE, V7X_LANES), F32)
    stats_f32 = pltpu.VMEM((len(A_PATTERNS), DIL_TILE, V7X_LANES), F32)
    return pl.pallas_call(
        _dilated_kernel, grid=(bsz, s // DIL_TILE, pairs),
        in_specs=[cur(cq), prev(ck), cur(ck), prev(cv), cur(cv),
                  pl.BlockSpec((len(A_PATTERNS), 2, 2, A_SPAN, 2 * A_SPAN),
                               lambda b, t, g: (0, 0, g, 0, 0))],
        out_specs=pl.BlockSpec(blk, lambda b, t, g: (b, t, g)),
        out_shape=jax.ShapeDtypeStruct((bsz, s, A_WIDTH), BF16),
        scratch_shapes=[tile_f32, both_f32, both_f32, stats_f32, stats_f32, stats_f32],
        compiler_params=_cparams(("parallel", "parallel", "arbitrary")), name="dilated_mixture",
    )(proj3, proj3, proj3, proj3, proj3, bias)


FOX_PREP_ROWS = 512
FOX_TQ = 512
FOX_TK = 512
FOX_HEADS_PER_STEP = 4


def _fox_prep_kernel(sm_ref, fb_ref, q0_ref, q1_ref, k0_ref, k1_ref, v0_ref, v1_ref,
                     qa_ref, ka_ref, va_ref, carry_scr):
    t = sm_ref.shape[0]

    @pl.when(pl.program_id(1) == 0)
    def _():
        carry_scr[...] = jnp.zeros_like(carry_scr)

    x = _log_sigmoid(sm_ref[...] + fb_ref[...])
    tri = _tril_ones(t, BF16)
    hi, mid, lo = _split3(x)
    c = (jnp.dot(tri, hi, preferred_element_type=F32)
         + jnp.dot(tri, mid, preferred_element_type=F32)
         + jnp.dot(tri, lo, preferred_element_type=F32)) + carry_scr[0:1, :]
    carry_scr[...] = jnp.broadcast_to(c[t - 1:t, :], carry_scr.shape)
    chi, cmid, clo = [p.astype(F32) for p in _split3(c * LOG2E)]

    lane = lax.broadcasted_iota(jnp.int32, (t, V7X_LANES), 1)
    low = lane < HEAD_DIM
    scale = HEAD_DIM ** -0.5 * LOG2E
    q_refs, k_refs, v_refs = (q0_ref, q1_ref), (k0_ref, k1_ref), (v0_ref, v1_ref)
    for h in range(B_HEADS):
        src, pair = divmod(h // 2, 2)
        cs = slice(V7X_LANES * pair, V7X_LANES * (pair + 1))
        qp = q_refs[src][:, cs].astype(F32)
        kp = k_refs[src][:, cs].astype(F32)
        vp = v_refs[src][:, cs].astype(F32)
        if h % 2 == 1:
            qp = pltpu.roll(qp, HEAD_DIM, axis=1)
            kp = pltpu.roll(kp, HEAD_DIM, axis=1)
        c1, c2, c3 = chi[:, h:h + 1], cmid[:, h:h + 1], clo[:, h:h + 1]
        one = jnp.ones((t, 1), F32)
        zero = jnp.zeros((t, 1), F32)
        q_aug, k_aug = zero, zero
        for idx, (qv, kv) in enumerate(((c1, one), (c2, one), (c3, one),
                                        (one, -c1), (one, -c2), (one, -c3))):
            hit = lane == HEAD_DIM + idx
            q_aug = jnp.where(hit, qv, q_aug)
            k_aug = jnp.where(hit, kv, k_aug)
        qa_ref[h] = jnp.where(low, qp * scale, q_aug).T.astype(BF16)
        ka_ref[h] = jnp.where(low, kp, k_aug).astype(BF16)
        v_aug = jnp.where(low, vp, 1.0) if h % 2 == 0 else jnp.where(low, 1.0, vp)
        va_ref[h, 0] = v_aug.T.astype(BF16)


def fox_prep(proj3, small3, f_bias):
    bsz, s, _ = proj3.shape
    t = FOX_PREP_ROWS
    half = B_WIDTH // 2
    fb = jnp.zeros((1, V7X_LANES), F32).at[0, SMALL_FL:SMALL_FL + B_HEADS].set(f_bias)

    def colspec(col):
        return pl.BlockSpec((None, t, half), lambda b, i, c=col // half: (b, i, c))

    assert t == FOX_TK
    k_spec = pl.BlockSpec((None, B_HEADS, t, V7X_LANES), lambda b, i: (b, 0, i, 0))
    k_shape = jax.ShapeDtypeStruct((bsz, B_HEADS, s, V7X_LANES), BF16)
    q_spec = pl.BlockSpec((None, B_HEADS, V7X_LANES, t), lambda b, i: (b, 0, 0, i))
    q_shape = jax.ShapeDtypeStruct((bsz, B_HEADS, V7X_LANES, s), BF16)
    v_spec = pl.BlockSpec((None, B_HEADS, 1, V7X_LANES, t), lambda b, i: (b, 0, i, 0, 0))
    v_shape = jax.ShapeDtypeStruct((bsz, B_HEADS, s // t, V7X_LANES, t), BF16)
    return pl.pallas_call(
        _fox_prep_kernel, grid=(bsz, s // t),
        in_specs=[pl.BlockSpec((None, t, V7X_LANES), lambda b, i: (b, i, 0)),
                  pl.BlockSpec((1, V7X_LANES), lambda b, i: (0, 0)),
                  colspec(COL_QB), colspec(COL_QB + half),
                  colspec(COL_KB), colspec(COL_KB + half),
                  colspec(COL_VB), colspec(COL_VB + half)],
        out_specs=[q_spec, k_spec, v_spec],
        out_shape=[q_shape, k_shape, v_shape],
        scratch_shapes=[pltpu.VMEM((8, V7X_LANES), F32)],
        compiler_params=_cparams(("parallel", "arbitrary")), name="fox_prep",
    )(small3, fb, proj3, proj3, proj3, proj3, proj3, proj3)


def _fox_kernel(q_ref, k_ref, v_ref, o_ref, m_scr, acc_scr):
    tq, tk = FOX_TQ, FOX_TK
    qi = pl.program_id(2)
    nfull = qi * (tq // tk)
    krow = lax.broadcasted_iota(jnp.int32, (tk, tq), 0)
    qcol = lax.broadcasted_iota(jnp.int32, (tk, tq), 1)
    frow = lax.broadcasted_iota(jnp.int32, (V7X_LANES, tq), 0)
    nh = q_ref.shape[0]
    qs = [q_ref[e] for e in range(nh)]
    m_scr[...] = jnp.full(m_scr.shape, NEG, F32)
    acc_scr[...] = jnp.zeros(acc_scr.shape, F32)

    def heads(kt, mask):
        rows = pl.ds(pl.multiple_of(kt * tk, tk), tk)
        scores = [jnp.dot(k_ref[e, rows, :], qs[e], preferred_element_type=F32) for e in range(nh)]
        for e in range(nh):
            m = m_scr[e, 0:1, :]
            s = scores[e] if mask is None else jnp.where(mask, scores[e], NEG)
            m_new = jnp.maximum(m, jnp.max(s, axis=0, keepdims=True))
            alpha = jnp.exp2(m - m_new)
            p = jnp.exp2(s - m_new)
            pv = jnp.dot(v_ref[e, kt], p.astype(BF16), preferred_element_type=F32)
            acc_scr[e] = alpha * acc_scr[e] + pv
            m_scr[e] = jnp.broadcast_to(m_new, (8, tq))

    def body(kt, c):
        heads(kt, None)
        return c

    lax.fori_loop(0, nfull, body, 0)
    for dd in range(tq // tk):
        heads(nfull + dd, krow + dd * tk <= qcol)
    for g in range(nh // 2):
        even, odd = acc_scr[2 * g], acc_scr[2 * g + 1]
        res = [acc / pltpu.roll(acc, HEAD_DIM, axis=0) for acc in (even, odd)]
        pair = jnp.where(frow < HEAD_DIM, res[0], res[1])
        o_ref[:, V7X_LANES * g:V7X_LANES * (g + 1)] = pair.T.astype(o_ref.dtype)


def fox_attention(q_aug, k_aug, v_aug):
    bsz, nh, s, _ = k_aug.shape
    tq, tk, hs = FOX_TQ, FOX_TK, FOX_HEADS_PER_STEP
    return pl.pallas_call(
        _fox_kernel, grid=(bsz, nh // hs, s // tq),
        in_specs=[pl.BlockSpec((None, hs, V7X_LANES, tq), lambda b, g, i: (b, g, 0, i)),
                  pl.BlockSpec((None, hs, s, V7X_LANES), lambda b, g, i: (b, g, 0, 0)),
                  pl.BlockSpec((None, hs, s // tk, V7X_LANES, tk), lambda b, g, i: (b, g, 0, 0, 0))],
        out_specs=pl.BlockSpec((None, tq, hs * HEAD_DIM), lambda b, g, i: (b, i, g)),
        out_shape=jax.ShapeDtypeStruct((bsz, s, B_WIDTH), BF16),
        scratch_shapes=[pltpu.VMEM((hs, 8, tq), F32), pltpu.VMEM((hs, V7X_LANES, tq), F32)],
        compiler_params=_cparams(("parallel", "parallel", "arbitrary")), name="fox_attention",
    )(q_aug, k_aug, v_aug)


GLA_ROWS = 512


def _gla_chunk(rows, q_ref, k_ref, v_ref, r_ref, sm_bf, w2_ref, cb_ref, gn_ref, o_ref,
               st_scr, b_scr, k_scr, tri, row_sub, row_ck, col_ck):
    ck, sub = C_CHUNK, C_SUB
    nblk = ck // sub
    heads = range(C_HEADS)
    nt = (((1,), (1,)), ((), ()))
    kcols = [slice(C_DK_PAD * h, C_DK_PAD * (h + 1)) for h in heads]
    vcols = [slice(C_DV_PAD * h, C_DV_PAD * (h + 1)) for h in heads]
    g = [jnp.dot(sm_bf, w2_ref[h], preferred_element_type=F32) + cb_ref[h] for h in heads]
    parts = [_split3(_log_sigmoid(g[h]) / C_GATE_TAU) for h in heads]
    b = [(jnp.dot(tri, parts[h][0], preferred_element_type=F32)
          + jnp.dot(tri, parts[h][1], preferred_element_type=F32)
          + jnp.dot(tri, parts[h][2], preferred_element_type=F32)) for h in heads]
    qf = [q_ref[rows, kcols[h]].astype(F32) * (C_DK ** -0.5) for h in heads]
    kf = [k_ref[rows, kcols[h]].astype(F32) for h in heads]
    vb = [v_ref[rows, vcols[h]] for h in heads]
    st = [st_scr[h] for h in heads]
    for h in heads:
        b_scr[h] = b[h]
        k_scr[h] = kf[h]
    o_inter = [lax.dot_general((qf[h] * jnp.exp(b[h])).astype(BF16), st[h].astype(BF16), nt,
                               preferred_element_type=F32) for h in heads]
    a = {}
    for h in heads:
        for blk in range(1, nblk):
            r0 = blk * sub
            ref_row = b_scr[h, r0 - 1:r0, :]
            kd = jnp.where(row_ck < r0, kf[h] * jnp.exp(jnp.minimum(ref_row - b[h], 0.0)), 0.0)
            qd = qf[h][r0:r0 + sub, :] * jnp.exp(b[h][r0:r0 + sub, :] - ref_row)
            a[h, blk] = lax.dot_general(qd.astype(BF16), kd.astype(BF16), nt, preferred_element_type=F32)
    kdec = [(kf[h] * jnp.exp(b_scr[h, ck - 1:ck, :] - b[h])).astype(BF16) for h in heads]
    upd = [lax.dot_general(vb[h], kdec[h], (((0,), (0,)), ((), ())), preferred_element_type=F32)
           for h in heads]
    for h in heads:
        st_scr[h] = st[h] * jnp.exp(b_scr[h, ck - 1:ck, :]) + upd[h]
    for h in heads:
        b2 = b[h] * LOG2E
        outs = []
        for blk in range(nblk):
            r0 = blk * sub
            qb = qf[h][r0:r0 + sub, :]
            bb = b2[r0:r0 + sub, :]
            scores = a[h, blk] if blk > 0 else jnp.zeros((sub, ck), F32)
            for j in range(sub):
                bj = b_scr[h, r0 + j:r0 + j + 1, :] * LOG2E
                kj = k_scr[h, r0 + j:r0 + j + 1, :]
                x = qb * kj * jnp.exp2(bb - bj)
                colv = jnp.sum(x, axis=-1, keepdims=True)
                scores = jnp.where(jnp.logical_and(col_ck == r0 + j, row_sub >= j), colv, scores)
            outs.append(o_inter[h][r0:r0 + sub, :]
                        + jnp.dot(scores.astype(BF16), vb[h], preferred_element_type=F32))
        o = jnp.concatenate(outs, axis=0)
        ms = jnp.sum(o * o, axis=-1, keepdims=True) * (1.0 / C_DV)
        on = o * lax.rsqrt(ms + EPS) * gn_ref[...]
        rr = r_ref[rows, vcols[h]].astype(F32)
        o_ref[rows, vcols[h]] = (on * (rr / (1.0 + jnp.exp(-rr)))).astype(o_ref.dtype)


def _gla_kernel(q_ref, k_ref, v_ref, r_ref, sm_ref, w2_ref, cb_ref, gn_ref, o_ref,
                st_scr, b_scr, k_scr):
    ck, sub = C_CHUNK, C_SUB

    @pl.when(pl.program_id(1) == 0)
    def _():
        st_scr[...] = jnp.zeros_like(st_scr)

    tri = _tril_ones(ck, BF16)
    row_sub = lax.broadcasted_iota(jnp.int32, (sub, 1), 0)
    row_ck = lax.broadcasted_iota(jnp.int32, (ck, 1), 0)
    col_ck = lax.broadcasted_iota(jnp.int32, (sub, ck), 1)

    def chunk(c, carry):
        rows = pl.ds(pl.multiple_of(c * ck, ck), ck)
        sm_bf = sm_ref[rows, :].astype(BF16)
        _gla_chunk(rows, q_ref, k_ref, v_ref, r_ref, sm_bf, w2_ref, cb_ref, gn_ref, o_ref,
                   st_scr, b_scr, k_scr, tri, row_sub, row_ck, col_ck)
        return carry

    lax.fori_loop(0, q_ref.shape[0] // ck, chunk, 0)


def gla_mixer(proj3, small3, w2p, cbp, gnp):
    bsz, s, _ = proj3.shape
    t = GLA_ROWS
    kw, vw = C_HEADS * C_DK_PAD, C_HEADS * C_DV_PAD

    def col(base, width):
        return pl.BlockSpec((None, t, width), lambda b, i, c=base // width: (b, i, c))

    return pl.pallas_call(
        _gla_kernel, grid=(bsz, s // t),
        in_specs=[col(COL_QC, kw), col(COL_KC, kw), col(COL_VC, vw), col(COL_RC, vw),
                  pl.BlockSpec((None, t, V7X_LANES), lambda b, i: (b, i, 0)),
                  pl.BlockSpec((C_HEADS, V7X_LANES, C_DK_PAD), lambda b, i: (0, 0, 0)),
                  pl.BlockSpec((C_HEADS, 1, C_DK_PAD), lambda b, i: (0, 0, 0)),
                  pl.BlockSpec((1, C_DV_PAD), lambda b, i: (0, 0))],
        out_specs=pl.BlockSpec((None, t, vw), lambda b, i: (b, i, 0)),
        out_shape=jax.ShapeDtypeStruct((bsz, s, vw), BF16),
        scratch_shapes=[pltpu.VMEM((C_HEADS, C_DV_PAD, C_DK_PAD), F32),
                        pltpu.VMEM((C_HEADS, C_CHUNK, C_DK_PAD), F32),
                        pltpu.VMEM((C_HEADS, C_CHUNK, C_DK_PAD), F32)],
        compiler_params=_cparams(("parallel", "arbitrary")), name="gla_mixer",
    )(proj3, proj3, proj3, proj3, small3, w2p, cbp, gnp)


def _out_proj_kernel(x_ref, a_ref, b_ref, c_ref, wa_ref, wb_ref, wc_ref, o_ref):
    acc = jnp.dot(a_ref[...], wa_ref[...], preferred_element_type=F32)
    acc += jnp.dot(b_ref[...], wb_ref[...], preferred_element_type=F32)
    acc += jnp.dot(c_ref[...], wc_ref[...], preferred_element_type=F32)
    o_ref[...] = x_ref[...] + acc


def out_proj(x2d, oa, ob, oc, wa, wb, wc, *, tm=512, tn=D_MODEL):
    m, n = x2d.shape

    def rows(a):
        return pl.BlockSpec((tm, a.shape[1]), lambda i, j: (i, 0))

    def cols(w):
        return pl.BlockSpec((w.shape[0], tn), lambda i, j: (0, j))

    xs = pl.BlockSpec((tm, tn), lambda i, j: (i, j))
    return pl.pallas_call(
        _out_proj_kernel, grid=(m // tm, n // tn),
        in_specs=[xs, rows(oa), rows(ob), rows(oc), cols(wa), cols(wb), cols(wc)],
        out_specs=xs, out_shape=jax.ShapeDtypeStruct((m, n), F32),
        compiler_params=_cparams(("parallel", "arbitrary")), name="out_proj",
    )(x2d, oa, ob, oc, wa, wb, wc)


CROSS_ROWS = 512


def _cross_kernel(x_ref, g_ref, wq_ref, k_ref, v_ref, wo_ref, o_ref):
    x = x_ref[...]
    h = _rms_rows(x, g_ref[...]).astype(BF16)
    q = jnp.dot(h, wq_ref[...], preferred_element_type=F32).astype(BF16)
    cols = [slice(CROSS_DH * hd, CROSS_DH * (hd + 1)) for hd in range(CROSS_HEADS)]
    scores = [lax.dot_general(q[:, cs], k_ref[:, cs], (((1,), (1,)), ((), ())),
                              preferred_element_type=F32) * (CROSS_DH ** -0.5) for cs in cols]
    heads = []
    for cs, logits in zip(cols, scores):
        m = jnp.max(logits, axis=-1, keepdims=True)
        p = jnp.exp(logits - m)
        ssum = jnp.sum(p, axis=-1, keepdims=True)
        pv = jnp.dot(p.astype(BF16), v_ref[:, cs], preferred_element_type=F32)
        heads.append((pv / ssum).astype(BF16))
    o = jnp.concatenate(heads, axis=1)
    o_ref[...] = x + jnp.dot(o, wo_ref[...], preferred_element_type=F32)


def cross_attention(x2d, gain, wq, kv, wo, *, seq):
    m, d = x2d.shape
    tm = CROSS_ROWS
    mem_len = kv.shape[1]
    per_batch = seq // tm
    return pl.pallas_call(
        _cross_kernel, grid=(m // tm,),
        in_specs=[pl.BlockSpec((tm, d), lambda i: (i, 0)),
                  pl.BlockSpec((1, d), lambda i: (0, 0)),
                  pl.BlockSpec((d, CROSS_WIDTH), lambda i: (0, 0)),
                  pl.BlockSpec((None, mem_len, CROSS_WIDTH), lambda i: (i // per_batch, 0, 0)),
                  pl.BlockSpec((None, mem_len, CROSS_WIDTH), lambda i: (i // per_batch, 0, 1)),
                  pl.BlockSpec((CROSS_WIDTH, d), lambda i: (0, 0))],
        out_specs=pl.BlockSpec((tm, d), lambda i: (i, 0)),
        out_shape=jax.ShapeDtypeStruct((m, d), F32),
        compiler_params=_cparams(("parallel",)), name="cross_attention",
    )(x2d, gain.reshape(1, d), wq, kv, kv, wo)


FFN_ROWS = 1024
FFN_CHUNK = FFN_ROWS
FFN_HALO = 16


def _ffn_kernel(x_ref, xh_ref, g_ref, wv_ref, wg_ref, cw_ref, cb_ref, wd_ref, gf_ref, o_ref,
                h_scr, hh_scr, *, final):
    j = pl.program_id(1)

    @pl.when(j == 0)
    def _():
        _norm_to_scratch(x_ref, g_ref, h_scr)
        hh_scr[...] = _rms_rows(xh_ref[...], g_ref[...]).astype(BF16)
        o_ref[...] = x_ref[...]

    rc = FFN_CHUNK
    nchunk = h_scr.shape[0] // rc
    prev = jnp.dot(hh_scr[...], wg_ref[...], preferred_element_type=F32)
    row = lax.broadcasted_iota(jnp.int32, (rc, wg_ref.shape[1]), 0)
    vals, gates = [], []
    for c in range(nchunk):
        h = h_scr[c * rc:(c + 1) * rc, :]
        gates.append(jnp.dot(h, wg_ref[...], preferred_element_type=F32))
        vals.append(jnp.dot(h, wv_ref[...], preferred_element_type=F32))
    for c in range(nchunk):
        rows = slice(c * rc, (c + 1) * rc)
        val, gate = vals[c], gates[c]
        prev1 = prev[prev.shape[0] - 1:, :]
        prev2 = prev[prev.shape[0] - 2:prev.shape[0] - 1, :]
        g1 = jnp.where(row == 0, prev1, pltpu.roll(gate, 1, axis=0))
        g2 = jnp.where(row == 0, prev2, jnp.where(row == 1, prev1, pltpu.roll(gate, 2, axis=0)))
        gc = cb_ref[...] + cw_ref[0:1, :] * g2
        gc = gc + cw_ref[1:2, :] * g1
        gc = gc + cw_ref[2:3, :] * gate
        act = (gc / (1.0 + jnp.exp(-gc)) * val).astype(BF16)
        o_ref[rows, :] += jnp.dot(act, wd_ref[...], preferred_element_type=F32)
        prev = gate

    if final:
        @pl.when(j == pl.num_programs(1) - 1)
        def _():
            def body(r, c):
                sl = pl.ds(pl.multiple_of(r * NORM_ROWS, NORM_ROWS), NORM_ROWS)
                o_ref[sl, :] = _rms_rows(o_ref[sl, :], gf_ref[...])
                return c
            lax.fori_loop(0, o_ref.shape[0] // NORM_ROWS, body, 0)


def conv_ffn(x2d, gain, wv, wg, cw, cb, wd, gain_final, *, seq, final):
    m, d = x2d.shape
    tm, tf = FFN_ROWS, FF_TILE
    nt = m // tm
    tail = x2d.reshape(nt, tm, d)[:, tm - FFN_HALO:, :]
    halo = jnp.concatenate([jnp.zeros_like(tail[:1]), tail[:-1]], axis=0)
    starts = (jnp.arange(nt) % (seq // tm) == 0)[:, None, None]
    halo = jnp.where(starts, 0.0, halo)
    ff = wv.shape[1]
    return pl.pallas_call(
        functools.partial(_ffn_kernel, final=final), grid=(nt, ff // tf),
        in_specs=[pl.BlockSpec((tm, d), lambda i, j: (i, 0)),
                  pl.BlockSpec((None, FFN_HALO, d), lambda i, j: (i, 0, 0)),
                  pl.BlockSpec((1, d), lambda i, j: (0, 0)),
                  pl.BlockSpec((d, tf), lambda i, j: (0, j)),
                  pl.BlockSpec((d, tf), lambda i, j: (0, j)),
                  pl.BlockSpec((CONV_W, tf), lambda i, j: (0, j)),
                  pl.BlockSpec((1, tf), lambda i, j: (0, j)),
                  pl.BlockSpec((tf, d), lambda i, j: (j, 0)),
                  pl.BlockSpec((1, d), lambda i, j: (0, 0))],
        out_specs=pl.BlockSpec((tm, d), lambda i, j: (i, 0)),
        out_shape=jax.ShapeDtypeStruct((m, d), F32),
        scratch_shapes=[pltpu.VMEM((tm, d), BF16), pltpu.VMEM((FFN_HALO, d), BF16)],
        compiler_params=_cparams(("parallel", "arbitrary")), name="conv_ffn",
    )(x2d, halo, gain.reshape(1, d), wv, wg, cw, cb.reshape(1, ff), wd, gain_final.reshape(1, d))


def _pad_heads(w, heads, width, padded):
    lead = w.shape[:-1]
    w = w.reshape(lead + (heads, width))
    w = jnp.pad(w, [(0, 0)] * len(lead) + [(0, 0), (0, padded - width)])
    return w.reshape(lead + (heads * padded,))


def _layer_params(l, w_in, c_gate_w2, c_gate_b, c_norm, w_out, w_up, conv_w, conv_b, w_down):
    splits = np.cumsum([A_WIDTH] * 3 + [B_WIDTH] * 3 + [B_HEADS, C_KW, C_KW, C_VW, C_VW, C_GATE_RANK])
    w_in_l = w_in[l].astype(BF16)
    fl, qc, kc, vc, rc, gl = jnp.split(w_in_l[:, splits[5]:], (splits[6:11] - splits[5]).tolist(), axis=1)
    w_all = jnp.concatenate([
        _pad_heads(vc, C_HEADS, C_DV, C_DV_PAD), _pad_heads(rc, C_HEADS, C_DV, C_DV_PAD),
        _pad_heads(qc, C_HEADS, C_DK, C_DK_PAD), _pad_heads(kc, C_HEADS, C_DK, C_DK_PAD),
        w_in_l[:, :splits[5]]],
        axis=1)
    w_small = jnp.zeros((D_MODEL, V7X_LANES), BF16)
    w_small = w_small.at[:, SMALL_FL:SMALL_FL + B_HEADS].set(fl)
    w_small = w_small.at[:, SMALL_GL:SMALL_GL + C_GATE_RANK].set(gl)
    w2 = _pad_heads(c_gate_w2[l], C_HEADS, C_DK, C_DK_PAD).reshape(C_GATE_RANK, C_HEADS, C_DK_PAD)
    w2p = jnp.zeros((C_HEADS, V7X_LANES, C_DK_PAD), F32)
    w2p = w2p.at[:, SMALL_GL:SMALL_GL + C_GATE_RANK, :].set(jnp.transpose(w2, (1, 0, 2))).astype(BF16)
    cbp = _pad_heads(c_gate_b[l], C_HEADS, C_DK, C_DK_PAD).reshape(C_HEADS, 1, C_DK_PAD)
    gnp = jnp.pad(c_norm[l], (0, C_DV_PAD - C_DV)).reshape(1, C_DV_PAD)
    wo = w_out[l].astype(BF16)
    wa = wo[:A_WIDTH]
    wb = wo[A_WIDTH:A_WIDTH + B_WIDTH]
    wc = wo[A_WIDTH + B_WIDTH:].reshape(C_HEADS, C_DV, D_MODEL)
    wc = jnp.pad(wc, ((0, 0), (0, C_DV_PAD - C_DV), (0, 0))).reshape(C_HEADS * C_DV_PAD, D_MODEL)
    fpad = D_FF_PAD - D_FF
    w_up_l = w_up[l].astype(BF16)
    wv = jnp.pad(w_up_l[:, :D_FF], ((0, 0), (0, fpad)))
    wg = jnp.pad(w_up_l[:, D_FF:], ((0, 0), (0, fpad)))
    cw = jnp.pad(conv_w[l], ((0, 0), (0, fpad)))
    cb = jnp.pad(conv_b[l], (0, fpad))
    wd = jnp.pad(w_down[l].astype(BF16), ((0, fpad), (0, 0)))
    return dict(w_all=w_all, w_small=w_small, w2p=w2p, cbp=cbp, gnp=gnp,
                wa=wa, wb=wb, wc=wc, wv=wv, wg=wg, cw=cw, cb=cb, wd=wd)


def hybrid_mixer(x2d, gain, p, f_bias, bias_a, *, bsz, seq):
    proj, small = in_proj(x2d, gain, p["w_all"], p["w_small"])
    proj3 = proj.reshape(bsz, seq, PROJ_PAD)
    small3 = small.reshape(bsz, seq, V7X_LANES)
    o_a = dilated_mixture(proj3, bias_a).reshape(bsz * seq, A_WIDTH)
    o_b = fox_attention(*fox_prep(proj3, small3, f_bias)).reshape(bsz * seq, B_WIDTH)
    o_c = gla_mixer(proj3, small3, p["w2p"], p["cbp"], p["gnp"]).reshape(bsz * seq, C_HEADS * C_DV_PAD)
    return out_proj(x2d, o_a, o_b, o_c, p["wa"], p["wb"], p["wc"])


def kernel(x, mem, rel_table, mem_norm, norm_final, norm_mix, w_in, f_bias, c_gate_w2, c_gate_b,
           c_norm, w_out, norm_cross, w_cq, w_ckv, w_co, norm_ffn, w_up, conv_w, conv_b, w_down):
    bsz, seq, d = x.shape
    depth = w_in.shape[0]
    mem_len = mem.shape[1]
    assert d == D_MODEL and seq % (A_SPAN * A_PATTERNS[-1][1]) == 0 and seq % 1024 == 0
    bias_a = jnp.stack([_dilated_bias(rel_table, dil) for _, dil in A_PATTERNS], axis=0)
    x2d = x.reshape(bsz * seq, d)
    mem2d = mem.reshape(bsz * mem_len, d)
    for l in range(depth):
        p = _layer_params(l, w_in, c_gate_w2, c_gate_b, c_norm, w_out, w_up, conv_w, conv_b, w_down)
        x2d = hybrid_mixer(x2d, norm_mix[l], p, f_bias[l], bias_a, bsz=bsz, seq=seq)
        kv = norm_matmul(mem2d, mem_norm, w_ckv[l].astype(BF16), tm=min(512, bsz * mem_len), tn=512,
                         name="mem_kv").reshape(bsz, mem_len, 2 * CROSS_WIDTH)
        x2d = cross_attention(x2d, norm_cross[l], w_cq[l].astype(BF16), kv, w_co[l].astype(BF16), seq=seq)
        x2d = conv_ffn(x2d, norm_ffn[l], p["wv"], p["wg"], p["cw"], p["cb"], p["wd"], norm_final,
                       seq=seq, final=l == depth - 1)
    return x2d.reshape(bsz, seq, d)
```

```python
import functools
import math

import numpy as np
import jax
import jax.numpy as jnp
from jax import lax
from jax.experimental import pallas as pl
from jax.experimental.pallas import tpu as pltpu

F32 = jnp.float32
BF16 = jnp.bfloat16

V7X_LANES = 128
V7X_VMEM_LIMIT_BYTES = 56 * 1024 * 1024

D_MODEL = 2048
HEAD_DIM = 64
A_WIDTH = 3 * D_MODEL // 8
A_HEADS = A_WIDTH // HEAD_DIM
A_PATTERNS = ((128, 1), (512, 4), (2048, 16))
A_SPAN = 128
B_WIDTH = D_MODEL // 4
B_HEADS = B_WIDTH // HEAD_DIM
C_HEADS = 4
C_VW = 3 * D_MODEL // 8
C_DV = C_VW // C_HEADS
C_DK = C_DV // 2
C_KW = C_HEADS * C_DK
C_GATE_RANK = 16
C_GATE_TAU = 16.0
C_CHUNK = 64
C_SUB = 8
C_DK_PAD = 128
C_DV_PAD = 256
CROSS_HEADS = 4
CROSS_DH = 128
CROSS_WIDTH = CROSS_HEADS * CROSS_DH
D_FF = ((8 * D_MODEL // 3 + 127) // 128) * 128
FF_TILE = 512
D_FF_PAD = ((D_FF + FF_TILE - 1) // FF_TILE) * FF_TILE
CONV_W = 3
REL_BUCKETS = 32
REL_MAX_DIST = 2048
EPS = 1e-6
NEG = -1e30
LOG2E = 1.4426950408889634

COL_VC = 0
COL_RC = COL_VC + C_HEADS * C_DV_PAD
COL_QC = COL_RC + C_HEADS * C_DV_PAD
COL_KC = COL_QC + C_HEADS * C_DK_PAD
COL_QA = COL_KC + C_HEADS * C_DK_PAD
COL_KA = COL_QA + A_WIDTH
COL_VA = COL_KA + A_WIDTH
COL_QB = COL_VA + A_WIDTH
COL_KB = COL_QB + B_WIDTH
COL_VB = COL_KB + B_WIDTH
PROJ_PAD = COL_VB + B_WIDTH
SMALL_FL = 0
SMALL_GL = B_HEADS


def _cparams(sem):
    return pltpu.CompilerParams(dimension_semantics=sem,
                                vmem_limit_bytes=V7X_VMEM_LIMIT_BYTES)


def _rms_rows(x, g):
    ms = jnp.mean(x * x, axis=-1, keepdims=True)
    return x * lax.rsqrt(ms + EPS) * g


def _split3(x):
    hi = x.astype(BF16)
    r1 = x - hi.astype(F32)
    mid = r1.astype(BF16)
    lo = (r1 - mid.astype(F32)).astype(BF16)
    return hi, mid, lo


def _log_sigmoid(x):
    return jnp.minimum(x, 0.0) - jnp.log(1.0 + jnp.exp(-jnp.abs(x)))


def _tril_ones(n, dtype):
    r = lax.broadcasted_iota(jnp.int32, (n, n), 0)
    c = lax.broadcasted_iota(jnp.int32, (n, n), 1)
    return jnp.where(r >= c, 1.0, 0.0).astype(dtype)


NORM_ROWS = 256


def _norm_to_scratch(x_ref, g_ref, h_scr):
    def body(r, c):
        sl = pl.ds(pl.multiple_of(r * NORM_ROWS, NORM_ROWS), NORM_ROWS)
        h_scr[sl, :] = _rms_rows(x_ref[sl, :], g_ref[...]).astype(BF16)
        return c
    lax.fori_loop(0, x_ref.shape[0] // NORM_ROWS, body, 0)


def _norm_matmul_kernel(x_ref, g_ref, w_ref, o_ref, h_scr):
    @pl.when(pl.program_id(1) == 0)
    def _():
        _norm_to_scratch(x_ref, g_ref, h_scr)
    o_ref[...] = jnp.dot(h_scr[...], w_ref[...],
                         preferred_element_type=F32).astype(o_ref.dtype)


def norm_matmul(x2d, gain, w, *, tm, tn, name):
    m, k = x2d.shape
    n = w.shape[1]
    return pl.pallas_call(
        _norm_matmul_kernel, grid=(m // tm, n // tn),
        in_specs=[pl.BlockSpec((tm, k), lambda i, j: (i, 0)),
                  pl.BlockSpec((1, k), lambda i, j: (0, 0)),
                  pl.BlockSpec((k, tn), lambda i, j: (0, j))],
        out_specs=pl.BlockSpec((tm, tn), lambda i, j: (i, j)),
        out_shape=jax.ShapeDtypeStruct((m, n), BF16),
        scratch_shapes=[pltpu.VMEM((tm, k), BF16)],
        compiler_params=_cparams(("parallel", "arbitrary")), name=name)(x2d, gain.reshape(1, k), w)


IN_PROJ_ROWS = 1024
IN_PROJ_TILE = 3 * A_WIDTH


def _in_proj_kernel(x_ref, g_ref, w_ref, ws_ref, o_ref, os_ref, h_scr):
    @pl.when(pl.program_id(1) == 0)
    def _():
        _norm_to_scratch(x_ref, g_ref, h_scr)
        os_ref[...] = jnp.dot(h_scr[...], ws_ref[...], preferred_element_type=F32)

    o_ref[...] = jnp.dot(h_scr[...], w_ref[...], preferred_element_type=F32).astype(o_ref.dtype)


def in_proj(x2d, gain, w_all, w_small):
    m, k = x2d.shape
    tm, tn = IN_PROJ_ROWS, IN_PROJ_TILE
    ns = w_small.shape[1]
    return pl.pallas_call(
        _in_proj_kernel, grid=(m // tm, PROJ_PAD // tn),
        in_specs=[pl.BlockSpec((tm, k), lambda i, j: (i, 0)),
                  pl.BlockSpec((1, k), lambda i, j: (0, 0)),
                  pl.BlockSpec((k, tn), lambda i, j: (0, j)),
                  pl.BlockSpec((k, ns), lambda i, j: (0, 0))],
        out_specs=[pl.BlockSpec((tm, tn), lambda i, j: (i, j)),
                   pl.BlockSpec((tm, ns), lambda i, j: (i, 0))],
        out_shape=[jax.ShapeDtypeStruct((m, PROJ_PAD), BF16),
                   jax.ShapeDtypeStruct((m, ns), F32)],
        scratch_shapes=[pltpu.VMEM((tm, k), BF16)],
        compiler_params=_cparams(("parallel", "arbitrary")), name="in_proj",
    )(x2d, gain.reshape(1, k), w_all, w_small)


def _t5_bucket(dist):
    max_exact = REL_BUCKETS // 2
    d = np.maximum(dist, 1).astype(np.float32)
    large = max_exact + (np.log(d / max_exact) / math.log(REL_MAX_DIST / max_exact)
                         * (REL_BUCKETS - max_exact)).astype(np.int32)
    return np.where(dist < max_exact, dist, np.minimum(large, REL_BUCKETS - 1)).astype(np.int32)


def _dilated_bias(rel_table, dilation):
    span = A_SPAN
    steps = np.arange(2 * span, -span, -1)
    valid = (steps >= 0) & (steps <= span)
    per_step = rel_table[_t5_bucket(np.clip(steps, 0, span) * dilation)].astype(F32)
    per_step = jnp.where(valid[:, None], per_step * LOG2E, NEG).T
    later = jnp.stack([per_step[:, 2 * span - i - span: 4 * span - i - span] for i in range(span)],
                      axis=1)
    first = jnp.where((np.arange(2 * span) >= span)[None, None, :], later, NEG)
    return jnp.stack([first, later], axis=0)


DIL_TILE = A_SPAN * max(d for _, d in A_PATTERNS)
DIL_UNITS = DIL_TILE // A_SPAN
DIL_GROUP = 4
DIL_COMBINE_ROWS = 256


def _dilated_kernel(qb_ref, kp_ref, kc_ref, vp_ref, vc_ref, bias_ref, o_ref,
                    q_ref, kk_scr, vv_scr, m_scr, n_scr, s_scr):
    tile, span = DIL_TILE, A_SPAN
    first_tile = pl.program_id(1) == 0
    q_ref[...] = qb_ref[...].astype(F32)
    kk_scr[0:tile, :] = kp_ref[...].astype(F32)
    kk_scr[tile:2 * tile, :] = kc_ref[...].astype(F32)
    vv_scr[0:tile, :] = vp_ref[...].astype(F32)
    vv_scr[tile:2 * tile, :] = vc_ref[...].astype(F32)
    lane_q = lax.broadcasted_iota(jnp.int32, (span, V7X_LANES), 1)
    lane_k = lax.broadcasted_iota(jnp.int32, (2 * span, V7X_LANES), 1)
    low_q = lane_q < HEAD_DIM
    low_k = lane_k < HEAD_DIM
    scale = HEAD_DIM ** -0.5 * LOG2E
    qmask = (jnp.where(low_q, scale, 0.0), jnp.where(low_q, 0.0, scale))
    nt = (((1,), (1,)), ((), ()))

    for pat, (_, d) in enumerate(A_PATTERNS):
        for u0 in range(0, DIL_UNITS, DIL_GROUP):
            units = []
            for u in range(u0, u0 + DIL_GROUP):
                n, r = divmod(u, d)
                q0 = r + span * d * n
                k0 = tile + q0 - span * d
                if d == 1:
                    qrows, krows = pl.ds(q0, span), pl.ds(k0, 2 * span)
                else:
                    qrows, krows = pl.ds(q0, span, stride=d), pl.ds(k0, 2 * span, stride=d)
                qf = q_ref[qrows, :]
                kb = kk_scr[krows, :].astype(BF16)
                sel = jnp.where(first_tile, 0, 1) if n == 0 else 1
                logits = [lax.dot_general((qf * qmask[e]).astype(BF16), kb, nt,
                                          preferred_element_type=F32) + bias_ref[pat, sel, e]
                          for e in range(2)]
                units.append((qrows, krows, logits))
            results = []
            for qrows, krows, logits in units:
                vf = vv_scr[krows, :]
                pvs, ms = [], []
                for e in range(2):
                    m = jnp.max(logits[e], axis=-1, keepdims=True)
                    p = jnp.exp2(logits[e] - m)
                    ve = (jnp.where(low_k, vf, 1.0) if e == 0 else jnp.where(low_k, 1.0, vf)).astype(BF16)
                    pvs.append(jnp.dot(p.astype(BF16), ve, preferred_element_type=F32))
                    ms.append(m)
                results.append((qrows, pvs, ms))
            for qrows, pvs, ms in results:
                m_scr[pat, qrows, :] = jnp.where(low_q, ms[0], ms[1])
                n_scr[pat, qrows, :] = jnp.where(low_q, pvs[0], pvs[1])
                s_scr[pat, qrows, :] = pltpu.roll(jnp.where(low_q, pvs[1], pvs[0]), HEAD_DIM, axis=1)

    def combine(c, carry):
        rows = pl.ds(pl.multiple_of(c * DIL_COMBINE_ROWS, DIL_COMBINE_ROWS), DIL_COMBINE_ROWS)
        ms = [m_scr[pat, rows, :] for pat in range(len(A_PATTERNS))]
        top = functools.reduce(jnp.maximum, ms)
        ws = [jnp.exp2(m - top) for m in ms]
        num = sum(w * n_scr[pat, rows, :] for pat, w in enumerate(ws))
        den = sum(w * s_scr[pat, rows, :] for pat, w in enumerate(ws))
        o_ref[rows, :] = (num / den).astype(o_ref.dtype)
        return carry

    lax.fori_loop(0, tile // DIL_COMBINE_ROWS, combine, 0)


def dilated_mixture(proj3, bias):
    bsz, s, _ = proj3.shape
    pairs = A_HEADS // 2
    blk = (None, DIL_TILE, V7X_LANES)
    cq, ck, cv = COL_QA // V7X_LANES, COL_KA // V7X_LANES, COL_VA // V7X_LANES

    def cur(col):
        return pl.BlockSpec(blk, lambda b, t, g: (b, t, col + g))

    def prev(col):
        return pl.BlockSpec(blk, lambda b, t, g: (b, jnp.maximum(t - 1, 0), col + g))

    tile_f32 = pltpu.VMEM((DIL_TILE, V7X_LANES), F32)
    both_f32 = pltpu.VMEM((2 * DIL_TILE, V7X_LANES), F32)
    stats_f32 = pltpu.VMEM((len(A_PATTERNS), DIL_TILE, V7X_LANES), F32)
    return pl.pallas_call(
        _dilated_kernel, grid=(bsz, s // DIL_TILE, pairs),
        in_specs=[cur(cq), prev(ck), cur(ck), prev(cv), cur(cv),
                  pl.BlockSpec((len(A_PATTERNS), 2, 2, A_SPAN, 2 * A_SPAN),
                               lambda b, t, g: (0, 0, g, 0, 0))],
        out_specs=pl.BlockSpec(blk, lambda b, t, g: (b, t, g)),
        out_shape=jax.ShapeDtypeStruct((bsz, s, A_WIDTH), BF16),
        scratch_shapes=[tile_f32, both_f32, both_f32, stats_f32, stats_f32, stats_f32],
        compiler_params=_cparams(("parallel", "parallel", "arbitrary")), name="dilated_mixture",
    )(proj3, proj3, proj3, proj3, proj3, bias)


FOX_PREP_ROWS = 512
FOX_TQ = 512
FOX_TK = 512
FOX_HEADS_PER_STEP = 4


def _fox_prep_kernel(sm_ref, fb_ref, q0_ref, q1_ref, k0_ref, k1_ref, v0_ref, v1_ref,
                     qa_ref, ka_ref, va_ref, carry_scr):
    t = sm_ref.shape[0]

    @pl.when(pl.program_id(1) == 0)
    def _():
        carry_scr[...] = jnp.zeros_like(carry_scr)

    x = _log_sigmoid(sm_ref[...] + fb_ref[...])
    tri = _tril_ones(t, BF16)
    hi, mid, lo = _split3(x)
    c = (jnp.dot(tri, hi, preferred_element_type=F32)
         + jnp.dot(tri, mid, preferred_element_type=F32)
         + jnp.dot(tri, lo, preferred_element_type=F32)) + carry_scr[0:1, :]
    carry_scr[...] = jnp.broadcast_to(c[t - 1:t, :], carry_scr.shape)
    chi, cmid, clo = [p.astype(F32) for p in _split3(c * LOG2E)]

    lane = lax.broadcasted_iota(jnp.int32, (t, V7X_LANES), 1)
    low = lane < HEAD_DIM
    scale = HEAD_DIM ** -0.5 * LOG2E
    q_refs, k_refs, v_refs = (q0_ref, q1_ref), (k0_ref, k1_ref), (v0_ref, v1_ref)
    for h in range(B_HEADS):
        src, pair = divmod(h // 2, 2)
        cs = slice(V7X_LANES * pair, V7X_LANES * (pair + 1))
        qp = q_refs[src][:, cs].astype(F32)
        kp = k_refs[src][:, cs].astype(F32)
        vp = v_refs[src][:, cs].astype(F32)
        if h % 2 == 1:
            qp = pltpu.roll(qp, HEAD_DIM, axis=1)
            kp = pltpu.roll(kp, HEAD_DIM, axis=1)
        c1, c2, c3 = chi[:, h:h + 1], cmid[:, h:h + 1], clo[:, h:h + 1]
        one = jnp.ones((t, 1), F32)
        zero = jnp.zeros((t, 1), F32)
        q_aug, k_aug = zero, zero
        for idx, (qv, kv) in enumerate(((c1, one), (c2, one), (c3, one),
                                        (one, -c1), (one, -c2), (one, -c3))):
            hit = lane == HEAD_DIM + idx
            q_aug = jnp.where(hit, qv, q_aug)
            k_aug = jnp.where(hit, kv, k_aug)
        qa_ref[h] = jnp.where(low, qp * scale, q_aug).T.astype(BF16)
        ka_ref[h] = jnp.where(low, kp, k_aug).astype(BF16)
        v_aug = jnp.where(low, vp, 1.0) if h % 2 == 0 else jnp.where(low, 1.0, vp)
        va_ref[h, 0] = v_aug.T.astype(BF16)


def fox_prep(proj3, small3, f_bias):
    bsz, s, _ = proj3.shape
    t = FOX_PREP_ROWS
    half = B_WIDTH // 2
    fb = jnp.zeros((1, V7X_LANES), F32).at[0, SMALL_FL:SMALL_FL + B_HEADS].set(f_bias)

    def colspec(col):
        return pl.BlockSpec((None, t, half), lambda b, i, c=col // half: (b, i, c))

    assert t == FOX_TK
    k_spec = pl.BlockSpec((None, B_HEADS, t, V7X_LANES), lambda b, i: (b, 0, i, 0))
    k_shape = jax.ShapeDtypeStruct((bsz, B_HEADS, s, V7X_LANES), BF16)
    q_spec = pl.BlockSpec((None, B_HEADS, V7X_LANES, t), lambda b, i: (b, 0, 0, i))
    q_shape = jax.ShapeDtypeStruct((bsz, B_HEADS, V7X_LANES, s), BF16)
    v_spec = pl.BlockSpec((None, B_HEADS, 1, V7X_LANES, t), lambda b, i: (b, 0, i, 0, 0))
    v_shape = jax.ShapeDtypeStruct((bsz, B_HEADS, s // t, V7X_LANES, t), BF16)
    return pl.pallas_call(
        _fox_prep_kernel, grid=(bsz, s // t),
        in_specs=[pl.BlockSpec((None, t, V7X_LANES), lambda b, i: (b, i, 0)),
                  pl.BlockSpec((1, V7X_LANES), lambda b, i: (0, 0)),
                  colspec(COL_QB), colspec(COL_QB + half),
                  colspec(COL_KB), colspec(COL_KB + half),
                  colspec(COL_VB), colspec(COL_VB + half)],
        out_specs=[q_spec, k_spec, v_spec],
        out_shape=[q_shape, k_shape, v_shape],
        scratch_shapes=[pltpu.VMEM((8, V7X_LANES), F32)],
        compiler_params=_cparams(("parallel", "arbitrary")), name="fox_prep",
    )(small3, fb, proj3, proj3, proj3, proj3, proj3, proj3)


def _fox_kernel(q_ref, k_ref, v_ref, o_ref, m_scr, acc_scr):
    tq, tk = FOX_TQ, FOX_TK
    qi = pl.program_id(2)
    nfull = qi * (tq // tk)
    krow = lax.broadcasted_iota(jnp.int32, (tk, tq), 0)
    qcol = lax.broadcasted_iota(jnp.int32, (tk, tq), 1)
    frow = lax.broadcasted_iota(jnp.int32, (V7X_LANES, tq), 0)
    nh = q_ref.shape[0]
    qs = [q_ref[e] for e in range(nh)]
    m_scr[...] = jnp.full(m_scr.shape, NEG, F32)
    acc_scr[...] = jnp.zeros(acc_scr.shape, F32)

    def heads(kt, mask):
        rows = pl.ds(pl.multiple_of(kt * tk, tk), tk)
        scores = [jnp.dot(k_ref[e, rows, :], qs[e], preferred_element_type=F32) for e in range(nh)]
        for e in range(nh):
            m = m_scr[e, 0:1, :]
            s = scores[e] if mask is None else jnp.where(mask, scores[e], NEG)
            m_new = jnp.maximum(m, jnp.max(s, axis=0, keepdims=True))
            alpha = jnp.exp2(m - m_new)
            p = jnp.exp2(s - m_new)
            pv = jnp.dot(v_ref[e, kt], p.astype(BF16), preferred_element_type=F32)
            acc_scr[e] = alpha * acc_scr[e] + pv
            m_scr[e] = jnp.broadcast_to(m_new, (8, tq))

    def body(kt, c):
        heads(kt, None)
        return c

    lax.fori_loop(0, nfull, body, 0)
    for dd in range(tq // tk):
        heads(nfull + dd, krow + dd * tk <= qcol)
    for g in range(nh // 2):
        even, odd = acc_scr[2 * g], acc_scr[2 * g + 1]
        res = [acc / pltpu.roll(acc, HEAD_DIM, axis=0) for acc in (even, odd)]
        pair = jnp.where(frow < HEAD_DIM, res[0], res[1])
        o_ref[:, V7X_LANES * g:V7X_LANES * (g + 1)] = pair.T.astype(o_ref.dtype)


def fox_attention(q_aug, k_aug, v_aug):
    bsz, nh, s, _ = k_aug.shape
    tq, tk, hs = FOX_TQ, FOX_TK, FOX_HEADS_PER_STEP
    return pl.pallas_call(
        _fox_kernel, grid=(bsz, nh // hs, s // tq),
        in_specs=[pl.BlockSpec((None, hs, V7X_LANES, tq), lambda b, g, i: (b, g, 0, i)),
                  pl.BlockSpec((None, hs, s, V7X_LANES), lambda b, g, i: (b, g, 0, 0)),
                  pl.BlockSpec((None, hs, s // tk, V7X_LANES, tk), lambda b, g, i: (b, g, 0, 0, 0))],
        out_specs=pl.BlockSpec((None, tq, hs * HEAD_DIM), lambda b, g, i: (b, i, g)),
        out_shape=jax.ShapeDtypeStruct((bsz, s, B_WIDTH), BF16),
        scratch_shapes=[pltpu.VMEM((hs, 8, tq), F32), pltpu.VMEM((hs, V7X_LANES, tq), F32)],
        compiler_params=_cparams(("parallel", "parallel", "arbitrary")), name="fox_attention",
    )(q_aug, k_aug, v_aug)


GLA_ROWS = 512


def _gla_chunk(rows, q_ref, k_ref, v_ref, r_ref, sm_bf, w2_ref, cb_ref, gn_ref, o_ref,
               st_scr, b_scr, k_scr, tri, row_sub, row_ck, col_ck):
    ck, sub = C_CHUNK, C_SUB
    nblk = ck // sub
    heads = range(C_HEADS)
    nt = (((1,), (1,)), ((), ()))
    kcols = [slice(C_DK_PAD * h, C_DK_PAD * (h + 1)) for h in heads]
    vcols = [slice(C_DV_PAD * h, C_DV_PAD * (h + 1)) for h in heads]
    g = [jnp.dot(sm_bf, w2_ref[h], preferred_element_type=F32) + cb_ref[h] for h in heads]
    parts = [_split3(_log_sigmoid(g[h]) / C_GATE_TAU) for h in heads]
    b = [(jnp.dot(tri, parts[h][0], preferred_element_type=F32)
          + jnp.dot(tri, parts[h][1], preferred_element_type=F32)
          + jnp.dot(tri, parts[h][2], preferred_element_type=F32)) for h in heads]
    qf = [q_ref[rows, kcols[h]].astype(F32) * (C_DK ** -0.5) for h in heads]
    kf = [k_ref[rows, kcols[h]].astype(F32) for h in heads]
    vb = [v_ref[rows, vcols[h]] for h in heads]
    st = [st_scr[h] for h in heads]
    for h in heads:
        b_scr[h] = b[h]
        k_scr[h] = kf[h]
    o_inter = [lax.dot_general((qf[h] * jnp.exp(b[h])).astype(BF16), st[h].astype(BF16), nt,
                               preferred_element_type=F32) for h in heads]
    a = {}
    for h in heads:
        for blk in range(1, nblk):
            r0 = blk * sub
            ref_row = b_scr[h, r0 - 1:r0, :]
            kd = jnp.where(row_ck < r0, kf[h] * jnp.exp(jnp.minimum(ref_row - b[h], 0.0)), 0.0)
            qd = qf[h][r0:r0 + sub, :] * jnp.exp(b[h][r0:r0 + sub, :] - ref_row)
            a[h, blk] = lax.dot_general(qd.astype(BF16), kd.astype(BF16), nt, preferred_element_type=F32)
    kdec = [(kf[h] * jnp.exp(b_scr[h, ck - 1:ck, :] - b[h])).astype(BF16) for h in heads]
    upd = [lax.dot_general(vb[h], kdec[h], (((0,), (0,)), ((), ())), preferred_element_type=F32)
           for h in heads]
    for h in heads:
        st_scr[h] = st[h] * jnp.exp(b_scr[h, ck - 1:ck, :]) + upd[h]
    for h in heads:
        b2 = b[h] * LOG2E
        outs = []
        for blk in range(nblk):
            r0 = blk * sub
            qb = qf[h][r0:r0 + sub, :]
            bb = b2[r0:r0 + sub, :]
            scores = a[h, blk] if blk > 0 else jnp.zeros((sub, ck), F32)
            for j in range(sub):
                bj = b_scr[h, r0 + j:r0 + j + 1, :] * LOG2E
                kj = k_scr[h, r0 + j:r0 + j + 1, :]
                x = qb * kj * jnp.exp2(bb - bj)
                colv = jnp.sum(x, axis=-1, keepdims=True)
                scores = jnp.where(jnp.logical_and(col_ck == r0 + j, row_sub >= j), colv, scores)
            outs.append(o_inter[h][r0:r0 + sub, :]
                        + jnp.dot(scores.astype(BF16), vb[h], preferred_element_type=F32))
        o = jnp.concatenate(outs, axis=0)
        ms = jnp.sum(o * o, axis=-1, keepdims=True) * (1.0 / C_DV)
        on = o * lax.rsqrt(ms + EPS) * gn_ref[...]
        rr = r_ref[rows, vcols[h]].astype(F32)
        o_ref[rows, vcols[h]] = (on * (rr / (1.0 + jnp.exp(-rr)))).astype(o_ref.dtype)


def _gla_kernel(q_ref, k_ref, v_ref, r_ref, sm_ref, w2_ref, cb_ref, gn_ref, o_ref,
                st_scr, b_scr, k_scr):
    ck, sub = C_CHUNK, C_SUB

    @pl.when(pl.program_id(1) == 0)
    def _():
        st_scr[...] = jnp.zeros_like(st_scr)

    tri = _tril_ones(ck, BF16)
    row_sub = lax.broadcasted_iota(jnp.int32, (sub, 1), 0)
    row_ck = lax.broadcasted_iota(jnp.int32, (ck, 1), 0)
    col_ck = lax.broadcasted_iota(jnp.int32, (sub, ck), 1)

    def chunk(c, carry):
        rows = pl.ds(pl.multiple_of(c * ck, ck), ck)
        sm_bf = sm_ref[rows, :].astype(BF16)
        _gla_chunk(rows, q_ref, k_ref, v_ref, r_ref, sm_bf, w2_ref, cb_ref, gn_ref, o_ref,
                   st_scr, b_scr, k_scr, tri, row_sub, row_ck, col_ck)
        return carry

    lax.fori_loop(0, q_ref.shape[0] // ck, chunk, 0)


def gla_mixer(proj3, small3, w2p, cbp, gnp):
    bsz, s, _ = proj3.shape
    t = GLA_ROWS
    kw, vw = C_HEADS * C_DK_PAD, C_HEADS * C_DV_PAD

    def col(base, width):
        return pl.BlockSpec((None, t, width), lambda b, i, c=base // width: (b, i, c))

    return pl.pallas_call(
        _gla_kernel, grid=(bsz, s // t),
        in_specs=[col(COL_QC, kw), col(COL_KC, kw), col(COL_VC, vw), col(COL_RC, vw),
                  pl.BlockSpec((None, t, V7X_LANES), lambda b, i: (b, i, 0)),
                  pl.BlockSpec((C_HEADS, V7X_LANES, C_DK_PAD), lambda b, i: (0, 0, 0)),
                  pl.BlockSpec((C_HEADS, 1, C_DK_PAD), lambda b, i: (0, 0, 0)),
                  pl.BlockSpec((1, C_DV_PAD), lambda b, i: (0, 0))],
        out_specs=pl.BlockSpec((None, t, vw), lambda b, i: (b, i, 0)),
        out_shape=jax.ShapeDtypeStruct((bsz, s, vw), BF16),
        scratch_shapes=[pltpu.VMEM((C_HEADS, C_DV_PAD, C_DK_PAD), F32),
                        pltpu.VMEM((C_HEADS, C_CHUNK, C_DK_PAD), F32),
                        pltpu.VMEM((C_HEADS, C_CHUNK, C_DK_PAD), F32)],
        compiler_params=_cparams(("parallel", "arbitrary")), name="gla_mixer",
    )(proj3, proj3, proj3, proj3, small3, w2p, cbp, gnp)


MIX_OUT_ROWS = 512


def _mix_out_cross_kernel(x_ref, a_ref, b_ref, c_ref, wa_ref, wb_ref, wc_ref,
                          g_ref, wq_ref, k_ref, v_ref, wo_ref, o_ref):
    acc = jnp.dot(a_ref[...], wa_ref[...], preferred_element_type=F32)
    acc += jnp.dot(b_ref[...], wb_ref[...], preferred_element_type=F32)
    acc += jnp.dot(c_ref[...], wc_ref[...], preferred_element_type=F32)
    x = x_ref[...] + acc
    h = _rms_rows(x, g_ref[...]).astype(BF16)
    q = jnp.dot(h, wq_ref[...], preferred_element_type=F32).astype(BF16)
    cols = [slice(CROSS_DH * hd, CROSS_DH * (hd + 1)) for hd in range(CROSS_HEADS)]
    scores = [lax.dot_general(q[:, cs], k_ref[:, cs], (((1,), (1,)), ((), ())),
                              preferred_element_type=F32) * (CROSS_DH ** -0.5) for cs in cols]
    heads = []
    for cs, logits in zip(cols, scores):
        m = jnp.max(logits, axis=-1, keepdims=True)
        p = jnp.exp(logits - m)
        ssum = jnp.sum(p, axis=-1, keepdims=True)
        pv = jnp.dot(p.astype(BF16), v_ref[:, cs], preferred_element_type=F32)
        heads.append((pv / ssum).astype(BF16))
    o = jnp.concatenate(heads, axis=1)
    o_ref[...] = x + jnp.dot(o, wo_ref[...], preferred_element_type=F32)


def mix_out_cross(x2d, oa, ob, oc, wa, wb, wc, gain, wq, kv, wo, *, seq):
    m, d = x2d.shape
    tm = MIX_OUT_ROWS
    mem_len = kv.shape[1]
    per_batch = seq // tm

    def rows(a):
        return pl.BlockSpec((tm, a.shape[1]), lambda i: (i, 0))

    def resident(a):
        return pl.BlockSpec(a.shape, lambda i: (0,) * a.ndim, pipeline_mode=pl.Buffered(1))

    g2 = gain.reshape(1, d)
    return pl.pallas_call(
        _mix_out_cross_kernel, grid=(m // tm,),
        in_specs=[rows(x2d), rows(oa), rows(ob), rows(oc), resident(wa), resident(wb), resident(wc),
                  resident(g2), resident(wq),
                  pl.BlockSpec((None, mem_len, CROSS_WIDTH), lambda i: (i // per_batch, 0, 0)),
                  pl.BlockSpec((None, mem_len, CROSS_WIDTH), lambda i: (i // per_batch, 0, 1)),
                  resident(wo)],
        out_specs=pl.BlockSpec((tm, d), lambda i: (i, 0)),
        out_shape=jax.ShapeDtypeStruct((m, d), F32),
        compiler_params=_cparams(("parallel",)), name="mix_out_cross",
    )(x2d, oa, ob, oc, wa, wb, wc, g2, wq, kv, kv, wo)


FFN_ROWS = 1024
FFN_CHUNK = FFN_ROWS
FFN_HALO = 16


def _ffn_kernel(x_ref, xh_ref, g_ref, wv_ref, wg_ref, cw_ref, cb_ref, wd_ref, gf_ref, o_ref,
                h_scr, hh_scr, *, final):
    j = pl.program_id(1)

    @pl.when(j == 0)
    def _():
        _norm_to_scratch(x_ref, g_ref, h_scr)
        hh_scr[...] = _rms_rows(xh_ref[...], g_ref[...]).astype(BF16)
        o_ref[...] = x_ref[...]

    rc = FFN_CHUNK
    nchunk = h_scr.shape[0] // rc
    prev = jnp.dot(hh_scr[...], wg_ref[...], preferred_element_type=F32)
    row = lax.broadcasted_iota(jnp.int32, (rc, wg_ref.shape[1]), 0)
    vals, gates = [], []
    for c in range(nchunk):
        h = h_scr[c * rc:(c + 1) * rc, :]
        gates.append(jnp.dot(h, wg_ref[...], preferred_element_type=F32))
        vals.append(jnp.dot(h, wv_ref[...], preferred_element_type=F32))
    for c in range(nchunk):
        rows = slice(c * rc, (c + 1) * rc)
        val, gate = vals[c], gates[c]
        prev1 = prev[prev.shape[0] - 1:, :]
        prev2 = prev[prev.shape[0] - 2:prev.shape[0] - 1, :]
        g1 = jnp.where(row == 0, prev1, pltpu.roll(gate, 1, axis=0))
        g2 = jnp.where(row == 0, prev2, jnp.where(row == 1, prev1, pltpu.roll(gate, 2, axis=0)))
        gc = cb_ref[...] + cw_ref[0:1, :] * g2
        gc = gc + cw_ref[1:2, :] * g1
        gc = gc + cw_ref[2:3, :] * gate
        act = (gc / (1.0 + jnp.exp(-gc)) * val).astype(BF16)
        o_ref[rows, :] += jnp.dot(act, wd_ref[...], preferred_element_type=F32)
        prev = gate

    if final:
        @pl.when(j == pl.num_programs(1) - 1)
        def _():
            def body(r, c):
                sl = pl.ds(pl.multiple_of(r * NORM_ROWS, NORM_ROWS), NORM_ROWS)
                o_ref[sl, :] = _rms_rows(o_ref[sl, :], gf_ref[...])
                return c
            lax.fori_loop(0, o_ref.shape[0] // NORM_ROWS, body, 0)


def conv_ffn(x2d, gain, wv, wg, cw, cb, wd, gain_final, *, seq, final):
    m, d = x2d.shape
    tm, tf = FFN_ROWS, FF_TILE
    nt = m // tm
    tail = x2d.reshape(nt, tm, d)[:, tm - FFN_HALO:, :]
    halo = jnp.concatenate([jnp.zeros_like(tail[:1]), tail[:-1]], axis=0)
    starts = (jnp.arange(nt) % (seq // tm) == 0)[:, None, None]
    halo = jnp.where(starts, 0.0, halo)
    ff = wv.shape[1]
    return pl.pallas_call(
        functools.partial(_ffn_kernel, final=final), grid=(nt, ff // tf),
        in_specs=[pl.BlockSpec((tm, d), lambda i, j: (i, 0)),
                  pl.BlockSpec((None, FFN_HALO, d), lambda i, j: (i, 0, 0)),
                  pl.BlockSpec((1, d), lambda i, j: (0, 0)),
                  pl.BlockSpec((d, tf), lambda i, j: (0, j)),
                  pl.BlockSpec((d, tf), lambda i, j: (0, j)),
                  pl.BlockSpec((CONV_W, tf), lambda i, j: (0, j)),
                  pl.BlockSpec((1, tf), lambda i, j: (0, j)),
                  pl.BlockSpec((tf, d), lambda i, j: (j, 0)),
                  pl.BlockSpec((1, d), lambda i, j: (0, 0))],
        out_specs=pl.BlockSpec((tm, d), lambda i, j: (i, 0)),
        out_shape=jax.ShapeDtypeStruct((m, d), F32),
        scratch_shapes=[pltpu.VMEM((tm, d), BF16), pltpu.VMEM((FFN_HALO, d), BF16)],
        compiler_params=_cparams(("parallel", "arbitrary")), name="conv_ffn",
    )(x2d, halo, gain.reshape(1, d), wv, wg, cw, cb.reshape(1, ff), wd, gain_final.reshape(1, d))


def _pad_heads(w, heads, width, padded):
    lead = w.shape[:-1]
    w = w.reshape(lead + (heads, width))
    w = jnp.pad(w, [(0, 0)] * len(lead) + [(0, 0), (0, padded - width)])
    return w.reshape(lead + (heads * padded,))


def _layer_params(l, w_in, c_gate_w2, c_gate_b, c_norm, w_out, w_up, conv_w, conv_b, w_down):
    splits = np.cumsum([A_WIDTH] * 3 + [B_WIDTH] * 3 + [B_HEADS, C_KW, C_KW, C_VW, C_VW, C_GATE_RANK])
    w_in_l = w_in[l].astype(BF16)
    fl, qc, kc, vc, rc, gl = jnp.split(w_in_l[:, splits[5]:], (splits[6:11] - splits[5]).tolist(), axis=1)
    w_all = jnp.concatenate([
        _pad_heads(vc, C_HEADS, C_DV, C_DV_PAD), _pad_heads(rc, C_HEADS, C_DV, C_DV_PAD),
        _pad_heads(qc, C_HEADS, C_DK, C_DK_PAD), _pad_heads(kc, C_HEADS, C_DK, C_DK_PAD),
        w_in_l[:, :splits[5]]],
        axis=1)
    w_small = jnp.zeros((D_MODEL, V7X_LANES), BF16)
    w_small = w_small.at[:, SMALL_FL:SMALL_FL + B_HEADS].set(fl)
    w_small = w_small.at[:, SMALL_GL:SMALL_GL + C_GATE_RANK].set(gl)
    w2 = _pad_heads(c_gate_w2[l], C_HEADS, C_DK, C_DK_PAD).reshape(C_GATE_RANK, C_HEADS, C_DK_PAD)
    w2p = jnp.zeros((C_HEADS, V7X_LANES, C_DK_PAD), F32)
    w2p = w2p.at[:, SMALL_GL:SMALL_GL + C_GATE_RANK, :].set(jnp.transpose(w2, (1, 0, 2))).astype(BF16)
    cbp = _pad_heads(c_gate_b[l], C_HEADS, C_DK, C_DK_PAD).reshape(C_HEADS, 1, C_DK_PAD)
    gnp = jnp.pad(c_norm[l], (0, C_DV_PAD - C_DV)).reshape(1, C_DV_PAD)
    wo = w_out[l].astype(BF16)
    wa = wo[:A_WIDTH]
    wb = wo[A_WIDTH:A_WIDTH + B_WIDTH]
    wc = wo[A_WIDTH + B_WIDTH:].reshape(C_HEADS, C_DV, D_MODEL)
    wc = jnp.pad(wc, ((0, 0), (0, C_DV_PAD - C_DV), (0, 0))).reshape(C_HEADS * C_DV_PAD, D_MODEL)
    fpad = D_FF_PAD - D_FF
    w_up_l = w_up[l].astype(BF16)
    wv = jnp.pad(w_up_l[:, :D_FF], ((0, 0), (0, fpad)))
    wg = jnp.pad(w_up_l[:, D_FF:], ((0, 0), (0, fpad)))
    cw = jnp.pad(conv_w[l], ((0, 0), (0, fpad)))
    cb = jnp.pad(conv_b[l], (0, fpad))
    wd = jnp.pad(w_down[l].astype(BF16), ((0, fpad), (0, 0)))
    return dict(w_all=w_all, w_small=w_small, w2p=w2p, cbp=cbp, gnp=gnp,
                wa=wa, wb=wb, wc=wc, wv=wv, wg=wg, cw=cw, cb=cb, wd=wd)


def hybrid_mixer(x2d, gain, p, f_bias, bias_a, *, bsz, seq):
    proj, small = in_proj(x2d, gain, p["w_all"], p["w_small"])
    proj3 = proj.reshape(bsz, seq, PROJ_PAD)
    small3 = small.reshape(bsz, seq, V7X_LANES)
    o_a = dilated_mixture(proj3, bias_a).reshape(bsz * seq, A_WIDTH)
    o_b = fox_attention(*fox_prep(proj3, small3, f_bias)).reshape(bsz * seq, B_WIDTH)
    o_c = gla_mixer(proj3, small3, p["w2p"], p["cbp"], p["gnp"]).reshape(bsz * seq, C_HEADS * C_DV_PAD)
    return o_a, o_b, o_c


def kernel(x, mem, rel_table, mem_norm, norm_final, norm_mix, w_in, f_bias, c_gate_w2, c_gate_b,
           c_norm, w_out, norm_cross, w_cq, w_ckv, w_co, norm_ffn, w_up, conv_w, conv_b, w_down):
    bsz, seq, d = x.shape
    depth = w_in.shape[0]
    mem_len = mem.shape[1]
    assert d == D_MODEL and seq % (A_SPAN * A_PATTERNS[-1][1]) == 0 and seq % 1024 == 0
    bias_a = jnp.stack([_dilated_bias(rel_table, dil) for _, dil in A_PATTERNS], axis=0)
    x2d = x.reshape(bsz * seq, d)
    mem2d = mem.reshape(bsz * mem_len, d)
    for l in range(depth):
        p = _layer_params(l, w_in, c_gate_w2, c_gate_b, c_norm, w_out, w_up, conv_w, conv_b, w_down)
        o_a, o_b, o_c = hybrid_mixer(x2d, norm_mix[l], p, f_bias[l], bias_a, bsz=bsz, seq=seq)
        kv = norm_matmul(mem2d, mem_norm, w_ckv[l].astype(BF16), tm=min(512, bsz * mem_len), tn=512,
                         name="mem_kv").reshape(bsz, mem_len, 2 * CROSS_WIDTH)
        x2d = mix_out_cross(x2d, o_a, o_b, o_c, p["wa"], p["wb"], p["wc"], norm_cross[l],
                            w_cq[l].astype(BF16), kv, w_co[l].astype(BF16), seq=seq)
        x2d = conv_ffn(x2d, norm_ffn[l], p["wv"], p["wg"], p["cw"], p["cb"], p["wd"], norm_final,
                       seq=seq, final=l == depth - 1)
    return x2d.reshape(bsz, seq, d)
```

```python
import functools
import math

import numpy as np
import jax
import jax.numpy as jnp
from jax import lax
from jax.experimental import pallas as pl
from jax.experimental.pallas import tpu as pltpu

F32 = jnp.float32
BF16 = jnp.bfloat16

V7X_LANES = 128
V7X_VMEM_LIMIT_BYTES = 56 * 1024 * 1024

D_MODEL = 2048
HEAD_DIM = 64
A_WIDTH = 3 * D_MODEL // 8
A_HEADS = A_WIDTH // HEAD_DIM
A_PATTERNS = ((128, 1), (512, 4), (2048, 16))
A_SPAN = 128
B_WIDTH = D_MODEL // 4
B_HEADS = B_WIDTH // HEAD_DIM
C_HEADS = 4
C_VW = 3 * D_MODEL // 8
C_DV = C_VW // C_HEADS
C_DK = C_DV // 2
C_KW = C_HEADS * C_DK
C_GATE_RANK = 16
C_GATE_TAU = 16.0
C_CHUNK = 64
C_SUB = 8
C_DK_PAD = 128
C_DV_PAD = 256
CROSS_HEADS = 4
CROSS_DH = 128
CROSS_WIDTH = CROSS_HEADS * CROSS_DH
D_FF = ((8 * D_MODEL // 3 + 127) // 128) * 128
FF_TILE = 512
D_FF_PAD = ((D_FF + FF_TILE - 1) // FF_TILE) * FF_TILE
CONV_W = 3
REL_BUCKETS = 32
REL_MAX_DIST = 2048
EPS = 1e-6
NEG = -1e30
LOG2E = 1.4426950408889634

COL_VC = 0
COL_RC = COL_VC + C_HEADS * C_DV_PAD
COL_QC = COL_RC + C_HEADS * C_DV_PAD
COL_KC = COL_QC + C_HEADS * C_DK_PAD
COL_QA = COL_KC + C_HEADS * C_DK_PAD
COL_KA = COL_QA + A_WIDTH
COL_VA = COL_KA + A_WIDTH
COL_QB = COL_VA + A_WIDTH
COL_KB = COL_QB + B_WIDTH
COL_VB = COL_KB + B_WIDTH
PROJ_PAD = COL_VB + B_WIDTH
SMALL_FL = 0
SMALL_GL = B_HEADS


def _cparams(sem):
    return pltpu.CompilerParams(dimension_semantics=sem,
                                vmem_limit_bytes=V7X_VMEM_LIMIT_BYTES)


def _rms_rows(x, g):
    ms = jnp.mean(x * x, axis=-1, keepdims=True)
    return x * lax.rsqrt(ms + EPS) * g


def _split3(x):
    hi = x.astype(BF16)
    r1 = x - hi.astype(F32)
    mid = r1.astype(BF16)
    lo = (r1 - mid.astype(F32)).astype(BF16)
    return hi, mid, lo


def _log_sigmoid(x):
    return jnp.minimum(x, 0.0) - jnp.log(1.0 + jnp.exp(-jnp.abs(x)))


def _tril_ones(n, dtype):
    r = lax.broadcasted_iota(jnp.int32, (n, n), 0)
    c = lax.broadcasted_iota(jnp.int32, (n, n), 1)
    return jnp.where(r >= c, 1.0, 0.0).astype(dtype)


NORM_ROWS = 256


def _norm_to_scratch(x_ref, g_ref, h_scr):
    def body(r, c):
        sl = pl.ds(pl.multiple_of(r * NORM_ROWS, NORM_ROWS), NORM_ROWS)
        h_scr[sl, :] = _rms_rows(x_ref[sl, :], g_ref[...]).astype(BF16)
        return c
    lax.fori_loop(0, x_ref.shape[0] // NORM_ROWS, body, 0)


def _norm_matmul_kernel(x_ref, g_ref, w_ref, o_ref, h_scr):
    @pl.when(pl.program_id(1) == 0)
    def _():
        _norm_to_scratch(x_ref, g_ref, h_scr)
    o_ref[...] = jnp.dot(h_scr[...], w_ref[...],
                         preferred_element_type=F32).astype(o_ref.dtype)


def norm_matmul(x2d, gain, w, *, tm, tn, name):
    m, k = x2d.shape
    n = w.shape[1]
    return pl.pallas_call(
        _norm_matmul_kernel, grid=(m // tm, n // tn),
        in_specs=[pl.BlockSpec((tm, k), lambda i, j: (i, 0)),
                  pl.BlockSpec((1, k), lambda i, j: (0, 0)),
                  pl.BlockSpec((k, tn), lambda i, j: (0, j))],
        out_specs=pl.BlockSpec((tm, tn), lambda i, j: (i, j)),
        out_shape=jax.ShapeDtypeStruct((m, n), BF16),
        scratch_shapes=[pltpu.VMEM((tm, k), BF16)],
        compiler_params=_cparams(("parallel", "arbitrary")), name=name)(x2d, gain.reshape(1, k), w)


IN_PROJ_ROWS = 1024
IN_PROJ_TILE = 3 * A_WIDTH


def _in_proj_kernel(x_ref, g_ref, w_ref, ws_ref, o_ref, os_ref, h_scr):
    @pl.when(pl.program_id(1) == 0)
    def _():
        _norm_to_scratch(x_ref, g_ref, h_scr)
        os_ref[...] = jnp.dot(h_scr[...], ws_ref[...], preferred_element_type=F32)

    o_ref[...] = jnp.dot(h_scr[...], w_ref[...], preferred_element_type=F32).astype(o_ref.dtype)


def in_proj(x2d, gain, w_all, w_small):
    m, k = x2d.shape
    tm, tn = IN_PROJ_ROWS, IN_PROJ_TILE
    ns = w_small.shape[1]
    return pl.pallas_call(
        _in_proj_kernel, grid=(m // tm, PROJ_PAD // tn),
        in_specs=[pl.BlockSpec((tm, k), lambda i, j: (i, 0)),
                  pl.BlockSpec((1, k), lambda i, j: (0, 0)),
                  pl.BlockSpec((k, tn), lambda i, j: (0, j)),
                  pl.BlockSpec((k, ns), lambda i, j: (0, 0))],
        out_specs=[pl.BlockSpec((tm, tn), lambda i, j: (i, j)),
                   pl.BlockSpec((tm, ns), lambda i, j: (i, 0))],
        out_shape=[jax.ShapeDtypeStruct((m, PROJ_PAD), BF16),
                   jax.ShapeDtypeStruct((m, ns), F32)],
        scratch_shapes=[pltpu.VMEM((tm, k), BF16)],
        compiler_params=_cparams(("parallel", "arbitrary")), name="in_proj",
    )(x2d, gain.reshape(1, k), w_all, w_small)


def _t5_bucket(dist):
    max_exact = REL_BUCKETS // 2
    d = np.maximum(dist, 1).astype(np.float32)
    large = max_exact + (np.log(d / max_exact) / math.log(REL_MAX_DIST / max_exact)
                         * (REL_BUCKETS - max_exact)).astype(np.int32)
    return np.where(dist < max_exact, dist, np.minimum(large, REL_BUCKETS - 1)).astype(np.int32)


def _dilated_bias(rel_table, dilation):
    span = A_SPAN
    steps = np.arange(2 * span, -span, -1)
    valid = (steps >= 0) & (steps <= span)
    per_step = rel_table[_t5_bucket(np.clip(steps, 0, span) * dilation)].astype(F32)
    per_step = jnp.where(valid[:, None], per_step * LOG2E, NEG).T
    later = jnp.stack([per_step[:, 2 * span - i - span: 4 * span - i - span] for i in range(span)],
                      axis=1)
    first = jnp.where((np.arange(2 * span) >= span)[None, None, :], later, NEG)
    return jnp.stack([first, later], axis=0)


DIL_TILE = A_SPAN * max(d for _, d in A_PATTERNS)
DIL_UNITS = DIL_TILE // A_SPAN
DIL_GROUP = 4
DIL_COMBINE_ROWS = 256


def _dilated_kernel(qb_ref, kp_ref, kc_ref, vp_ref, vc_ref, bias_ref, o_ref,
                    q_ref, kk_scr, vv_scr, m_scr, n_scr, s_scr):
    tile, span = DIL_TILE, A_SPAN
    first_tile = pl.program_id(1) == 0
    q_ref[...] = qb_ref[...].astype(F32)
    kk_scr[0:tile, :] = kp_ref[...].astype(F32)
    kk_scr[tile:2 * tile, :] = kc_ref[...].astype(F32)
    vv_scr[0:tile, :] = vp_ref[...].astype(F32)
    vv_scr[tile:2 * tile, :] = vc_ref[...].astype(F32)
    lane_q = lax.broadcasted_iota(jnp.int32, (span, V7X_LANES), 1)
    lane_k = lax.broadcasted_iota(jnp.int32, (2 * span, V7X_LANES), 1)
    low_q = lane_q < HEAD_DIM
    low_k = lane_k < HEAD_DIM
    scale = HEAD_DIM ** -0.5 * LOG2E
    qmask = (jnp.where(low_q, scale, 0.0), jnp.where(low_q, 0.0, scale))
    nt = (((1,), (1,)), ((), ()))

    for pat, (_, d) in enumerate(A_PATTERNS):
        for u0 in range(0, DIL_UNITS, DIL_GROUP):
            units = []
            for u in range(u0, u0 + DIL_GROUP):
                n, r = divmod(u, d)
                q0 = r + span * d * n
                k0 = tile + q0 - span * d
                if d == 1:
                    qrows, krows = pl.ds(q0, span), pl.ds(k0, 2 * span)
                else:
                    qrows, krows = pl.ds(q0, span, stride=d), pl.ds(k0, 2 * span, stride=d)
                qf = q_ref[qrows, :]
                kb = kk_scr[krows, :].astype(BF16)
                sel = jnp.where(first_tile, 0, 1) if n == 0 else 1
                logits = [lax.dot_general((qf * qmask[e]).astype(BF16), kb, nt,
                                          preferred_element_type=F32) + bias_ref[pat, sel, e]
                          for e in range(2)]
                units.append((qrows, krows, logits))
            results = []
            for qrows, krows, logits in units:
                vf = vv_scr[krows, :]
                pvs, ms = [], []
                for e in range(2):
                    m = jnp.max(logits[e], axis=-1, keepdims=True)
                    p = jnp.exp2(logits[e] - m)
                    ve = (jnp.where(low_k, vf, 1.0) if e == 0 else jnp.where(low_k, 1.0, vf)).astype(BF16)
                    pvs.append(jnp.dot(p.astype(BF16), ve, preferred_element_type=F32))
                    ms.append(m)
                results.append((qrows, pvs, ms))
            for qrows, pvs, ms in results:
                m_scr[pat, qrows, :] = jnp.where(low_q, ms[0], ms[1])
                n_scr[pat, qrows, :] = jnp.where(low_q, pvs[0], pvs[1])
                s_scr[pat, qrows, :] = pltpu.roll(jnp.where(low_q, pvs[1], pvs[0]), HEAD_DIM, axis=1)

    def combine(c, carry):
        rows = pl.ds(pl.multiple_of(c * DIL_COMBINE_ROWS, DIL_COMBINE_ROWS), DIL_COMBINE_ROWS)
        ms = [m_scr[pat, rows, :] for pat in range(len(A_PATTERNS))]
        top = functools.reduce(jnp.maximum, ms)
        ws = [jnp.exp2(m - top) for m in ms]
        num = sum(w * n_scr[pat, rows, :] for pat, w in enumerate(ws))
        den = sum(w * s_scr[pat, rows, :] for pat, w in enumerate(ws))
        o_ref[rows, :] = (num / den).astype(o_ref.dtype)
        return carry

    lax.fori_loop(0, tile // DIL_COMBINE_ROWS, combine, 0)


def dilated_mixture(proj3, bias):
    bsz, s, _ = proj3.shape
    pairs = A_HEADS // 2
    blk = (None, DIL_TILE, V7X_LANES)
    cq, ck, cv = COL_QA // V7X_LANES, COL_KA // V7X_LANES, COL_VA // V7X_LANES

    def cur(col):
        return pl.BlockSpec(blk, lambda b, t, g: (b, t, col + g))

    def prev(col):
        return pl.BlockSpec(blk, lambda b, t, g: (b, jnp.maximum(t - 1, 0), col + g))

    tile_f32 = pltpu.VMEM((DIL_TILE, V7X_LANES), F32)
    both_f32 = pltpu.VMEM((2 * DIL_TILE, V7X_LANES), F32)
    stats_f32 = pltpu.VMEM((len(A_PATTERNS), DIL_TILE, V7X_LANES), F32)
    return pl.pallas_call(
        _dilated_kernel, grid=(bsz, s // DIL_TILE, pairs),
        in_specs=[cur(cq), prev(ck), cur(ck), prev(cv), cur(cv),
                  pl.BlockSpec((len(A_PATTERNS), 2, 2, A_SPAN, 2 * A_SPAN),
                               lambda b, t, g: (0, 0, g, 0, 0))],
        out_specs=pl.BlockSpec(blk, lambda b, t, g: (b, t, g)),
        out_shape=jax.ShapeDtypeStruct((bsz, s, A_WIDTH), BF16),
        scratch_shapes=[tile_f32, both_f32, both_f32, stats_f32, stats_f32, stats_f32],
        compiler_params=_cparams(("parallel", "parallel", "arbitrary")), name="dilated_mixture",
    )(proj3, proj3, proj3, proj3, proj3, bias)


FOX_PREP_ROWS = 512
FOX_TQ = 512
FOX_TK = 512
FOX_HEADS_PER_STEP = 4


def _fox_prep_kernel(sm_ref, fb_ref, q0_ref, q1_ref, k0_ref, k1_ref, v0_ref, v1_ref,
                     qa_ref, ka_ref, va_ref, carry_scr):
    t = sm_ref.shape[0]

    @pl.when(pl.program_id(1) == 0)
    def _():
        carry_scr[...] = jnp.zeros_like(carry_scr)

    x = _log_sigmoid(sm_ref[...] + fb_ref[...])
    tri = _tril_ones(t, BF16)
    hi, mid, lo = _split3(x)
    c = (jnp.dot(tri, hi, preferred_element_type=F32)
         + jnp.dot(tri, mid, preferred_element_type=F32)
         + jnp.dot(tri, lo, preferred_element_type=F32)) + carry_scr[0:1, :]
    carry_scr[...] = jnp.broadcast_to(c[t - 1:t, :], carry_scr.shape)
    chi, cmid, clo = [p.astype(F32) for p in _split3(c * LOG2E)]

    lane = lax.broadcasted_iota(jnp.int32, (t, V7X_LANES), 1)
    low = lane < HEAD_DIM
    scale = HEAD_DIM ** -0.5 * LOG2E
    q_refs, k_refs, v_refs = (q0_ref, q1_ref), (k0_ref, k1_ref), (v0_ref, v1_ref)
    for h in range(B_HEADS):
        src, pair = divmod(h // 2, 2)
        cs = slice(V7X_LANES * pair, V7X_LANES * (pair + 1))
        qp = q_refs[src][:, cs].astype(F32)
        kp = k_refs[src][:, cs].astype(F32)
        vp = v_refs[src][:, cs].astype(F32)
        if h % 2 == 1:
            qp = pltpu.roll(qp, HEAD_DIM, axis=1)
            kp = pltpu.roll(kp, HEAD_DIM, axis=1)
        c1, c2, c3 = chi[:, h:h + 1], cmid[:, h:h + 1], clo[:, h:h + 1]
        one = jnp.ones((t, 1), F32)
        zero = jnp.zeros((t, 1), F32)
        q_aug, k_aug = zero, zero
        for idx, (qv, kv) in enumerate(((c1, one), (c2, one), (c3, one),
                                        (one, -c1), (one, -c2), (one, -c3))):
            hit = lane == HEAD_DIM + idx
            q_aug = jnp.where(hit, qv, q_aug)
            k_aug = jnp.where(hit, kv, k_aug)
        qa_ref[h] = jnp.where(low, qp * scale, q_aug).T.astype(BF16)
        ka_ref[h] = jnp.where(low, kp, k_aug).astype(BF16)
        v_aug = jnp.where(low, vp, 1.0) if h % 2 == 0 else jnp.where(low, 1.0, vp)
        va_ref[h, 0] = v_aug.T.astype(BF16)


def fox_prep(proj3, small3, f_bias):
    bsz, s, _ = proj3.shape
    t = FOX_PREP_ROWS
    half = B_WIDTH // 2
    fb = jnp.zeros((1, V7X_LANES), F32).at[0, SMALL_FL:SMALL_FL + B_HEADS].set(f_bias)

    def colspec(col):
        return pl.BlockSpec((None, t, half), lambda b, i, c=col // half: (b, i, c))

    assert t == FOX_TK
    k_spec = pl.BlockSpec((None, B_HEADS, t, V7X_LANES), lambda b, i: (b, 0, i, 0))
    k_shape = jax.ShapeDtypeStruct((bsz, B_HEADS, s, V7X_LANES), BF16)
    q_spec = pl.BlockSpec((None, B_HEADS, V7X_LANES, t), lambda b, i: (b, 0, 0, i))
    q_shape = jax.ShapeDtypeStruct((bsz, B_HEADS, V7X_LANES, s), BF16)
    v_spec = pl.BlockSpec((None, B_HEADS, 1, V7X_LANES, t), lambda b, i: (b, 0, i, 0, 0))
    v_shape = jax.ShapeDtypeStruct((bsz, B_HEADS, s // t, V7X_LANES, t), BF16)
    return pl.pallas_call(
        _fox_prep_kernel, grid=(bsz, s // t),
        in_specs=[pl.BlockSpec((None, t, V7X_LANES), lambda b, i: (b, i, 0)),
                  pl.BlockSpec((1, V7X_LANES), lambda b, i: (0, 0)),
                  colspec(COL_QB), colspec(COL_QB + half),
                  colspec(COL_KB), colspec(COL_KB + half),
                  colspec(COL_VB), colspec(COL_VB + half)],
        out_specs=[q_spec, k_spec, v_spec],
        out_shape=[q_shape, k_shape, v_shape],
        scratch_shapes=[pltpu.VMEM((8, V7X_LANES), F32)],
        compiler_params=_cparams(("parallel", "arbitrary")), name="fox_prep",
    )(small3, fb, proj3, proj3, proj3, proj3, proj3, proj3)


def _fox_kernel(q_ref, k_ref, v_ref, o_ref, m_scr, acc_scr):
    tq, tk = FOX_TQ, FOX_TK
    qi = pl.program_id(2)
    nfull = qi * (tq // tk)
    krow = lax.broadcasted_iota(jnp.int32, (tk, tq), 0)
    qcol = lax.broadcasted_iota(jnp.int32, (tk, tq), 1)
    frow = lax.broadcasted_iota(jnp.int32, (V7X_LANES, tq), 0)
    nh = q_ref.shape[0]
    qs = [q_ref[e] for e in range(nh)]
    m_scr[...] = jnp.full(m_scr.shape, NEG, F32)
    acc_scr[...] = jnp.zeros(acc_scr.shape, F32)

    def heads(kt, mask):
        rows = pl.ds(pl.multiple_of(kt * tk, tk), tk)
        scores = [jnp.dot(k_ref[e, rows, :], qs[e], preferred_element_type=F32) for e in range(nh)]
        for e in range(nh):
            m = m_scr[e, 0:1, :]
            s = scores[e] if mask is None else jnp.where(mask, scores[e], NEG)
            m_new = jnp.maximum(m, jnp.max(s, axis=0, keepdims=True))
            alpha = jnp.exp2(m - m_new)
            p = jnp.exp2(s - m_new)
            pv = jnp.dot(v_ref[e, kt], p.astype(BF16), preferred_element_type=F32)
            acc_scr[e] = alpha * acc_scr[e] + pv
            m_scr[e] = jnp.broadcast_to(m_new, (8, tq))

    def body(kt, c):
        heads(kt, None)
        return c

    lax.fori_loop(0, nfull, body, 0)
    for dd in range(tq // tk):
        heads(nfull + dd, krow + dd * tk <= qcol)
    for g in range(nh // 2):
        even, odd = acc_scr[2 * g], acc_scr[2 * g + 1]
        res = [acc / pltpu.roll(acc, HEAD_DIM, axis=0) for acc in (even, odd)]
        pair = jnp.where(frow < HEAD_DIM, res[0], res[1])
        o_ref[:, V7X_LANES * g:V7X_LANES * (g + 1)] = pair.T.astype(o_ref.dtype)


def fox_attention(q_aug, k_aug, v_aug):
    bsz, nh, s, _ = k_aug.shape
    tq, tk, hs = FOX_TQ, FOX_TK, FOX_HEADS_PER_STEP
    return pl.pallas_call(
        _fox_kernel, grid=(bsz, nh // hs, s // tq),
        in_specs=[pl.BlockSpec((None, hs, V7X_LANES, tq), lambda b, g, i: (b, g, 0, i)),
                  pl.BlockSpec((None, hs, s, V7X_LANES), lambda b, g, i: (b, g, 0, 0)),
                  pl.BlockSpec((None, hs, s // tk, V7X_LANES, tk), lambda b, g, i: (b, g, 0, 0, 0))],
        out_specs=pl.BlockSpec((None, tq, hs * HEAD_DIM), lambda b, g, i: (b, i, g)),
        out_shape=jax.ShapeDtypeStruct((bsz, s, B_WIDTH), BF16),
        scratch_shapes=[pltpu.VMEM((hs, 8, tq), F32), pltpu.VMEM((hs, V7X_LANES, tq), F32)],
        compiler_params=_cparams(("parallel", "parallel", "arbitrary")), name="fox_attention",
    )(q_aug, k_aug, v_aug)


GLA_ROWS = 512
GLA_BATCH = 4


def _gla_chunk(rows, q_ref, k_ref, v_ref, r_ref, sm_bf, w2_ref, cb_ref, gn_ref, o_ref,
               st_scr, b_scr, k_scr, tri, row_sub, row_ck, col_ck):
    ck, sub = C_CHUNK, C_SUB
    nblk = ck // sub
    heads = range(q_ref.shape[0] * C_HEADS)
    bi = [h // C_HEADS for h in heads]
    hh = [h % C_HEADS for h in heads]
    nt = (((1,), (1,)), ((), ()))
    kcols = [slice(C_DK_PAD * hh[h], C_DK_PAD * (hh[h] + 1)) for h in heads]
    vcols = [slice(C_DV_PAD * hh[h], C_DV_PAD * (hh[h] + 1)) for h in heads]
    g = [jnp.dot(sm_bf[bi[h]], w2_ref[hh[h]], preferred_element_type=F32) + cb_ref[hh[h]] for h in heads]
    parts = [_split3(_log_sigmoid(g[h]) / C_GATE_TAU) for h in heads]
    b = [(jnp.dot(tri, parts[h][0], preferred_element_type=F32)
          + jnp.dot(tri, parts[h][1], preferred_element_type=F32)
          + jnp.dot(tri, parts[h][2], preferred_element_type=F32)) for h in heads]
    qf = [q_ref[bi[h], rows, kcols[h]].astype(F32) * (C_DK ** -0.5) for h in heads]
    kf = [k_ref[bi[h], rows, kcols[h]].astype(F32) for h in heads]
    vb = [v_ref[bi[h], rows, vcols[h]] for h in heads]
    st = [st_scr[h] for h in heads]
    for h in heads:
        b_scr[h] = b[h]
        k_scr[h] = kf[h]
    o_inter = [lax.dot_general((qf[h] * jnp.exp(b[h])).astype(BF16), st[h].astype(BF16), nt,
                               preferred_element_type=F32) for h in heads]
    a = {}
    for h in heads:
        for blk in range(1, nblk):
            r0 = blk * sub
            ref_row = b_scr[h, r0 - 1:r0, :]
            kd = jnp.where(row_ck < r0, kf[h] * jnp.exp(jnp.minimum(ref_row - b[h], 0.0)), 0.0)
            qd = qf[h][r0:r0 + sub, :] * jnp.exp(b[h][r0:r0 + sub, :] - ref_row)
            a[h, blk] = lax.dot_general(qd.astype(BF16), kd.astype(BF16), nt, preferred_element_type=F32)
    kdec = [(kf[h] * jnp.exp(b_scr[h, ck - 1:ck, :] - b[h])).astype(BF16) for h in heads]
    upd = [lax.dot_general(vb[h], kdec[h], (((0,), (0,)), ((), ())), preferred_element_type=F32)
           for h in heads]
    for h in heads:
        st_scr[h] = st[h] * jnp.exp(b_scr[h, ck - 1:ck, :]) + upd[h]
    for h in heads:
        b2 = b[h] * LOG2E
        outs = []
        for blk in range(nblk):
            r0 = blk * sub
            qb = qf[h][r0:r0 + sub, :]
            bb = b2[r0:r0 + sub, :]
            scores = a[h, blk] if blk > 0 else jnp.zeros((sub, ck), F32)
            for j in range(sub):
                bj = b_scr[h, r0 + j:r0 + j + 1, :] * LOG2E
                kj = k_scr[h, r0 + j:r0 + j + 1, :]
                x = qb * kj * jnp.exp2(bb - bj)
                colv = jnp.sum(x, axis=-1, keepdims=True)
                scores = jnp.where(jnp.logical_and(col_ck == r0 + j, row_sub >= j), colv, scores)
            outs.append(o_inter[h][r0:r0 + sub, :]
                        + jnp.dot(scores.astype(BF16), vb[h], preferred_element_type=F32))
        o = jnp.concatenate(outs, axis=0)
        ms = jnp.sum(o * o, axis=-1, keepdims=True) * (1.0 / C_DV)
        on = o * lax.rsqrt(ms + EPS) * gn_ref[...]
        rr = r_ref[bi[h], rows, vcols[h]].astype(F32)
        o_ref[bi[h], rows, vcols[h]] = (on * (rr / (1.0 + jnp.exp(-rr)))).astype(o_ref.dtype)


def _gla_kernel(q_ref, k_ref, v_ref, r_ref, sm_ref, w2_ref, cb_ref, gn_ref, o_ref,
                st_scr, b_scr, k_scr):
    ck, sub = C_CHUNK, C_SUB

    @pl.when(pl.program_id(1) == 0)
    def _():
        st_scr[...] = jnp.zeros_like(st_scr)

    tri = _tril_ones(ck, BF16)
    row_sub = lax.broadcasted_iota(jnp.int32, (sub, 1), 0)
    row_ck = lax.broadcasted_iota(jnp.int32, (ck, 1), 0)
    col_ck = lax.broadcasted_iota(jnp.int32, (sub, ck), 1)

    def chunk(c, carry):
        rows = pl.ds(pl.multiple_of(c * ck, ck), ck)
        sm_bf = [sm_ref[i, rows, :].astype(BF16) for i in range(sm_ref.shape[0])]
        _gla_chunk(rows, q_ref, k_ref, v_ref, r_ref, sm_bf, w2_ref, cb_ref, gn_ref, o_ref,
                   st_scr, b_scr, k_scr, tri, row_sub, row_ck, col_ck)
        return carry

    lax.fori_loop(0, q_ref.shape[1] // ck, chunk, 0)


def gla_mixer(proj3, small3, w2p, cbp, gnp):
    bsz, s, _ = proj3.shape
    t = GLA_ROWS
    kw, vw = C_HEADS * C_DK_PAD, C_HEADS * C_DV_PAD

    nb = math.gcd(bsz, GLA_BATCH)

    def col(base, width):
        return pl.BlockSpec((nb, t, width), lambda b, i, c=base // width: (b, i, c))

    return pl.pallas_call(
        _gla_kernel, grid=(bsz // nb, s // t),
        in_specs=[col(COL_QC, kw), col(COL_KC, kw), col(COL_VC, vw), col(COL_RC, vw),
                  pl.BlockSpec((nb, t, V7X_LANES), lambda b, i: (b, i, 0)),
                  pl.BlockSpec((C_HEADS, V7X_LANES, C_DK_PAD), lambda b, i: (0, 0, 0)),
                  pl.BlockSpec((C_HEADS, 1, C_DK_PAD), lambda b, i: (0, 0, 0)),
                  pl.BlockSpec((1, C_DV_PAD), lambda b, i: (0, 0))],
        out_specs=pl.BlockSpec((nb, t, vw), lambda b, i: (b, i, 0)),
        out_shape=jax.ShapeDtypeStruct((bsz, s, vw), BF16),
        scratch_shapes=[pltpu.VMEM((nb * C_HEADS, C_DV_PAD, C_DK_PAD), F32),
                        pltpu.VMEM((nb * C_HEADS, C_CHUNK, C_DK_PAD), F32),
                        pltpu.VMEM((nb * C_HEADS, C_CHUNK, C_DK_PAD), F32)],
        compiler_params=_cparams(("parallel", "arbitrary")), name="gla_mixer",
    )(proj3, proj3, proj3, proj3, small3, w2p, cbp, gnp)


MIX_OUT_ROWS = 512


def _mix_out_cross_kernel(x_ref, a_ref, b_ref, c_ref, wa_ref, wb_ref, wc_ref,
                          g_ref, wq_ref, k_ref, v_ref, wo_ref, o_ref):
    acc = jnp.dot(a_ref[...], wa_ref[...], preferred_element_type=F32)
    acc += jnp.dot(b_ref[...], wb_ref[...], preferred_element_type=F32)
    acc += jnp.dot(c_ref[...], wc_ref[...], preferred_element_type=F32)
    x = x_ref[...] + acc
    h = _rms_rows(x, g_ref[...]).astype(BF16)
    q = jnp.dot(h, wq_ref[...], preferred_element_type=F32).astype(BF16)
    cols = [slice(CROSS_DH * hd, CROSS_DH * (hd + 1)) for hd in range(CROSS_HEADS)]
    scores = [lax.dot_general(q[:, cs], k_ref[:, cs], (((1,), (1,)), ((), ())),
                              preferred_element_type=F32) * (CROSS_DH ** -0.5) for cs in cols]
    heads = []
    for cs, logits in zip(cols, scores):
        m = jnp.max(logits, axis=-1, keepdims=True)
        p = jnp.exp(logits - m)
        ssum = jnp.sum(p, axis=-1, keepdims=True)
        pv = jnp.dot(p.astype(BF16), v_ref[:, cs], preferred_element_type=F32)
        heads.append((pv / ssum).astype(BF16))
    o = jnp.concatenate(heads, axis=1)
    o_ref[...] = x + jnp.dot(o, wo_ref[...], preferred_element_type=F32)


def mix_out_cross(x2d, oa, ob, oc, wa, wb, wc, gain, wq, kv, wo, *, seq):
    m, d = x2d.shape
    tm = MIX_OUT_ROWS
    mem_len = kv.shape[1]
    per_batch = seq // tm

    def rows(a):
        return pl.BlockSpec((tm, a.shape[1]), lambda i: (i, 0))

    def resident(a):
        return pl.BlockSpec(a.shape, lambda i: (0,) * a.ndim, pipeline_mode=pl.Buffered(1))

    g2 = gain.reshape(1, d)
    return pl.pallas_call(
        _mix_out_cross_kernel, grid=(m // tm,),
        in_specs=[rows(x2d), rows(oa), rows(ob), rows(oc), resident(wa), resident(wb), resident(wc),
                  resident(g2), resident(wq),
                  pl.BlockSpec((None, mem_len, CROSS_WIDTH), lambda i: (i // per_batch, 0, 0)),
                  pl.BlockSpec((None, mem_len, CROSS_WIDTH), lambda i: (i // per_batch, 0, 1)),
                  resident(wo)],
        out_specs=pl.BlockSpec((tm, d), lambda i: (i, 0)),
        out_shape=jax.ShapeDtypeStruct((m, d), F32),
        compiler_params=_cparams(("parallel",)), name="mix_out_cross",
    )(x2d, oa, ob, oc, wa, wb, wc, g2, wq, kv, kv, wo)


FFN_ROWS = 1024
FFN_CHUNK = FFN_ROWS
FFN_HALO = 16


def _ffn_kernel(x_ref, xh_ref, g_ref, wv_ref, wg_ref, cw_ref, cb_ref, wd_ref, gf_ref, o_ref,
                h_scr, hh_scr, *, final):
    j = pl.program_id(1)

    @pl.when(j == 0)
    def _():
        _norm_to_scratch(x_ref, g_ref, h_scr)
        hh_scr[...] = _rms_rows(xh_ref[...], g_ref[...]).astype(BF16)
        o_ref[...] = x_ref[...]

    rc = FFN_CHUNK
    nchunk = h_scr.shape[0] // rc
    prev = jnp.dot(hh_scr[...], wg_ref[...], preferred_element_type=F32)
    row = lax.broadcasted_iota(jnp.int32, (rc, wg_ref.shape[1]), 0)
    vals, gates = [], []
    for c in range(nchunk):
        h = h_scr[c * rc:(c + 1) * rc, :]
        gates.append(jnp.dot(h, wg_ref[...], preferred_element_type=F32))
        vals.append(jnp.dot(h, wv_ref[...], preferred_element_type=F32))
    for c in range(nchunk):
        rows = slice(c * rc, (c + 1) * rc)
        val, gate = vals[c], gates[c]
        prev1 = prev[prev.shape[0] - 1:, :]
        prev2 = prev[prev.shape[0] - 2:prev.shape[0] - 1, :]
        g1 = jnp.where(row == 0, prev1, pltpu.roll(gate, 1, axis=0))
        g2 = jnp.where(row == 0, prev2, jnp.where(row == 1, prev1, pltpu.roll(gate, 2, axis=0)))
        gc = cb_ref[...] + cw_ref[0:1, :] * g2
        gc = gc + cw_ref[1:2, :] * g1
        gc = gc + cw_ref[2:3, :] * gate
        act = (gc / (1.0 + jnp.exp(-gc)) * val).astype(BF16)
        o_ref[rows, :] += jnp.dot(act, wd_ref[...], preferred_element_type=F32)
        prev = gate

    if final:
        @pl.when(j == pl.num_programs(1) - 1)
        def _():
            def body(r, c):
                sl = pl.ds(pl.multiple_of(r * NORM_ROWS, NORM_ROWS), NORM_ROWS)
                o_ref[sl, :] = _rms_rows(o_ref[sl, :], gf_ref[...])
                return c
            lax.fori_loop(0, o_ref.shape[0] // NORM_ROWS, body, 0)


def conv_ffn(x2d, gain, wv, wg, cw, cb, wd, gain_final, *, seq, final):
    m, d = x2d.shape
    tm, tf = FFN_ROWS, FF_TILE
    nt = m // tm
    tail = x2d.reshape(nt, tm, d)[:, tm - FFN_HALO:, :]
    halo = jnp.concatenate([jnp.zeros_like(tail[:1]), tail[:-1]], axis=0)
    starts = (jnp.arange(nt) % (seq // tm) == 0)[:, None, None]
    halo = jnp.where(starts, 0.0, halo)
    ff = wv.shape[1]
    return pl.pallas_call(
        functools.partial(_ffn_kernel, final=final), grid=(nt, ff // tf),
        in_specs=[pl.BlockSpec((tm, d), lambda i, j: (i, 0)),
                  pl.BlockSpec((None, FFN_HALO, d), lambda i, j: (i, 0, 0)),
                  pl.BlockSpec((1, d), lambda i, j: (0, 0)),
                  pl.BlockSpec((d, tf), lambda i, j: (0, j)),
                  pl.BlockSpec((d, tf), lambda i, j: (0, j)),
                  pl.BlockSpec((CONV_W, tf), lambda i, j: (0, j)),
                  pl.BlockSpec((1, tf), lambda i, j: (0, j)),
                  pl.BlockSpec((tf, d), lambda i, j: (j, 0)),
                  pl.BlockSpec((1, d), lambda i, j: (0, 0))],
        out_specs=pl.BlockSpec((tm, d), lambda i, j: (i, 0)),
        out_shape=jax.ShapeDtypeStruct((m, d), F32),
        scratch_shapes=[pltpu.VMEM((tm, d), BF16), pltpu.VMEM((FFN_HALO, d), BF16)],
        compiler_params=_cparams(("parallel", "arbitrary")), name="conv_ffn",
    )(x2d, halo, gain.reshape(1, d), wv, wg, cw, cb.reshape(1, ff), wd, gain_final.reshape(1, d))


def _pad_heads(w, heads, width, padded):
    lead = w.shape[:-1]
    w = w.reshape(lead + (heads, width))
    w = jnp.pad(w, [(0, 0)] * len(lead) + [(0, 0), (0, padded - width)])
    return w.reshape(lead + (heads * padded,))


def _layer_params(l, w_in, c_gate_w2, c_gate_b, c_norm, w_out, w_up, conv_w, conv_b, w_down):
    splits = np.cumsum([A_WIDTH] * 3 + [B_WIDTH] * 3 + [B_HEADS, C_KW, C_KW, C_VW, C_VW, C_GATE_RANK])
    w_in_l = w_in[l].astype(BF16)
    fl, qc, kc, vc, rc, gl = jnp.split(w_in_l[:, splits[5]:], (splits[6:11] - splits[5]).tolist(), axis=1)
    w_all = jnp.concatenate([
        _pad_heads(vc, C_HEADS, C_DV, C_DV_PAD), _pad_heads(rc, C_HEADS, C_DV, C_DV_PAD),
        _pad_heads(qc, C_HEADS, C_DK, C_DK_PAD), _pad_heads(kc, C_HEADS, C_DK, C_DK_PAD),
        w_in_l[:, :splits[5]]],
        axis=1)
    w_small = jnp.zeros((D_MODEL, V7X_LANES), BF16)
    w_small = w_small.at[:, SMALL_FL:SMALL_FL + B_HEADS].set(fl)
    w_small = w_small.at[:, SMALL_GL:SMALL_GL + C_GATE_RANK].set(gl)
    w2 = _pad_heads(c_gate_w2[l], C_HEADS, C_DK, C_DK_PAD).reshape(C_GATE_RANK, C_HEADS, C_DK_PAD)
    w2p = jnp.zeros((C_HEADS, V7X_LANES, C_DK_PAD), F32)
    w2p = w2p.at[:, SMALL_GL:SMALL_GL + C_GATE_RANK, :].set(jnp.transpose(w2, (1, 0, 2))).astype(BF16)
    cbp = _pad_heads(c_gate_b[l], C_HEADS, C_DK, C_DK_PAD).reshape(C_HEADS, 1, C_DK_PAD)
    gnp = jnp.pad(c_norm[l], (0, C_DV_PAD - C_DV)).reshape(1, C_DV_PAD)
    wo = w_out[l].astype(BF16)
    wa = wo[:A_WIDTH]
    wb = wo[A_WIDTH:A_WIDTH + B_WIDTH]
    wc = wo[A_WIDTH + B_WIDTH:].reshape(C_HEADS, C_DV, D_MODEL)
    wc = jnp.pad(wc, ((0, 0), (0, C_DV_PAD - C_DV), (0, 0))).reshape(C_HEADS * C_DV_PAD, D_MODEL)
    fpad = D_FF_PAD - D_FF
    w_up_l = w_up[l].astype(BF16)
    wv = jnp.pad(w_up_l[:, :D_FF], ((0, 0), (0, fpad)))
    wg = jnp.pad(w_up_l[:, D_FF:], ((0, 0), (0, fpad)))
    cw = jnp.pad(conv_w[l], ((0, 0), (0, fpad)))
    cb = jnp.pad(conv_b[l], (0, fpad))
    wd = jnp.pad(w_down[l].astype(BF16), ((0, fpad), (0, 0)))
    return dict(w_all=w_all, w_small=w_small, w2p=w2p, cbp=cbp, gnp=gnp,
                wa=wa, wb=wb, wc=wc, wv=wv, wg=wg, cw=cw, cb=cb, wd=wd)


def hybrid_mixer(x2d, gain, p, f_bias, bias_a, *, bsz, seq):
    proj, small = in_proj(x2d, gain, p["w_all"], p["w_small"])
    proj3 = proj.reshape(bsz, seq, PROJ_PAD)
    small3 = small.reshape(bsz, seq, V7X_LANES)
    o_a = dilated_mixture(proj3, bias_a).reshape(bsz * seq, A_WIDTH)
    o_b = fox_attention(*fox_prep(proj3, small3, f_bias)).reshape(bsz * seq, B_WIDTH)
    o_c = gla_mixer(proj3, small3, p["w2p"], p["cbp"], p["gnp"]).reshape(bsz * seq, C_HEADS * C_DV_PAD)
    return o_a, o_b, o_c


def kernel(x, mem, rel_table, mem_norm, norm_final, norm_mix, w_in, f_bias, c_gate_w2, c_gate_b,
           c_norm, w_out, norm_cross, w_cq, w_ckv, w_co, norm_ffn, w_up, conv_w, conv_b, w_down):
    bsz, seq, d = x.shape
    depth = w_in.shape[0]
    mem_len = mem.shape[1]
    assert d == D_MODEL and seq % (A_SPAN * A_PATTERNS[-1][1]) == 0 and seq % 1024 == 0
    bias_a = jnp.stack([_dilated_bias(rel_table, dil) for _, dil in A_PATTERNS], axis=0)
    x2d = x.reshape(bsz * seq, d)
    mem2d = mem.reshape(bsz * mem_len, d)
    for l in range(depth):
        p = _layer_params(l, w_in, c_gate_w2, c_gate_b, c_norm, w_out, w_up, conv_w, conv_b, w_down)
        o_a, o_b, o_c = hybrid_mixer(x2d, norm_mix[l], p, f_bias[l], bias_a, bsz=bsz, seq=seq)
        kv = norm_matmul(mem2d, mem_norm, w_ckv[l].astype(BF16), tm=min(512, bsz * mem_len), tn=512,
                         name="mem_kv").reshape(bsz, mem_len, 2 * CROSS_WIDTH)
        x2d = mix_out_cross(x2d, o_a, o_b, o_c, p["wa"], p["wb"], p["wc"], norm_cross[l],
                            w_cq[l].astype(BF16), kv, w_co[l].astype(BF16), seq=seq)
        x2d = conv_ffn(x2d, norm_ffn[l], p["wv"], p["wg"], p["cw"], p["cb"], p["wd"], norm_final,
                       seq=seq, final=l == depth - 1)
    return x2d.reshape(bsz, seq, d)
```

```python
import functools
import math

import numpy as np
import jax
import jax.numpy as jnp
from jax import lax
from jax.experimental import pallas as pl
from jax.experimental.pallas import tpu as pltpu

F32 = jnp.float32
BF16 = jnp.bfloat16

V7X_LANES = 128
V7X_VMEM_LIMIT_BYTES = 56 * 1024 * 1024

D_MODEL = 2048
HEAD_DIM = 64
A_WIDTH = 3 * D_MODEL // 8
A_HEADS = A_WIDTH // HEAD_DIM
A_PATTERNS = ((128, 1), (512, 4), (2048, 16))
A_SPAN = 128
B_WIDTH = D_MODEL // 4
B_HEADS = B_WIDTH // HEAD_DIM
C_HEADS = 4
C_VW = 3 * D_MODEL // 8
C_DV = C_VW // C_HEADS
C_DK = C_DV // 2
C_KW = C_HEADS * C_DK
C_GATE_RANK = 16
C_GATE_TAU = 16.0
C_CHUNK = 64
C_SUB = 8
C_DK_PAD = 128
C_DV_PAD = 256
CROSS_HEADS = 4
CROSS_DH = 128
CROSS_WIDTH = CROSS_HEADS * CROSS_DH
D_FF = ((8 * D_MODEL // 3 + 127) // 128) * 128
FF_TILE = 512
D_FF_PAD = ((D_FF + FF_TILE - 1) // FF_TILE) * FF_TILE
CONV_W = 3
REL_BUCKETS = 32
REL_MAX_DIST = 2048
EPS = 1e-6
NEG = -1e30
LOG2E = 1.4426950408889634

COL_VC = 0
COL_RC = COL_VC + C_HEADS * C_DV_PAD
COL_QC = COL_RC + C_HEADS * C_DV_PAD
COL_KC = COL_QC + C_HEADS * C_DK_PAD
COL_QA = COL_KC + C_HEADS * C_DK_PAD
COL_KA = COL_QA + A_WIDTH
COL_VA = COL_KA + A_WIDTH
COL_QB = COL_VA + A_WIDTH
COL_KB = COL_QB + B_WIDTH
COL_VB = COL_KB + B_WIDTH
PROJ_PAD = COL_VB + B_WIDTH
SMALL_FL = 0
SMALL_GL = B_HEADS


def _cparams(sem):
    return pltpu.CompilerParams(dimension_semantics=sem,
                                vmem_limit_bytes=V7X_VMEM_LIMIT_BYTES)


def _rms_rows(x, g):
    ms = jnp.mean(x * x, axis=-1, keepdims=True)
    return x * lax.rsqrt(ms + EPS) * g


def _split3(x):
    hi = x.astype(BF16)
    r1 = x - hi.astype(F32)
    mid = r1.astype(BF16)
    lo = (r1 - mid.astype(F32)).astype(BF16)
    return hi, mid, lo


def _log_sigmoid(x):
    return jnp.minimum(x, 0.0) - jnp.log(1.0 + jnp.exp(-jnp.abs(x)))


def _tril_ones(n, dtype):
    r = lax.broadcasted_iota(jnp.int32, (n, n), 0)
    c = lax.broadcasted_iota(jnp.int32, (n, n), 1)
    return jnp.where(r >= c, 1.0, 0.0).astype(dtype)


NORM_ROWS = 256


def _norm_to_scratch(x_ref, g_ref, h_scr):
    def body(r, c):
        sl = pl.ds(pl.multiple_of(r * NORM_ROWS, NORM_ROWS), NORM_ROWS)
        h_scr[sl, :] = _rms_rows(x_ref[sl, :], g_ref[...]).astype(BF16)
        return c
    lax.fori_loop(0, x_ref.shape[0] // NORM_ROWS, body, 0)


def _norm_matmul_kernel(x_ref, g_ref, w_ref, o_ref, h_scr):
    @pl.when(pl.program_id(1) == 0)
    def _():
        _norm_to_scratch(x_ref, g_ref, h_scr)
    o_ref[...] = jnp.dot(h_scr[...], w_ref[...],
                         preferred_element_type=F32).astype(o_ref.dtype)


def norm_matmul(x2d, gain, w, *, tm, tn, name):
    m, k = x2d.shape
    n = w.shape[1]
    return pl.pallas_call(
        _norm_matmul_kernel, grid=(m // tm, n // tn),
        in_specs=[pl.BlockSpec((tm, k), lambda i, j: (i, 0)),
                  pl.BlockSpec((1, k), lambda i, j: (0, 0)),
                  pl.BlockSpec((k, tn), lambda i, j: (0, j))],
        out_specs=pl.BlockSpec((tm, tn), lambda i, j: (i, j)),
        out_shape=jax.ShapeDtypeStruct((m, n), BF16),
        scratch_shapes=[pltpu.VMEM((tm, k), BF16)],
        compiler_params=_cparams(("parallel", "arbitrary")), name=name)(x2d, gain.reshape(1, k), w)


IN_PROJ_ROWS = 1024
IN_PROJ_TILE = 3 * A_WIDTH


def _in_proj_kernel(x_ref, g_ref, w_ref, ws_ref, o_ref, os_ref, h_scr):
    @pl.when(pl.program_id(1) == 0)
    def _():
        _norm_to_scratch(x_ref, g_ref, h_scr)
        os_ref[...] = jnp.dot(h_scr[...], ws_ref[...], preferred_element_type=F32)

    o_ref[...] = jnp.dot(h_scr[...], w_ref[...], preferred_element_type=F32).astype(o_ref.dtype)


def in_proj(x2d, gain, w_all, w_small):
    m, k = x2d.shape
    tm, tn = IN_PROJ_ROWS, IN_PROJ_TILE
    ns = w_small.shape[1]
    return pl.pallas_call(
        _in_proj_kernel, grid=(m // tm, PROJ_PAD // tn),
        in_specs=[pl.BlockSpec((tm, k), lambda i, j: (i, 0)),
                  pl.BlockSpec((1, k), lambda i, j: (0, 0)),
                  pl.BlockSpec((k, tn), lambda i, j: (0, j)),
                  pl.BlockSpec((k, ns), lambda i, j: (0, 0))],
        out_specs=[pl.BlockSpec((tm, tn), lambda i, j: (i, j)),
                   pl.BlockSpec((tm, ns), lambda i, j: (i, 0))],
        out_shape=[jax.ShapeDtypeStruct((m, PROJ_PAD), BF16),
                   jax.ShapeDtypeStruct((m, ns), F32)],
        scratch_shapes=[pltpu.VMEM((tm, k), BF16)],
        compiler_params=_cparams(("parallel", "arbitrary")), name="in_proj",
    )(x2d, gain.reshape(1, k), w_all, w_small)


def _t5_bucket(dist):
    max_exact = REL_BUCKETS // 2
    d = np.maximum(dist, 1).astype(np.float32)
    large = max_exact + (np.log(d / max_exact) / math.log(REL_MAX_DIST / max_exact)
                         * (REL_BUCKETS - max_exact)).astype(np.int32)
    return np.where(dist < max_exact, dist, np.minimum(large, REL_BUCKETS - 1)).astype(np.int32)


def _dilated_bias(rel_table, dilation):
    span = A_SPAN
    steps = np.arange(2 * span, -span, -1)
    valid = (steps >= 0) & (steps <= span)
    per_step = rel_table[_t5_bucket(np.clip(steps, 0, span) * dilation)].astype(F32)
    per_step = jnp.where(valid[:, None], per_step * LOG2E, NEG).T
    later = jnp.stack([per_step[:, 2 * span - i - span: 4 * span - i - span] for i in range(span)],
                      axis=1)
    first = jnp.where((np.arange(2 * span) >= span)[None, None, :], later, NEG)
    return jnp.stack([first, later], axis=0)


DIL_TILE = A_SPAN * max(d for _, d in A_PATTERNS)
DIL_UNITS = DIL_TILE // A_SPAN
DIL_GROUP = 4
DIL_COMBINE_ROWS = 256


def _dilated_kernel(qb_ref, kp_ref, kc_ref, vp_ref, vc_ref, bias_ref, o_ref,
                    q_ref, kk_scr, vv_scr, m_scr, n_scr, s_scr):
    tile, span = DIL_TILE, A_SPAN
    first_tile = pl.program_id(1) == 0
    q_ref[...] = qb_ref[...].astype(F32)
    kk_scr[0:tile, :] = kp_ref[...].astype(F32)
    kk_scr[tile:2 * tile, :] = kc_ref[...].astype(F32)
    vv_scr[0:tile, :] = vp_ref[...].astype(F32)
    vv_scr[tile:2 * tile, :] = vc_ref[...].astype(F32)
    lane_q = lax.broadcasted_iota(jnp.int32, (span, V7X_LANES), 1)
    lane_k = lax.broadcasted_iota(jnp.int32, (2 * span, V7X_LANES), 1)
    low_q = lane_q < HEAD_DIM
    low_k = lane_k < HEAD_DIM
    scale = HEAD_DIM ** -0.5 * LOG2E
    qmask = (jnp.where(low_q, scale, 0.0), jnp.where(low_q, 0.0, scale))
    nt = (((1,), (1,)), ((), ()))

    for pat, (_, d) in enumerate(A_PATTERNS):
        for u0 in range(0, DIL_UNITS, DIL_GROUP):
            units = []
            for u in range(u0, u0 + DIL_GROUP):
                n, r = divmod(u, d)
                q0 = r + span * d * n
                k0 = tile + q0 - span * d
                if d == 1:
                    qrows, krows = pl.ds(q0, span), pl.ds(k0, 2 * span)
                else:
                    qrows, krows = pl.ds(q0, span, stride=d), pl.ds(k0, 2 * span, stride=d)
                qf = q_ref[qrows, :]
                kb = kk_scr[krows, :].astype(BF16)
                sel = jnp.where(first_tile, 0, 1) if n == 0 else 1
                logits = [lax.dot_general((qf * qmask[e]).astype(BF16), kb, nt,
                                          preferred_element_type=F32) + bias_ref[pat, sel, e]
                          for e in range(2)]
                units.append((qrows, krows, logits))
            results = []
            for qrows, krows, logits in units:
                vf = vv_scr[krows, :]
                pvs, ms = [], []
                for e in range(2):
                    m = jnp.max(logits[e], axis=-1, keepdims=True)
                    p = jnp.exp2(logits[e] - m)
                    ve = (jnp.where(low_k, vf, 1.0) if e == 0 else jnp.where(low_k, 1.0, vf)).astype(BF16)
                    pvs.append(jnp.dot(p.astype(BF16), ve, preferred_element_type=F32))
                    ms.append(m)
                results.append((qrows, pvs, ms))
            for qrows, pvs, ms in results:
                m_scr[pat, qrows, :] = jnp.where(low_q, ms[0], ms[1])
                n_scr[pat, qrows, :] = jnp.where(low_q, pvs[0], pvs[1])
                s_scr[pat, qrows, :] = pltpu.roll(jnp.where(low_q, pvs[1], pvs[0]), HEAD_DIM, axis=1)

    def combine(c, carry):
        rows = pl.ds(pl.multiple_of(c * DIL_COMBINE_ROWS, DIL_COMBINE_ROWS), DIL_COMBINE_ROWS)
        ms = [m_scr[pat, rows, :] for pat in range(len(A_PATTERNS))]
        top = functools.reduce(jnp.maximum, ms)
        ws = [jnp.exp2(m - top) for m in ms]
        num = sum(w * n_scr[pat, rows, :] for pat, w in enumerate(ws))
        den = sum(w * s_scr[pat, rows, :] for pat, w in enumerate(ws))
        o_ref[rows, :] = (num / den).astype(o_ref.dtype)
        return carry

    lax.fori_loop(0, tile // DIL_COMBINE_ROWS, combine, 0)


def dilated_mixture(proj3, bias):
    bsz, s, _ = proj3.shape
    pairs = A_HEADS // 2
    blk = (None, DIL_TILE, V7X_LANES)
    cq, ck, cv = COL_QA // V7X_LANES, COL_KA // V7X_LANES, COL_VA // V7X_LANES

    def cur(col):
        return pl.BlockSpec(blk, lambda b, t, g: (b, t, col + g))

    def prev(col):
        return pl.BlockSpec(blk, lambda b, t, g: (b, jnp.maximum(t - 1, 0), col + g))

    tile_f32 = pltpu.VMEM((DIL_TILE, V7X_LANES), F32)
    both_f32 = pltpu.VMEM((2 * DIL_TILE, V7X_LANES), F32)
    stats_f32 = pltpu.VMEM((len(A_PATTERNS), DIL_TILE, V7X_LANES), F32)
    return pl.pallas_call(
        _dilated_kernel, grid=(bsz, s // DIL_TILE, pairs),
        in_specs=[cur(cq), prev(ck), cur(ck), prev(cv), cur(cv),
                  pl.BlockSpec((len(A_PATTERNS), 2, 2, A_SPAN, 2 * A_SPAN),
                               lambda b, t, g: (0, 0, g, 0, 0))],
        out_specs=pl.BlockSpec(blk, lambda b, t, g: (b, t, g)),
        out_shape=jax.ShapeDtypeStruct((bsz, s, A_WIDTH), BF16),
        scratch_shapes=[tile_f32, both_f32, both_f32, stats_f32, stats_f32, stats_f32],
        compiler_params=_cparams(("parallel", "parallel", "arbitrary")), name="dilated_mixture",
    )(proj3, proj3, proj3, proj3, proj3, bias)


FOX_PREP_ROWS = 512
FOX_TQ = 512
FOX_TK = 512
FOX_HEADS_PER_STEP = 4


def _fox_prep_kernel(sm_ref, fb_ref, selq_ref, selk_ref, q0_ref, q1_ref, k0_ref, k1_ref, v0_ref, v1_ref,
                     qa_ref, ka_ref, va_ref, carry_scr):
    t = sm_ref.shape[0]

    @pl.when(pl.program_id(1) == 0)
    def _():
        carry_scr[...] = jnp.zeros_like(carry_scr)

    x = _log_sigmoid(sm_ref[...] + fb_ref[...])
    tri = _tril_ones(t, BF16)
    hi, mid, lo = _split3(x)
    c = (jnp.dot(tri, hi, preferred_element_type=F32)
         + jnp.dot(tri, mid, preferred_element_type=F32)
         + jnp.dot(tri, lo, preferred_element_type=F32)) + carry_scr[0:1, :]
    carry_scr[...] = jnp.broadcast_to(c[t - 1:t, :], carry_scr.shape)
    chi, cmid, clo = [p.astype(F32) for p in _split3(c * LOG2E)]

    lane = lax.broadcasted_iota(jnp.int32, (t, V7X_LANES), 1)
    low = lane < HEAD_DIM
    scale = HEAD_DIM ** -0.5 * LOG2E
    src = jnp.where(lane < B_HEADS, chi,
                    jnp.where(lane < 2 * B_HEADS, pltpu.roll(cmid, B_HEADS, axis=1),
                              jnp.where(lane < 3 * B_HEADS, pltpu.roll(clo, 2 * B_HEADS, axis=1),
                                        jnp.where(lane == 3 * B_HEADS, 1.0, 0.0)))).astype(BF16)
    q_refs, k_refs, v_refs = (q0_ref, q1_ref), (k0_ref, k1_ref), (v0_ref, v1_ref)
    for h in range(B_HEADS):
        src_ref, pair = divmod(h // 2, 2)
        cs = slice(V7X_LANES * pair, V7X_LANES * (pair + 1))
        qp = q_refs[src_ref][:, cs].astype(F32)
        kp = k_refs[src_ref][:, cs].astype(F32)
        vp = v_refs[src_ref][:, cs].astype(F32)
        if h % 2 == 1:
            qp = pltpu.roll(qp, HEAD_DIM, axis=1)
            kp = pltpu.roll(kp, HEAD_DIM, axis=1)
        q_aug = jnp.dot(src, selq_ref[h], preferred_element_type=F32)
        k_aug = jnp.dot(src, selk_ref[h], preferred_element_type=F32)
        qa_ref[h] = jnp.where(low, qp * scale, q_aug).T.astype(BF16)
        ka_ref[h] = jnp.where(low, kp, k_aug).astype(BF16)
        v_aug = jnp.where(low, vp, 1.0) if h % 2 == 0 else jnp.where(low, 1.0, vp)
        va_ref[h, 0] = v_aug.T.astype(BF16)


def _fox_selectors():
    selq = np.zeros((B_HEADS, V7X_LANES, V7X_LANES), np.float32)
    selk = np.zeros((B_HEADS, V7X_LANES, V7X_LANES), np.float32)
    one = 3 * B_HEADS
    for h in range(B_HEADS):
        for part in range(3):
            selq[h, part * B_HEADS + h, HEAD_DIM + part] = 1.0
            selq[h, one, HEAD_DIM + 3 + part] = 1.0
            selk[h, one, HEAD_DIM + part] = 1.0
            selk[h, part * B_HEADS + h, HEAD_DIM + 3 + part] = -1.0
    return jnp.asarray(selq, BF16), jnp.asarray(selk, BF16)


def fox_prep(proj3, small3, f_bias):
    bsz, s, _ = proj3.shape
    t = FOX_PREP_ROWS
    half = B_WIDTH // 2
    fb = jnp.zeros((1, V7X_LANES), F32).at[0, SMALL_FL:SMALL_FL + B_HEADS].set(f_bias)

    def colspec(col):
        return pl.BlockSpec((None, t, half), lambda b, i, c=col // half: (b, i, c))

    assert t == FOX_TK
    selq, selk = _fox_selectors()
    sel_spec = pl.BlockSpec(selq.shape, lambda b, i: (0, 0, 0))
    k_spec = pl.BlockSpec((None, B_HEADS, t, V7X_LANES), lambda b, i: (b, 0, i, 0))
    k_shape = jax.ShapeDtypeStruct((bsz, B_HEADS, s, V7X_LANES), BF16)
    q_spec = pl.BlockSpec((None, B_HEADS, V7X_LANES, t), lambda b, i: (b, 0, 0, i))
    q_shape = jax.ShapeDtypeStruct((bsz, B_HEADS, V7X_LANES, s), BF16)
    v_spec = pl.BlockSpec((None, B_HEADS, 1, V7X_LANES, t), lambda b, i: (b, 0, i, 0, 0))
    v_shape = jax.ShapeDtypeStruct((bsz, B_HEADS, s // t, V7X_LANES, t), BF16)
    return pl.pallas_call(
        _fox_prep_kernel, grid=(bsz, s // t),
        in_specs=[pl.BlockSpec((None, t, V7X_LANES), lambda b, i: (b, i, 0)),
                  pl.BlockSpec((1, V7X_LANES), lambda b, i: (0, 0)),
                  sel_spec, sel_spec,
                  colspec(COL_QB), colspec(COL_QB + half),
                  colspec(COL_KB), colspec(COL_KB + half),
                  colspec(COL_VB), colspec(COL_VB + half)],
        out_specs=[q_spec, k_spec, v_spec],
        out_shape=[q_shape, k_shape, v_shape],
        scratch_shapes=[pltpu.VMEM((8, V7X_LANES), F32)],
        compiler_params=_cparams(("parallel", "arbitrary")), name="fox_prep",
    )(small3, fb, selq, selk, proj3, proj3, proj3, proj3, proj3, proj3)


def _fox_kernel(q_ref, k_ref, v_ref, o_ref, m_scr, acc_scr):
    tq, tk = FOX_TQ, FOX_TK
    qi = pl.program_id(2)
    nfull = qi * (tq // tk)
    krow = lax.broadcasted_iota(jnp.int32, (tk, tq), 0)
    qcol = lax.broadcasted_iota(jnp.int32, (tk, tq), 1)
    frow = lax.broadcasted_iota(jnp.int32, (V7X_LANES, tq), 0)
    nh = q_ref.shape[0]
    qs = [q_ref[e] for e in range(nh)]
    m_scr[...] = jnp.full(m_scr.shape, NEG, F32)
    acc_scr[...] = jnp.zeros(acc_scr.shape, F32)

    def heads(kt, mask):
        rows = pl.ds(pl.multiple_of(kt * tk, tk), tk)
        scores = [jnp.dot(k_ref[e, rows, :], qs[e], preferred_element_type=F32) for e in range(nh)]
        for e in range(nh):
            m = m_scr[e, 0:1, :]
            s = scores[e] if mask is None else jnp.where(mask, scores[e], NEG)
            m_new = jnp.maximum(m, jnp.max(s, axis=0, keepdims=True))
            alpha = jnp.exp2(m - m_new)
            p = jnp.exp2(s - m_new)
            pv = jnp.dot(v_ref[e, kt], p.astype(BF16), preferred_element_type=F32)
            acc_scr[e] = alpha * acc_scr[e] + pv
            m_scr[e] = jnp.broadcast_to(m_new, (8, tq))

    def body(kt, c):
        heads(kt, None)
        return c

    lax.fori_loop(0, nfull, body, 0)
    for dd in range(tq // tk):
        heads(nfull + dd, krow + dd * tk <= qcol)
    for g in range(nh // 2):
        even, odd = acc_scr[2 * g], acc_scr[2 * g + 1]
        res = [acc / pltpu.roll(acc, HEAD_DIM, axis=0) for acc in (even, odd)]
        pair = jnp.where(frow < HEAD_DIM, res[0], res[1])
        o_ref[:, V7X_LANES * g:V7X_LANES * (g + 1)] = pair.T.astype(o_ref.dtype)


def fox_attention(q_aug, k_aug, v_aug):
    bsz, nh, s, _ = k_aug.shape
    tq, tk, hs = FOX_TQ, FOX_TK, FOX_HEADS_PER_STEP
    return pl.pallas_call(
        _fox_kernel, grid=(bsz, nh // hs, s // tq),
        in_specs=[pl.BlockSpec((None, hs, V7X_LANES, tq), lambda b, g, i: (b, g, 0, i)),
                  pl.BlockSpec((None, hs, s, V7X_LANES), lambda b, g, i: (b, g, 0, 0)),
                  pl.BlockSpec((None, hs, s // tk, V7X_LANES, tk), lambda b, g, i: (b, g, 0, 0, 0))],
        out_specs=pl.BlockSpec((None, tq, hs * HEAD_DIM), lambda b, g, i: (b, i, g)),
        out_shape=jax.ShapeDtypeStruct((bsz, s, B_WIDTH), BF16),
        scratch_shapes=[pltpu.VMEM((hs, 8, tq), F32), pltpu.VMEM((hs, V7X_LANES, tq), F32)],
        compiler_params=_cparams(("parallel", "parallel", "arbitrary")), name="fox_attention",
    )(q_aug, k_aug, v_aug)


GLA_ROWS = 512
GLA_BATCH = 4


def _gla_chunk(rows, q_ref, k_ref, v_ref, r_ref, sm_bf, w2_ref, cb_ref, gn_ref, o_ref,
               st_scr, b_scr, k_scr, tri, row_sub, row_ck, col_ck):
    ck, sub = C_CHUNK, C_SUB
    nblk = ck // sub
    heads = range(q_ref.shape[0] * C_HEADS)
    bi = [h // C_HEADS for h in heads]
    hh = [h % C_HEADS for h in heads]
    nt = (((1,), (1,)), ((), ()))
    kcols = [slice(C_DK_PAD * hh[h], C_DK_PAD * (hh[h] + 1)) for h in heads]
    vcols = [slice(C_DV_PAD * hh[h], C_DV_PAD * (hh[h] + 1)) for h in heads]
    g = [jnp.dot(sm_bf[bi[h]], w2_ref[hh[h]], preferred_element_type=F32) + cb_ref[hh[h]] for h in heads]
    parts = [_split3(_log_sigmoid(g[h]) / C_GATE_TAU) for h in heads]
    b = [(jnp.dot(tri, parts[h][0], preferred_element_type=F32)
          + jnp.dot(tri, parts[h][1], preferred_element_type=F32)
          + jnp.dot(tri, parts[h][2], preferred_element_type=F32)) for h in heads]
    qf = [q_ref[bi[h], rows, kcols[h]].astype(F32) * (C_DK ** -0.5) for h in heads]
    kf = [k_ref[bi[h], rows, kcols[h]].astype(F32) for h in heads]
    vb = [v_ref[bi[h], rows, vcols[h]] for h in heads]
    st = [st_scr[h] for h in heads]
    for h in heads:
        b_scr[h] = b[h]
        k_scr[h] = kf[h]
    o_inter = [lax.dot_general((qf[h] * jnp.exp(b[h])).astype(BF16), st[h].astype(BF16), nt,
                               preferred_element_type=F32) for h in heads]
    a = {}
    for h in heads:
        for blk in range(1, nblk):
            r0 = blk * sub
            ref_row = b_scr[h, r0 - 1:r0, :]
            kd = jnp.where(row_ck < r0, kf[h] * jnp.exp(jnp.minimum(ref_row - b[h], 0.0)), 0.0)
            qd = qf[h][r0:r0 + sub, :] * jnp.exp(b[h][r0:r0 + sub, :] - ref_row)
            a[h, blk] = lax.dot_general(qd.astype(BF16), kd.astype(BF16), nt, preferred_element_type=F32)
    kdec = [(kf[h] * jnp.exp(b_scr[h, ck - 1:ck, :] - b[h])).astype(BF16) for h in heads]
    upd = [lax.dot_general(vb[h], kdec[h], (((0,), (0,)), ((), ())), preferred_element_type=F32)
           for h in heads]
    for h in heads:
        st_scr[h] = st[h] * jnp.exp(b_scr[h, ck - 1:ck, :]) + upd[h]
    for h in heads:
        b2 = b[h] * LOG2E
        outs = []
        for blk in range(nblk):
            r0 = blk * sub
            qb = qf[h][r0:r0 + sub, :]
            bb = b2[r0:r0 + sub, :]
            scores = a[h, blk] if blk > 0 else jnp.zeros((sub, ck), F32)
            for j in range(sub):
                bj = b_scr[h, r0 + j:r0 + j + 1, :] * LOG2E
                kj = k_scr[h, r0 + j:r0 + j + 1, :]
                x = qb * kj * jnp.exp2(bb - bj)
                colv = jnp.sum(x, axis=-1, keepdims=True)
                scores = jnp.where(jnp.logical_and(col_ck == r0 + j, row_sub >= j), colv, scores)
            outs.append(o_inter[h][r0:r0 + sub, :]
                        + jnp.dot(scores.astype(BF16), vb[h], preferred_element_type=F32))
        o = jnp.concatenate(outs, axis=0)
        ms = jnp.sum(o * o, axis=-1, keepdims=True) * (1.0 / C_DV)
        on = o * lax.rsqrt(ms + EPS) * gn_ref[...]
        rr = r_ref[bi[h], rows, vcols[h]].astype(F32)
        o_ref[bi[h], rows, vcols[h]] = (on * (rr / (1.0 + jnp.exp(-rr)))).astype(o_ref.dtype)


def _gla_kernel(q_ref, k_ref, v_ref, r_ref, sm_ref, w2_ref, cb_ref, gn_ref, o_ref,
                st_scr, b_scr, k_scr):
    ck, sub = C_CHUNK, C_SUB

    @pl.when(pl.program_id(1) == 0)
    def _():
        st_scr[...] = jnp.zeros_like(st_scr)

    tri = _tril_ones(ck, BF16)
    row_sub = lax.broadcasted_iota(jnp.int32, (sub, 1), 0)
    row_ck = lax.broadcasted_iota(jnp.int32, (ck, 1), 0)
    col_ck = lax.broadcasted_iota(jnp.int32, (sub, ck), 1)

    def chunk(c, carry):
        rows = pl.ds(pl.multiple_of(c * ck, ck), ck)
        sm_bf = [sm_ref[i, rows, :].astype(BF16) for i in range(sm_ref.shape[0])]
        _gla_chunk(rows, q_ref, k_ref, v_ref, r_ref, sm_bf, w2_ref, cb_ref, gn_ref, o_ref,
                   st_scr, b_scr, k_scr, tri, row_sub, row_ck, col_ck)
        return carry

    lax.fori_loop(0, q_ref.shape[1] // ck, chunk, 0)


def gla_mixer(proj3, small3, w2p, cbp, gnp):
    bsz, s, _ = proj3.shape
    t = GLA_ROWS
    kw, vw = C_HEADS * C_DK_PAD, C_HEADS * C_DV_PAD

    nb = math.gcd(bsz, GLA_BATCH)

    def col(base, width):
        return pl.BlockSpec((nb, t, width), lambda b, i, c=base // width: (b, i, c))

    return pl.pallas_call(
        _gla_kernel, grid=(bsz // nb, s // t),
        in_specs=[col(COL_QC, kw), col(COL_KC, kw), col(COL_VC, vw), col(COL_RC, vw),
                  pl.BlockSpec((nb, t, V7X_LANES), lambda b, i: (b, i, 0)),
                  pl.BlockSpec((C_HEADS, V7X_LANES, C_DK_PAD), lambda b, i: (0, 0, 0)),
                  pl.BlockSpec((C_HEADS, 1, C_DK_PAD), lambda b, i: (0, 0, 0)),
                  pl.BlockSpec((1, C_DV_PAD), lambda b, i: (0, 0))],
        out_specs=pl.BlockSpec((nb, t, vw), lambda b, i: (b, i, 0)),
        out_shape=jax.ShapeDtypeStruct((bsz, s, vw), BF16),
        scratch_shapes=[pltpu.VMEM((nb * C_HEADS, C_DV_PAD, C_DK_PAD), F32),
                        pltpu.VMEM((nb * C_HEADS, C_CHUNK, C_DK_PAD), F32),
                        pltpu.VMEM((nb * C_HEADS, C_CHUNK, C_DK_PAD), F32)],
        compiler_params=_cparams(("parallel", "arbitrary")), name="gla_mixer",
    )(proj3, proj3, proj3, proj3, small3, w2p, cbp, gnp)


MIX_OUT_ROWS = 512


def _mix_out_cross_kernel(x_ref, a_ref, b_ref, c_ref, wa_ref, wb_ref, wc_ref,
                          g_ref, wq_ref, k_ref, v_ref, wo_ref, o_ref):
    acc = jnp.dot(a_ref[...], wa_ref[...], preferred_element_type=F32)
    acc += jnp.dot(b_ref[...], wb_ref[...], preferred_element_type=F32)
    acc += jnp.dot(c_ref[...], wc_ref[...], preferred_element_type=F32)
    x = x_ref[...] + acc
    h = _rms_rows(x, g_ref[...]).astype(BF16)
    q = jnp.dot(h, wq_ref[...], preferred_element_type=F32).astype(BF16)
    cols = [slice(CROSS_DH * hd, CROSS_DH * (hd + 1)) for hd in range(CROSS_HEADS)]
    scores = [lax.dot_general(q[:, cs], k_ref[:, cs], (((1,), (1,)), ((), ())),
                              preferred_element_type=F32) * (CROSS_DH ** -0.5) for cs in cols]
    heads = []
    for cs, logits in zip(cols, scores):
        m = jnp.max(logits, axis=-1, keepdims=True)
        p = jnp.exp(logits - m)
        ssum = jnp.sum(p, axis=-1, keepdims=True)
        pv = jnp.dot(p.astype(BF16), v_ref[:, cs], preferred_element_type=F32)
        heads.append((pv / ssum).astype(BF16))
    o = jnp.concatenate(heads, axis=1)
    o_ref[...] = x + jnp.dot(o, wo_ref[...], preferred_element_type=F32)


def mix_out_cross(x2d, oa, ob, oc, wa, wb, wc, gain, wq, kv, wo, *, seq):
    m, d = x2d.shape
    tm = MIX_OUT_ROWS
    mem_len = kv.shape[1]
    per_batch = seq // tm

    def rows(a):
        return pl.BlockSpec((tm, a.shape[1]), lambda i: (i, 0))

    def resident(a):
        return pl.BlockSpec(a.shape, lambda i: (0,) * a.ndim, pipeline_mode=pl.Buffered(1))

    g2 = gain.reshape(1, d)
    return pl.pallas_call(
        _mix_out_cross_kernel, grid=(m // tm,),
        in_specs=[rows(x2d), rows(oa), rows(ob), rows(oc), resident(wa), resident(wb), resident(wc),
                  resident(g2), resident(wq),
                  pl.BlockSpec((None, mem_len, CROSS_WIDTH), lambda i: (i // per_batch, 0, 0)),
                  pl.BlockSpec((None, mem_len, CROSS_WIDTH), lambda i: (i // per_batch, 0, 1)),
                  resident(wo)],
        out_specs=pl.BlockSpec((tm, d), lambda i: (i, 0)),
        out_shape=jax.ShapeDtypeStruct((m, d), F32),
        compiler_params=_cparams(("parallel",)), name="mix_out_cross",
    )(x2d, oa, ob, oc, wa, wb, wc, g2, wq, kv, kv, wo)


FFN_ROWS = 1024
FFN_CHUNK = FFN_ROWS
FFN_HALO = 16


def _ffn_kernel(x_ref, xh_ref, g_ref, wv_ref, wg_ref, cw_ref, cb_ref, wd_ref, gf_ref, o_ref,
                h_scr, hh_scr, *, final):
    j = pl.program_id(1)

    @pl.when(j == 0)
    def _():
        _norm_to_scratch(x_ref, g_ref, h_scr)
        hh_scr[...] = _rms_rows(xh_ref[...], g_ref[...]).astype(BF16)
        o_ref[...] = x_ref[...]

    rc = FFN_CHUNK
    nchunk = h_scr.shape[0] // rc
    prev = jnp.dot(hh_scr[...], wg_ref[...], preferred_element_type=F32)
    row = lax.broadcasted_iota(jnp.int32, (rc, wg_ref.shape[1]), 0)
    vals, gates = [], []
    for c in range(nchunk):
        h = h_scr[c * rc:(c + 1) * rc, :]
        gates.append(jnp.dot(h, wg_ref[...], preferred_element_type=F32))
        vals.append(jnp.dot(h, wv_ref[...], preferred_element_type=F32))
    for c in range(nchunk):
        rows = slice(c * rc, (c + 1) * rc)
        val, gate = vals[c], gates[c]
        prev1 = prev[prev.shape[0] - 1:, :]
        prev2 = prev[prev.shape[0] - 2:prev.shape[0] - 1, :]
        g1 = jnp.where(row == 0, prev1, pltpu.roll(gate, 1, axis=0))
        g2 = jnp.where(row == 0, prev2, jnp.where(row == 1, prev1, pltpu.roll(gate, 2, axis=0)))
        gc = cb_ref[...] + cw_ref[0:1, :] * g2
        gc = gc + cw_ref[1:2, :] * g1
        gc = gc + cw_ref[2:3, :] * gate
        act = (gc / (1.0 + jnp.exp(-gc)) * val).astype(BF16)
        o_ref[rows, :] += jnp.dot(act, wd_ref[...], preferred_element_type=F32)
        prev = gate

    if final:
        @pl.when(j == pl.num_programs(1) - 1)
        def _():
            def body(r, c):
                sl = pl.ds(pl.multiple_of(r * NORM_ROWS, NORM_ROWS), NORM_ROWS)
                o_ref[sl, :] = _rms_rows(o_ref[sl, :], gf_ref[...])
                return c
            lax.fori_loop(0, o_ref.shape[0] // NORM_ROWS, body, 0)


def conv_ffn(x2d, gain, wv, wg, cw, cb, wd, gain_final, *, seq, final):
    m, d = x2d.shape
    tm, tf = FFN_ROWS, FF_TILE
    nt = m // tm
    tail = x2d.reshape(nt, tm, d)[:, tm - FFN_HALO:, :]
    halo = jnp.concatenate([jnp.zeros_like(tail[:1]), tail[:-1]], axis=0)
    starts = (jnp.arange(nt) % (seq // tm) == 0)[:, None, None]
    halo = jnp.where(starts, 0.0, halo)
    ff = wv.shape[1]
    return pl.pallas_call(
        functools.partial(_ffn_kernel, final=final), grid=(nt, ff // tf),
        in_specs=[pl.BlockSpec((tm, d), lambda i, j: (i, 0)),
                  pl.BlockSpec((None, FFN_HALO, d), lambda i, j: (i, 0, 0)),
                  pl.BlockSpec((1, d), lambda i, j: (0, 0)),
                  pl.BlockSpec((d, tf), lambda i, j: (0, j)),
                  pl.BlockSpec((d, tf), lambda i, j: (0, j)),
                  pl.BlockSpec((CONV_W, tf), lambda i, j: (0, j)),
                  pl.BlockSpec((1, tf), lambda i, j: (0, j)),
                  pl.BlockSpec((tf, d), lambda i, j: (j, 0)),
                  pl.BlockSpec((1, d), lambda i, j: (0, 0))],
        out_specs=pl.BlockSpec((tm, d), lambda i, j: (i, 0)),
        out_shape=jax.ShapeDtypeStruct((m, d), F32),
        scratch_shapes=[pltpu.VMEM((tm, d), BF16), pltpu.VMEM((FFN_HALO, d), BF16)],
        compiler_params=_cparams(("parallel", "arbitrary")), name="conv_ffn",
    )(x2d, halo, gain.reshape(1, d), wv, wg, cw, cb.reshape(1, ff), wd, gain_final.reshape(1, d))


def _pad_heads(w, heads, width, padded):
    lead = w.shape[:-1]
    w = w.reshape(lead + (heads, width))
    w = jnp.pad(w, [(0, 0)] * len(lead) + [(0, 0), (0, padded - width)])
    return w.reshape(lead + (heads * padded,))


def _layer_params(l, w_in, c_gate_w2, c_gate_b, c_norm, w_out, w_up, conv_w, conv_b, w_down):
    splits = np.cumsum([A_WIDTH] * 3 + [B_WIDTH] * 3 + [B_HEADS, C_KW, C_KW, C_VW, C_VW, C_GATE_RANK])
    w_in_l = w_in[l].astype(BF16)
    fl, qc, kc, vc, rc, gl = jnp.split(w_in_l[:, splits[5]:], (splits[6:11] - splits[5]).tolist(), axis=1)
    w_all = jnp.concatenate([
        _pad_heads(vc, C_HEADS, C_DV, C_DV_PAD), _pad_heads(rc, C_HEADS, C_DV, C_DV_PAD),
        _pad_heads(qc, C_HEADS, C_DK, C_DK_PAD), _pad_heads(kc, C_HEADS, C_DK, C_DK_PAD),
        w_in_l[:, :splits[5]]],
        axis=1)
    w_small = jnp.zeros((D_MODEL, V7X_LANES), BF16)
    w_small = w_small.at[:, SMALL_FL:SMALL_FL + B_HEADS].set(fl)
    w_small = w_small.at[:, SMALL_GL:SMALL_GL + C_GATE_RANK].set(gl)
    w2 = _pad_heads(c_gate_w2[l], C_HEADS, C_DK, C_DK_PAD).reshape(C_GATE_RANK, C_HEADS, C_DK_PAD)
    w2p = jnp.zeros((C_HEADS, V7X_LANES, C_DK_PAD), F32)
    w2p = w2p.at[:, SMALL_GL:SMALL_GL + C_GATE_RANK, :].set(jnp.transpose(w2, (1, 0, 2))).astype(BF16)
    cbp = _pad_heads(c_gate_b[l], C_HEADS, C_DK, C_DK_PAD).reshape(C_HEADS, 1, C_DK_PAD)
    gnp = jnp.pad(c_norm[l], (0, C_DV_PAD - C_DV)).reshape(1, C_DV_PAD)
    wo = w_out[l].astype(BF16)
    wa = wo[:A_WIDTH]
    wb = wo[A_WIDTH:A_WIDTH + B_WIDTH]
    wc = wo[A_WIDTH + B_WIDTH:].reshape(C_HEADS, C_DV, D_MODEL)
    wc = jnp.pad(wc, ((0, 0), (0, C_DV_PAD - C_DV), (0, 0))).reshape(C_HEADS * C_DV_PAD, D_MODEL)
    fpad = D_FF_PAD - D_FF
    w_up_l = w_up[l].astype(BF16)
    wv = jnp.pad(w_up_l[:, :D_FF], ((0, 0), (0, fpad)))
    wg = jnp.pad(w_up_l[:, D_FF:], ((0, 0), (0, fpad)))
    cw = jnp.pad(conv_w[l], ((0, 0), (0, fpad)))
    cb = jnp.pad(conv_b[l], (0, fpad))
    wd = jnp.pad(w_down[l].astype(BF16), ((0, fpad), (0, 0)))
    return dict(w_all=w_all, w_small=w_small, w2p=w2p, cbp=cbp, gnp=gnp,
                wa=wa, wb=wb, wc=wc, wv=wv, wg=wg, cw=cw, cb=cb, wd=wd)


def hybrid_mixer(x2d, gain, p, f_bias, bias_a, *, bsz, seq):
    proj, small = in_proj(x2d, gain, p["w_all"], p["w_small"])
    proj3 = proj.reshape(bsz, seq, PROJ_PAD)
    small3 = small.reshape(bsz, seq, V7X_LANES)
    o_a = dilated_mixture(proj3, bias_a).reshape(bsz * seq, A_WIDTH)
    o_b = fox_attention(*fox_prep(proj3, small3, f_bias)).reshape(bsz * seq, B_WIDTH)
    o_c = gla_mixer(proj3, small3, p["w2p"], p["cbp"], p["gnp"]).reshape(bsz * seq, C_HEADS * C_DV_PAD)
    return o_a, o_b, o_c


def kernel(x, mem, rel_table, mem_norm, norm_final, norm_mix, w_in, f_bias, c_gate_w2, c_gate_b,
           c_norm, w_out, norm_cross, w_cq, w_ckv, w_co, norm_ffn, w_up, conv_w, conv_b, w_down):
    bsz, seq, d = x.shape
    depth = w_in.shape[0]
    mem_len = mem.shape[1]
    assert d == D_MODEL and seq % (A_SPAN * A_PATTERNS[-1][1]) == 0 and seq % 1024 == 0
    bias_a = jnp.stack([_dilated_bias(rel_table, dil) for _, dil in A_PATTERNS], axis=0)
    x2d = x.reshape(bsz * seq, d)
    mem2d = mem.reshape(bsz * mem_len, d)
    for l in range(depth):
        p = _layer_params(l, w_in, c_gate_w2, c_gate_b, c_norm, w_out, w_up, conv_w, conv_b, w_down)
        o_a, o_b, o_c = hybrid_mixer(x2d, norm_mix[l], p, f_bias[l], bias_a, bsz=bsz, seq=seq)
        kv = norm_matmul(mem2d, mem_norm, w_ckv[l].astype(BF16), tm=min(512, bsz * mem_len), tn=512,
                         name="mem_kv").reshape(bsz, mem_len, 2 * CROSS_WIDTH)
        x2d = mix_out_cross(x2d, o_a, o_b, o_c, p["wa"], p["wb"], p["wc"], norm_cross[l],
                            w_cq[l].astype(BF16), kv, w_co[l].astype(BF16), seq=seq)
        x2d = conv_ffn(x2d, norm_ffn[l], p["wv"], p["wg"], p["cw"], p["cb"], p["wd"], norm_final,
                       seq=seq, final=l == depth - 1)
    return x2d.reshape(bsz, seq, d)
```

```python
import functools
import math

import numpy as np
import jax
import jax.numpy as jnp
from jax import lax
from jax.experimental import pallas as pl
from jax.experimental.pallas import tpu as pltpu

F32 = jnp.float32
BF16 = jnp.bfloat16

V7X_LANES = 128
V7X_VMEM_LIMIT_BYTES = 56 * 1024 * 1024

D_MODEL = 2048
HEAD_DIM = 64
A_WIDTH = 3 * D_MODEL // 8
A_HEADS = A_WIDTH // HEAD_DIM
A_PATTERNS = ((128, 1), (512, 4), (2048, 16))
A_SPAN = 128
B_WIDTH = D_MODEL // 4
B_HEADS = B_WIDTH // HEAD_DIM
C_HEADS = 4
C_VW = 3 * D_MODEL // 8
C_DV = C_VW // C_HEADS
C_DK = C_DV // 2
C_KW = C_HEADS * C_DK
C_GATE_RANK = 16
C_GATE_TAU = 16.0
C_CHUNK = 64
C_SUB = 8
C_DK_PAD = 128
C_DV_PAD = 256
CROSS_HEADS = 4
CROSS_DH = 128
CROSS_WIDTH = CROSS_HEADS * CROSS_DH
D_FF = ((8 * D_MODEL // 3 + 127) // 128) * 128
FF_TILE = 512
D_FF_PAD = ((D_FF + FF_TILE - 1) // FF_TILE) * FF_TILE
CONV_W = 3
REL_BUCKETS = 32
REL_MAX_DIST = 2048
EPS = 1e-6
NEG = -1e30
LOG2E = 1.4426950408889634

COL_VC = 0
COL_RC = COL_VC + C_HEADS * C_DV_PAD
COL_QC = COL_RC + C_HEADS * C_DV_PAD
COL_KC = COL_QC + C_HEADS * C_DK_PAD
COL_QA = COL_KC + C_HEADS * C_DK_PAD
COL_KA = COL_QA + A_WIDTH
COL_VA = COL_KA + A_WIDTH
COL_QB = COL_VA + A_WIDTH
COL_KB = COL_QB + B_WIDTH
COL_VB = COL_KB + B_WIDTH
PROJ_PAD = COL_VB + B_WIDTH
SMALL_FL = 0
SMALL_GL = B_HEADS


def _cparams(sem):
    return pltpu.CompilerParams(dimension_semantics=sem,
                                vmem_limit_bytes=V7X_VMEM_LIMIT_BYTES)


def _rms_rows(x, g):
    ms = jnp.mean(x * x, axis=-1, keepdims=True)
    return x * lax.rsqrt(ms + EPS) * g


def _split3(x):
    hi = x.astype(BF16)
    r1 = x - hi.astype(F32)
    mid = r1.astype(BF16)
    lo = (r1 - mid.astype(F32)).astype(BF16)
    return hi, mid, lo


def _log_sigmoid(x):
    return jnp.minimum(x, 0.0) - jnp.log(1.0 + jnp.exp(-jnp.abs(x)))


def _tril_ones(n, dtype):
    r = lax.broadcasted_iota(jnp.int32, (n, n), 0)
    c = lax.broadcasted_iota(jnp.int32, (n, n), 1)
    return jnp.where(r >= c, 1.0, 0.0).astype(dtype)


NORM_ROWS = 256


def _norm_to_scratch(x_ref, g_ref, h_scr):
    def body(r, c):
        sl = pl.ds(pl.multiple_of(r * NORM_ROWS, NORM_ROWS), NORM_ROWS)
        h_scr[sl, :] = _rms_rows(x_ref[sl, :], g_ref[...]).astype(BF16)
        return c
    lax.fori_loop(0, x_ref.shape[0] // NORM_ROWS, body, 0)


def _norm_matmul_kernel(x_ref, g_ref, w_ref, o_ref, h_scr):
    @pl.when(pl.program_id(1) == 0)
    def _():
        _norm_to_scratch(x_ref, g_ref, h_scr)
    o_ref[...] = jnp.dot(h_scr[...], w_ref[...],
                         preferred_element_type=F32).astype(o_ref.dtype)


def norm_matmul(x2d, gain, w, *, tm, tn, name):
    m, k = x2d.shape
    n = w.shape[1]
    return pl.pallas_call(
        _norm_matmul_kernel, grid=(m // tm, n // tn),
        in_specs=[pl.BlockSpec((tm, k), lambda i, j: (i, 0)),
                  pl.BlockSpec((1, k), lambda i, j: (0, 0)),
                  pl.BlockSpec((k, tn), lambda i, j: (0, j))],
        out_specs=pl.BlockSpec((tm, tn), lambda i, j: (i, j)),
        out_shape=jax.ShapeDtypeStruct((m, n), BF16),
        scratch_shapes=[pltpu.VMEM((tm, k), BF16)],
        compiler_params=_cparams(("parallel", "arbitrary")), name=name)(x2d, gain.reshape(1, k), w)


IN_PROJ_ROWS = 1024
IN_PROJ_TILE = 3 * A_WIDTH


def _in_proj_kernel(x_ref, g_ref, w_ref, ws_ref, o_ref, os_ref, h_scr):
    @pl.when(pl.program_id(1) == 0)
    def _():
        _norm_to_scratch(x_ref, g_ref, h_scr)
        os_ref[...] = jnp.dot(h_scr[...], ws_ref[...], preferred_element_type=F32)

    o_ref[...] = jnp.dot(h_scr[...], w_ref[...], preferred_element_type=F32).astype(o_ref.dtype)


def in_proj(x2d, gain, w_all, w_small):
    m, k = x2d.shape
    tm, tn = IN_PROJ_ROWS, IN_PROJ_TILE
    ns = w_small.shape[1]
    return pl.pallas_call(
        _in_proj_kernel, grid=(m // tm, PROJ_PAD // tn),
        in_specs=[pl.BlockSpec((tm, k), lambda i, j: (i, 0)),
                  pl.BlockSpec((1, k), lambda i, j: (0, 0)),
                  pl.BlockSpec((k, tn), lambda i, j: (0, j)),
                  pl.BlockSpec((k, ns), lambda i, j: (0, 0))],
        out_specs=[pl.BlockSpec((tm, tn), lambda i, j: (i, j)),
                   pl.BlockSpec((tm, ns), lambda i, j: (i, 0))],
        out_shape=[jax.ShapeDtypeStruct((m, PROJ_PAD), BF16),
                   jax.ShapeDtypeStruct((m, ns), F32)],
        scratch_shapes=[pltpu.VMEM((tm, k), BF16)],
        compiler_params=_cparams(("parallel", "arbitrary")), name="in_proj",
    )(x2d, gain.reshape(1, k), w_all, w_small)


def _t5_bucket(dist):
    max_exact = REL_BUCKETS // 2
    d = np.maximum(dist, 1).astype(np.float32)
    large = max_exact + (np.log(d / max_exact) / math.log(REL_MAX_DIST / max_exact)
                         * (REL_BUCKETS - max_exact)).astype(np.int32)
    return np.where(dist < max_exact, dist, np.minimum(large, REL_BUCKETS - 1)).astype(np.int32)


def _dilated_bias(rel_table, dilation):
    span = A_SPAN
    steps = np.arange(2 * span, -span, -1)
    valid = (steps >= 0) & (steps <= span)
    per_step = rel_table[_t5_bucket(np.clip(steps, 0, span) * dilation)].astype(F32)
    per_step = jnp.where(valid[:, None], per_step * LOG2E, NEG).T
    later = jnp.stack([per_step[:, 2 * span - i - span: 4 * span - i - span] for i in range(span)],
                      axis=1)
    first = jnp.where((np.arange(2 * span) >= span)[None, None, :], later, NEG)
    return jnp.stack([first, later], axis=0)


DIL_TILE = A_SPAN * max(d for _, d in A_PATTERNS)
DIL_UNITS = DIL_TILE // A_SPAN
DIL_GROUP = 4
DIL_COMBINE_ROWS = 256


def _dilated_kernel(qb_ref, kp_ref, kc_ref, vp_ref, vc_ref, bias_ref, o_ref,
                    q_ref, kk_scr, vv_scr, m_scr, n_scr, s_scr):
    tile, span = DIL_TILE, A_SPAN
    first_tile = pl.program_id(1) == 0
    q_ref[...] = qb_ref[...].astype(F32)
    kk_scr[0:tile, :] = kp_ref[...].astype(F32)
    kk_scr[tile:2 * tile, :] = kc_ref[...].astype(F32)
    vv_scr[0:tile, :] = vp_ref[...].astype(F32)
    vv_scr[tile:2 * tile, :] = vc_ref[...].astype(F32)
    lane_q = lax.broadcasted_iota(jnp.int32, (span, V7X_LANES), 1)
    lane_k = lax.broadcasted_iota(jnp.int32, (2 * span, V7X_LANES), 1)
    low_q = lane_q < HEAD_DIM
    low_k = lane_k < HEAD_DIM
    scale = HEAD_DIM ** -0.5 * LOG2E
    qmask = (jnp.where(low_q, scale, 0.0), jnp.where(low_q, 0.0, scale))
    nt = (((1,), (1,)), ((), ()))

    for pat, (_, d) in enumerate(A_PATTERNS):
        for u0 in range(0, DIL_UNITS, DIL_GROUP):
            units = []
            for u in range(u0, u0 + DIL_GROUP):
                n, r = divmod(u, d)
                q0 = r + span * d * n
                k0 = tile + q0 - span * d
                if d == 1:
                    qrows, krows = pl.ds(q0, span), pl.ds(k0, 2 * span)
                else:
                    qrows, krows = pl.ds(q0, span, stride=d), pl.ds(k0, 2 * span, stride=d)
                qf = q_ref[qrows, :]
                kb = kk_scr[krows, :].astype(BF16)
                sel = jnp.where(first_tile, 0, 1) if n == 0 else 1
                logits = [lax.dot_general((qf * qmask[e]).astype(BF16), kb, nt,
                                          preferred_element_type=F32) + bias_ref[pat, sel, e]
                          for e in range(2)]
                units.append((qrows, krows, logits))
            results = []
            for qrows, krows, logits in units:
                vf = vv_scr[krows, :]
                pvs, ms = [], []
                for e in range(2):
                    m = jnp.max(logits[e], axis=-1, keepdims=True)
                    p = jnp.exp2(logits[e] - m)
                    ve = (jnp.where(low_k, vf, 1.0) if e == 0 else jnp.where(low_k, 1.0, vf)).astype(BF16)
                    pvs.append(jnp.dot(p.astype(BF16), ve, preferred_element_type=F32))
                    ms.append(m)
                results.append((qrows, pvs, ms))
            for qrows, pvs, ms in results:
                m_scr[pat, qrows, :] = jnp.where(low_q, ms[0], ms[1])
                n_scr[pat, qrows, :] = jnp.where(low_q, pvs[0], pvs[1])
                s_scr[pat, qrows, :] = pltpu.roll(jnp.where(low_q, pvs[1], pvs[0]), HEAD_DIM, axis=1)

    def combine(c, carry):
        rows = pl.ds(pl.multiple_of(c * DIL_COMBINE_ROWS, DIL_COMBINE_ROWS), DIL_COMBINE_ROWS)
        ms = [m_scr[pat, rows, :] for pat in range(len(A_PATTERNS))]
        top = functools.reduce(jnp.maximum, ms)
        ws = [jnp.exp2(m - top) for m in ms]
        num = sum(w * n_scr[pat, rows, :] for pat, w in enumerate(ws))
        den = sum(w * s_scr[pat, rows, :] for pat, w in enumerate(ws))
        o_ref[rows, :] = (num / den).astype(o_ref.dtype)
        return carry

    lax.fori_loop(0, tile // DIL_COMBINE_ROWS, combine, 0)


def dilated_mixture(proj3, bias):
    bsz, s, _ = proj3.shape
    pairs = A_HEADS // 2
    blk = (None, DIL_TILE, V7X_LANES)
    cq, ck, cv = COL_QA // V7X_LANES, COL_KA // V7X_LANES, COL_VA // V7X_LANES

    def cur(col):
        return pl.BlockSpec(blk, lambda b, t, g: (b, t, col + g))

    def prev(col):
        return pl.BlockSpec(blk, lambda b, t, g: (b, jnp.maximum(t - 1, 0), col + g))

    tile_f32 = pltpu.VMEM((DIL_TILE, V7X_LANES), F32)
    both_f32 = pltpu.VMEM((2 * DIL_TILE, V7X_LANES), F32)
    stats_f32 = pltpu.VMEM((len(A_PATTERNS), DIL_TILE, V7X_LANES), F32)
    return pl.pallas_call(
        _dilated_kernel, grid=(bsz, s // DIL_TILE, pairs),
        in_specs=[cur(cq), prev(ck), cur(ck), prev(cv), cur(cv),
                  pl.BlockSpec((len(A_PATTERNS), 2, 2, A_SPAN, 2 * A_SPAN),
                               lambda b, t, g: (0, 0, g, 0, 0))],
        out_specs=pl.BlockSpec(blk, lambda b, t, g: (b, t, g)),
        out_shape=jax.ShapeDtypeStruct((bsz, s, A_WIDTH), BF16),
        scratch_shapes=[tile_f32, both_f32, both_f32, stats_f32, stats_f32, stats_f32],
        compiler_params=_cparams(("parallel", "parallel", "arbitrary")), name="dilated_mixture",
    )(proj3, proj3, proj3, proj3, proj3, bias)


FOX_PREP_ROWS = 512
FOX_TQ = 512
FOX_TK = 512
FOX_HEADS_PER_STEP = 4


def _fox_prep_kernel(sm_ref, fb_ref, selq_ref, selk_ref, q0_ref, q1_ref, k0_ref, k1_ref, v0_ref, v1_ref,
                     qa_ref, ka_ref, va_ref, carry_scr):
    t = sm_ref.shape[0]

    @pl.when(pl.program_id(1) == 0)
    def _():
        carry_scr[...] = jnp.zeros_like(carry_scr)

    x = _log_sigmoid(sm_ref[...] + fb_ref[...])
    tri = _tril_ones(t, BF16)
    hi, mid, lo = _split3(x)
    c = (jnp.dot(tri, hi, preferred_element_type=F32)
         + jnp.dot(tri, mid, preferred_element_type=F32)
         + jnp.dot(tri, lo, preferred_element_type=F32)) + carry_scr[0:1, :]
    carry_scr[...] = jnp.broadcast_to(c[t - 1:t, :], carry_scr.shape)
    chi, cmid, clo = [p.astype(F32) for p in _split3(c * LOG2E)]

    lane = lax.broadcasted_iota(jnp.int32, (t, V7X_LANES), 1)
    low = lane < HEAD_DIM
    scale = HEAD_DIM ** -0.5 * LOG2E
    src = jnp.where(lane < B_HEADS, chi,
                    jnp.where(lane < 2 * B_HEADS, pltpu.roll(cmid, B_HEADS, axis=1),
                              jnp.where(lane < 3 * B_HEADS, pltpu.roll(clo, 2 * B_HEADS, axis=1),
                                        jnp.where(lane == 3 * B_HEADS, 1.0, 0.0)))).astype(BF16)
    q_refs, k_refs, v_refs = (q0_ref, q1_ref), (k0_ref, k1_ref), (v0_ref, v1_ref)
    for h in range(B_HEADS):
        src_ref, pair = divmod(h // 2, 2)
        cs = slice(V7X_LANES * pair, V7X_LANES * (pair + 1))
        qp = q_refs[src_ref][:, cs].astype(F32)
        kp = k_refs[src_ref][:, cs].astype(F32)
        vp = v_refs[src_ref][:, cs].astype(F32)
        if h % 2 == 1:
            qp = pltpu.roll(qp, HEAD_DIM, axis=1)
            kp = pltpu.roll(kp, HEAD_DIM, axis=1)
        q_aug = jnp.dot(src, selq_ref[h], preferred_element_type=F32)
        k_aug = jnp.dot(src, selk_ref[h], preferred_element_type=F32)
        qa_ref[h] = jnp.where(low, qp * scale, q_aug).T.astype(BF16)
        ka_ref[h] = jnp.where(low, kp, k_aug).astype(BF16)
        v_aug = jnp.where(low, vp, 1.0) if h % 2 == 0 else jnp.where(low, 1.0, vp)
        va_ref[h, 0] = v_aug.T.astype(BF16)


def _fox_selectors():
    selq = np.zeros((B_HEADS, V7X_LANES, V7X_LANES), np.float32)
    selk = np.zeros((B_HEADS, V7X_LANES, V7X_LANES), np.float32)
    one = 3 * B_HEADS
    for h in range(B_HEADS):
        for part in range(3):
            selq[h, part * B_HEADS + h, HEAD_DIM + part] = 1.0
            selq[h, one, HEAD_DIM + 3 + part] = 1.0
            selk[h, one, HEAD_DIM + part] = 1.0
            selk[h, part * B_HEADS + h, HEAD_DIM + 3 + part] = -1.0
    return jnp.asarray(selq, BF16), jnp.asarray(selk, BF16)


def fox_prep(proj3, small3, f_bias):
    bsz, s, _ = proj3.shape
    t = FOX_PREP_ROWS
    half = B_WIDTH // 2
    fb = jnp.zeros((1, V7X_LANES), F32).at[0, SMALL_FL:SMALL_FL + B_HEADS].set(f_bias)

    def colspec(col):
        return pl.BlockSpec((None, t, half), lambda b, i, c=col // half: (b, i, c))

    assert t == FOX_TK
    selq, selk = _fox_selectors()
    sel_spec = pl.BlockSpec(selq.shape, lambda b, i: (0, 0, 0))
    k_spec = pl.BlockSpec((None, B_HEADS, t, V7X_LANES), lambda b, i: (b, 0, i, 0))
    k_shape = jax.ShapeDtypeStruct((bsz, B_HEADS, s, V7X_LANES), BF16)
    q_spec = pl.BlockSpec((None, B_HEADS, V7X_LANES, t), lambda b, i: (b, 0, 0, i))
    q_shape = jax.ShapeDtypeStruct((bsz, B_HEADS, V7X_LANES, s), BF16)
    v_spec = pl.BlockSpec((None, B_HEADS, 1, V7X_LANES, t), lambda b, i: (b, 0, i, 0, 0))
    v_shape = jax.ShapeDtypeStruct((bsz, B_HEADS, s // t, V7X_LANES, t), BF16)
    return pl.pallas_call(
        _fox_prep_kernel, grid=(bsz, s // t),
        in_specs=[pl.BlockSpec((None, t, V7X_LANES), lambda b, i: (b, i, 0)),
                  pl.BlockSpec((1, V7X_LANES), lambda b, i: (0, 0)),
                  sel_spec, sel_spec,
                  colspec(COL_QB), colspec(COL_QB + half),
                  colspec(COL_KB), colspec(COL_KB + half),
                  colspec(COL_VB), colspec(COL_VB + half)],
        out_specs=[q_spec, k_spec, v_spec],
        out_shape=[q_shape, k_shape, v_shape],
        scratch_shapes=[pltpu.VMEM((8, V7X_LANES), F32)],
        compiler_params=_cparams(("parallel", "arbitrary")), name="fox_prep",
    )(small3, fb, selq, selk, proj3, proj3, proj3, proj3, proj3, proj3)


def _fox_kernel(q_ref, k_ref, v_ref, o_ref, m_scr, acc_scr):
    tq, tk = FOX_TQ, FOX_TK
    qi = pl.program_id(2)
    nfull = qi * (tq // tk)
    krow = lax.broadcasted_iota(jnp.int32, (tk, tq), 0)
    qcol = lax.broadcasted_iota(jnp.int32, (tk, tq), 1)
    frow = lax.broadcasted_iota(jnp.int32, (V7X_LANES, tq), 0)
    nh = q_ref.shape[0]
    qs = [q_ref[e] for e in range(nh)]
    m_scr[...] = jnp.full(m_scr.shape, NEG, F32)
    acc_scr[...] = jnp.zeros(acc_scr.shape, F32)

    def heads(kt, mask):
        rows = pl.ds(pl.multiple_of(kt * tk, tk), tk)
        scores = [jnp.dot(k_ref[e, rows, :], qs[e], preferred_element_type=F32) for e in range(nh)]
        for e in range(nh):
            m = m_scr[e, 0:1, :]
            s = scores[e] if mask is None else jnp.where(mask, scores[e], NEG)
            m_new = jnp.maximum(m, jnp.max(s, axis=0, keepdims=True))
            alpha = jnp.exp2(m - m_new)
            p = jnp.exp2(s - m_new)
            pv = jnp.dot(v_ref[e, kt], p.astype(BF16), preferred_element_type=F32)
            acc_scr[e] = alpha * acc_scr[e] + pv
            m_scr[e] = jnp.broadcast_to(m_new, (8, tq))

    def body(kt, c):
        heads(kt, None)
        return c

    lax.fori_loop(0, nfull, body, 0)
    for dd in range(tq // tk):
        heads(nfull + dd, krow + dd * tk <= qcol)
    for g in range(nh // 2):
        even, odd = acc_scr[2 * g], acc_scr[2 * g + 1]
        res = [acc / pltpu.roll(acc, HEAD_DIM, axis=0) for acc in (even, odd)]
        pair = jnp.where(frow < HEAD_DIM, res[0], res[1])
        o_ref[:, V7X_LANES * g:V7X_LANES * (g + 1)] = pair.T.astype(o_ref.dtype)


def fox_attention(q_aug, k_aug, v_aug):
    bsz, nh, s, _ = k_aug.shape
    tq, tk, hs = FOX_TQ, FOX_TK, FOX_HEADS_PER_STEP
    return pl.pallas_call(
        _fox_kernel, grid=(bsz, nh // hs, s // tq),
        in_specs=[pl.BlockSpec((None, hs, V7X_LANES, tq), lambda b, g, i: (b, g, 0, i)),
                  pl.BlockSpec((None, hs, s, V7X_LANES), lambda b, g, i: (b, g, 0, 0)),
                  pl.BlockSpec((None, hs, s // tk, V7X_LANES, tk), lambda b, g, i: (b, g, 0, 0, 0))],
        out_specs=pl.BlockSpec((None, tq, hs * HEAD_DIM), lambda b, g, i: (b, i, g)),
        out_shape=jax.ShapeDtypeStruct((bsz, s, B_WIDTH), BF16),
        scratch_shapes=[pltpu.VMEM((hs, 8, tq), F32), pltpu.VMEM((hs, V7X_LANES, tq), F32)],
        compiler_params=_cparams(("parallel", "parallel", "arbitrary")), name="fox_attention",
    )(q_aug, k_aug, v_aug)


GLA_ROWS = 512
GLA_BATCH = 4


def _gla_chunk(rows, q_ref, k_ref, v_ref, r_ref, sm_bf, w2_ref, cb_ref, gn_ref, o_ref,
               st_scr, b_scr, k_scr, tri, row_sub, row_ck, col_ck):
    ck, sub = C_CHUNK, C_SUB
    nblk = ck // sub
    heads = range(q_ref.shape[0] * C_HEADS)
    bi = [h // C_HEADS for h in heads]
    hh = [h % C_HEADS for h in heads]
    nt = (((1,), (1,)), ((), ()))
    kcols = [slice(C_DK_PAD * hh[h], C_DK_PAD * (hh[h] + 1)) for h in heads]
    vcols = [slice(C_DV_PAD * hh[h], C_DV_PAD * (hh[h] + 1)) for h in heads]
    g = [jnp.dot(sm_bf[bi[h]], w2_ref[hh[h]], preferred_element_type=F32) + cb_ref[hh[h]] for h in heads]
    parts = [_split3(_log_sigmoid(g[h]) * (LOG2E / C_GATE_TAU)) for h in heads]
    b = [(jnp.dot(tri, parts[h][0], preferred_element_type=F32)
          + jnp.dot(tri, parts[h][1], preferred_element_type=F32)
          + jnp.dot(tri, parts[h][2], preferred_element_type=F32)) for h in heads]
    qf = [q_ref[bi[h], rows, kcols[h]].astype(F32) * (C_DK ** -0.5) for h in heads]
    kf = [k_ref[bi[h], rows, kcols[h]].astype(F32) for h in heads]
    vb = [v_ref[bi[h], rows, vcols[h]] for h in heads]
    st = [st_scr[h] for h in heads]
    for h in heads:
        b_scr[h] = b[h]
        k_scr[h] = kf[h]
    o_inter = [lax.dot_general((qf[h] * jnp.exp2(b[h])).astype(BF16), st[h].astype(BF16), nt,
                               preferred_element_type=F32) for h in heads]
    a = {}
    for h in heads:
        for blk in range(1, nblk):
            r0 = blk * sub
            ref_row = b_scr[h, r0 - 1:r0, :]
            kd = jnp.where(row_ck < r0, kf[h] * jnp.exp2(jnp.minimum(ref_row - b[h], 0.0)), 0.0)
            qd = qf[h][r0:r0 + sub, :] * jnp.exp2(b[h][r0:r0 + sub, :] - ref_row)
            a[h, blk] = lax.dot_general(qd.astype(BF16), kd.astype(BF16), nt, preferred_element_type=F32)
    kdec = [(kf[h] * jnp.exp2(b_scr[h, ck - 1:ck, :] - b[h])).astype(BF16) for h in heads]
    upd = [lax.dot_general(vb[h], kdec[h], (((0,), (0,)), ((), ())), preferred_element_type=F32)
           for h in heads]
    for h in heads:
        st_scr[h] = st[h] * jnp.exp2(b_scr[h, ck - 1:ck, :]) + upd[h]
    for h in heads:
        outs = []
        for blk in range(nblk):
            r0 = blk * sub
            qb = qf[h][r0:r0 + sub, :]
            bb = b[h][r0:r0 + sub, :]
            scores = a[h, blk] if blk > 0 else jnp.zeros((sub, ck), F32)
            for j in range(sub):
                bj = b_scr[h, r0 + j:r0 + j + 1, :]
                kj = k_scr[h, r0 + j:r0 + j + 1, :]
                x = qb * kj * jnp.exp2(bb - bj)
                colv = jnp.sum(x, axis=-1, keepdims=True)
                scores = jnp.where(jnp.logical_and(col_ck == r0 + j, row_sub >= j), colv, scores)
            outs.append(o_inter[h][r0:r0 + sub, :]
                        + jnp.dot(scores.astype(BF16), vb[h], preferred_element_type=F32))
        o = jnp.concatenate(outs, axis=0)
        ms = jnp.sum(o * o, axis=-1, keepdims=True) * (1.0 / C_DV)
        on = o * lax.rsqrt(ms + EPS) * gn_ref[...]
        rr = r_ref[bi[h], rows, vcols[h]].astype(F32)
        o_ref[bi[h], rows, vcols[h]] = (on * (rr / (1.0 + jnp.exp(-rr)))).astype(o_ref.dtype)


def _gla_kernel(q_ref, k_ref, v_ref, r_ref, sm_ref, w2_ref, cb_ref, gn_ref, o_ref,
                st_scr, b_scr, k_scr):
    ck, sub = C_CHUNK, C_SUB

    @pl.when(pl.program_id(1) == 0)
    def _():
        st_scr[...] = jnp.zeros_like(st_scr)

    tri = _tril_ones(ck, BF16)
    row_sub = lax.broadcasted_iota(jnp.int32, (sub, 1), 0)
    row_ck = lax.broadcasted_iota(jnp.int32, (ck, 1), 0)
    col_ck = lax.broadcasted_iota(jnp.int32, (sub, ck), 1)

    def chunk(c, carry):
        rows = pl.ds(pl.multiple_of(c * ck, ck), ck)
        sm_bf = [sm_ref[i, rows, :].astype(BF16) for i in range(sm_ref.shape[0])]
        _gla_chunk(rows, q_ref, k_ref, v_ref, r_ref, sm_bf, w2_ref, cb_ref, gn_ref, o_ref,
                   st_scr, b_scr, k_scr, tri, row_sub, row_ck, col_ck)
        return carry

    lax.fori_loop(0, q_ref.shape[1] // ck, chunk, 0)


def gla_mixer(proj3, small3, w2p, cbp, gnp):
    bsz, s, _ = proj3.shape
    t = GLA_ROWS
    kw, vw = C_HEADS * C_DK_PAD, C_HEADS * C_DV_PAD

    nb = math.gcd(bsz, GLA_BATCH)

    def col(base, width):
        return pl.BlockSpec((nb, t, width), lambda b, i, c=base // width: (b, i, c))

    return pl.pallas_call(
        _gla_kernel, grid=(bsz // nb, s // t),
        in_specs=[col(COL_QC, kw), col(COL_KC, kw), col(COL_VC, vw), col(COL_RC, vw),
                  pl.BlockSpec((nb, t, V7X_LANES), lambda b, i: (b, i, 0)),
                  pl.BlockSpec((C_HEADS, V7X_LANES, C_DK_PAD), lambda b, i: (0, 0, 0)),
                  pl.BlockSpec((C_HEADS, 1, C_DK_PAD), lambda b, i: (0, 0, 0)),
                  pl.BlockSpec((1, C_DV_PAD), lambda b, i: (0, 0))],
        out_specs=pl.BlockSpec((nb, t, vw), lambda b, i: (b, i, 0)),
        out_shape=jax.ShapeDtypeStruct((bsz, s, vw), BF16),
        scratch_shapes=[pltpu.VMEM((nb * C_HEADS, C_DV_PAD, C_DK_PAD), F32),
                        pltpu.VMEM((nb * C_HEADS, C_CHUNK, C_DK_PAD), F32),
                        pltpu.VMEM((nb * C_HEADS, C_CHUNK, C_DK_PAD), F32)],
        compiler_params=_cparams(("parallel", "arbitrary")), name="gla_mixer",
    )(proj3, proj3, proj3, proj3, small3, w2p, cbp, gnp)


MIX_OUT_ROWS = 512


def _mix_out_cross_kernel(x_ref, a_ref, b_ref, c_ref, wa_ref, wb_ref, wc_ref,
                          g_ref, wq_ref, k_ref, v_ref, wo_ref, o_ref):
    acc = jnp.dot(a_ref[...], wa_ref[...], preferred_element_type=F32)
    acc += jnp.dot(b_ref[...], wb_ref[...], preferred_element_type=F32)
    acc += jnp.dot(c_ref[...], wc_ref[...], preferred_element_type=F32)
    x = x_ref[...] + acc
    h = _rms_rows(x, g_ref[...]).astype(BF16)
    q = jnp.dot(h, wq_ref[...], preferred_element_type=F32).astype(BF16)
    cols = [slice(CROSS_DH * hd, CROSS_DH * (hd + 1)) for hd in range(CROSS_HEADS)]
    scores = [lax.dot_general(q[:, cs], k_ref[:, cs], (((1,), (1,)), ((), ())),
                              preferred_element_type=F32) * (CROSS_DH ** -0.5) for cs in cols]
    heads = []
    for cs, logits in zip(cols, scores):
        m = jnp.max(logits, axis=-1, keepdims=True)
        p = jnp.exp(logits - m)
        ssum = jnp.sum(p, axis=-1, keepdims=True)
        pv = jnp.dot(p.astype(BF16), v_ref[:, cs], preferred_element_type=F32)
        heads.append((pv / ssum).astype(BF16))
    o = jnp.concatenate(heads, axis=1)
    o_ref[...] = x + jnp.dot(o, wo_ref[...], preferred_element_type=F32)


def mix_out_cross(x2d, oa, ob, oc, wa, wb, wc, gain, wq, kv, wo, *, seq):
    m, d = x2d.shape
    tm = MIX_OUT_ROWS
    mem_len = kv.shape[1]
    per_batch = seq // tm

    def rows(a):
        return pl.BlockSpec((tm, a.shape[1]), lambda i: (i, 0))

    def resident(a):
        return pl.BlockSpec(a.shape, lambda i: (0,) * a.ndim, pipeline_mode=pl.Buffered(1))

    g2 = gain.reshape(1, d)
    return pl.pallas_call(
        _mix_out_cross_kernel, grid=(m // tm,),
        in_specs=[rows(x2d), rows(oa), rows(ob), rows(oc), resident(wa), resident(wb), resident(wc),
                  resident(g2), resident(wq),
                  pl.BlockSpec((None, mem_len, CROSS_WIDTH), lambda i: (i // per_batch, 0, 0)),
                  pl.BlockSpec((None, mem_len, CROSS_WIDTH), lambda i: (i // per_batch, 0, 1)),
                  resident(wo)],
        out_specs=pl.BlockSpec((tm, d), lambda i: (i, 0)),
        out_shape=jax.ShapeDtypeStruct((m, d), F32),
        compiler_params=_cparams(("parallel",)), name="mix_out_cross",
    )(x2d, oa, ob, oc, wa, wb, wc, g2, wq, kv, kv, wo)


FFN_ROWS = 1024
FFN_HALO = 16


def _ffn_kernel(x_ref, xh_ref, g_ref, wv_ref, wg_ref, cw_ref, cb_ref, wd_ref, gf_ref, o_ref,
                h_scr, hh_scr, *, final):
    j = pl.program_id(1)

    @pl.when(j == 0)
    def _():
        _norm_to_scratch(x_ref, g_ref, h_scr)
        hh_scr[...] = _rms_rows(xh_ref[...], g_ref[...]).astype(BF16)
        o_ref[...] = x_ref[...]

    h = h_scr[...]
    halo = jnp.dot(hh_scr[...], wg_ref[...], preferred_element_type=F32)
    gate = jnp.dot(h, wg_ref[...], preferred_element_type=F32)
    val = jnp.dot(h, wv_ref[...], preferred_element_type=F32)
    row = lax.broadcasted_iota(jnp.int32, gate.shape, 0)
    prev1 = halo[FFN_HALO - 1:, :]
    prev2 = halo[FFN_HALO - 2:FFN_HALO - 1, :]
    g1 = jnp.where(row == 0, prev1, pltpu.roll(gate, 1, axis=0))
    g2 = jnp.where(row == 0, prev2, jnp.where(row == 1, prev1, pltpu.roll(gate, 2, axis=0)))
    gc = cb_ref[...] + cw_ref[0:1, :] * g2
    gc = gc + cw_ref[1:2, :] * g1
    gc = gc + cw_ref[2:3, :] * gate
    act = (gc / (1.0 + jnp.exp(-gc)) * val).astype(BF16)
    o_ref[...] += jnp.dot(act, wd_ref[...], preferred_element_type=F32)

    if final:
        @pl.when(j == pl.num_programs(1) - 1)
        def _():
            def body(r, c):
                sl = pl.ds(pl.multiple_of(r * NORM_ROWS, NORM_ROWS), NORM_ROWS)
                o_ref[sl, :] = _rms_rows(o_ref[sl, :], gf_ref[...])
                return c
            lax.fori_loop(0, o_ref.shape[0] // NORM_ROWS, body, 0)


def conv_ffn(x2d, gain, wv, wg, cw, cb, wd, gain_final, *, seq, final):
    m, d = x2d.shape
    tm, tf = FFN_ROWS, FF_TILE
    nt = m // tm
    tail = x2d.reshape(nt, tm, d)[:, tm - FFN_HALO:, :]
    halo = jnp.concatenate([jnp.zeros_like(tail[:1]), tail[:-1]], axis=0)
    starts = (jnp.arange(nt) % (seq // tm) == 0)[:, None, None]
    halo = jnp.where(starts, 0.0, halo)
    ff = wv.shape[1]
    return pl.pallas_call(
        functools.partial(_ffn_kernel, final=final), grid=(nt, ff // tf),
        in_specs=[pl.BlockSpec((tm, d), lambda i, j: (i, 0)),
                  pl.BlockSpec((None, FFN_HALO, d), lambda i, j: (i, 0, 0)),
                  pl.BlockSpec((1, d), lambda i, j: (0, 0)),
                  pl.BlockSpec((d, tf), lambda i, j: (0, j)),
                  pl.BlockSpec((d, tf), lambda i, j: (0, j)),
                  pl.BlockSpec((CONV_W, tf), lambda i, j: (0, j)),
                  pl.BlockSpec((1, tf), lambda i, j: (0, j)),
                  pl.BlockSpec((tf, d), lambda i, j: (j, 0)),
                  pl.BlockSpec((1, d), lambda i, j: (0, 0))],
        out_specs=pl.BlockSpec((tm, d), lambda i, j: (i, 0)),
        out_shape=jax.ShapeDtypeStruct((m, d), F32),
        scratch_shapes=[pltpu.VMEM((tm, d), BF16), pltpu.VMEM((FFN_HALO, d), BF16)],
        compiler_params=_cparams(("parallel", "arbitrary")), name="conv_ffn",
    )(x2d, halo, gain.reshape(1, d), wv, wg, cw, cb.reshape(1, ff), wd, gain_final.reshape(1, d))


def _pad_heads(w, heads, width, padded):
    lead = w.shape[:-1]
    w = w.reshape(lead + (heads, width))
    w = jnp.pad(w, [(0, 0)] * len(lead) + [(0, 0), (0, padded - width)])
    return w.reshape(lead + (heads * padded,))


def _layer_params(l, w_in, c_gate_w2, c_gate_b, c_norm, w_out, w_up, conv_w, conv_b, w_down):
    splits = np.cumsum([A_WIDTH] * 3 + [B_WIDTH] * 3 + [B_HEADS, C_KW, C_KW, C_VW, C_VW, C_GATE_RANK])
    w_in_l = w_in[l].astype(BF16)
    fl, qc, kc, vc, rc, gl = jnp.split(w_in_l[:, splits[5]:], (splits[6:11] - splits[5]).tolist(), axis=1)
    w_all = jnp.concatenate([
        _pad_heads(vc, C_HEADS, C_DV, C_DV_PAD), _pad_heads(rc, C_HEADS, C_DV, C_DV_PAD),
        _pad_heads(qc, C_HEADS, C_DK, C_DK_PAD), _pad_heads(kc, C_HEADS, C_DK, C_DK_PAD),
        w_in_l[:, :splits[5]]],
        axis=1)
    w_small = jnp.zeros((D_MODEL, V7X_LANES), BF16)
    w_small = w_small.at[:, SMALL_FL:SMALL_FL + B_HEADS].set(fl)
    w_small = w_small.at[:, SMALL_GL:SMALL_GL + C_GATE_RANK].set(gl)
    w2 = _pad_heads(c_gate_w2[l], C_HEADS, C_DK, C_DK_PAD).reshape(C_GATE_RANK, C_HEADS, C_DK_PAD)
    w2p = jnp.zeros((C_HEADS, V7X_LANES, C_DK_PAD), F32)
    w2p = w2p.at[:, SMALL_GL:SMALL_GL + C_GATE_RANK, :].set(jnp.transpose(w2, (1, 0, 2))).astype(BF16)
    cbp = _pad_heads(c_gate_b[l], C_HEADS, C_DK, C_DK_PAD).reshape(C_HEADS, 1, C_DK_PAD)
    gnp = jnp.pad(c_norm[l], (0, C_DV_PAD - C_DV)).reshape(1, C_DV_PAD)
    wo = w_out[l].astype(BF16)
    wa = wo[:A_WIDTH]
    wb = wo[A_WIDTH:A_WIDTH + B_WIDTH]
    wc = wo[A_WIDTH + B_WIDTH:].reshape(C_HEADS, C_DV, D_MODEL)
    wc = jnp.pad(wc, ((0, 0), (0, C_DV_PAD - C_DV), (0, 0))).reshape(C_HEADS * C_DV_PAD, D_MODEL)
    fpad = D_FF_PAD - D_FF
    w_up_l = w_up[l].astype(BF16)
    wv = jnp.pad(w_up_l[:, :D_FF], ((0, 0), (0, fpad)))
    wg = jnp.pad(w_up_l[:, D_FF:], ((0, 0), (0, fpad)))
    cw = jnp.pad(conv_w[l], ((0, 0), (0, fpad)))
    cb = jnp.pad(conv_b[l], (0, fpad))
    wd = jnp.pad(w_down[l].astype(BF16), ((0, fpad), (0, 0)))
    return dict(w_all=w_all, w_small=w_small, w2p=w2p, cbp=cbp, gnp=gnp,
                wa=wa, wb=wb, wc=wc, wv=wv, wg=wg, cw=cw, cb=cb, wd=wd)


def hybrid_mixer(x2d, gain, p, f_bias, bias_a, *, bsz, seq):
    proj, small = in_proj(x2d, gain, p["w_all"], p["w_small"])
    proj3 = proj.reshape(bsz, seq, PROJ_PAD)
    small3 = small.reshape(bsz, seq, V7X_LANES)
    o_a = dilated_mixture(proj3, bias_a).reshape(bsz * seq, A_WIDTH)
    o_b = fox_attention(*fox_prep(proj3, small3, f_bias)).reshape(bsz * seq, B_WIDTH)
    o_c = gla_mixer(proj3, small3, p["w2p"], p["cbp"], p["gnp"]).reshape(bsz * seq, C_HEADS * C_DV_PAD)
    return o_a, o_b, o_c


def kernel(x, mem, rel_table, mem_norm, norm_final, norm_mix, w_in, f_bias, c_gate_w2, c_gate_b,
           c_norm, w_out, norm_cross, w_cq, w_ckv, w_co, norm_ffn, w_up, conv_w, conv_b, w_down):
    bsz, seq, d = x.shape
    depth = w_in.shape[0]
    mem_len = mem.shape[1]
    assert d == D_MODEL and seq % (A_SPAN * A_PATTERNS[-1][1]) == 0 and seq % 1024 == 0
    bias_a = jnp.stack([_dilated_bias(rel_table, dil) for _, dil in A_PATTERNS], axis=0)
    x2d = x.reshape(bsz * seq, d)
    mem2d = mem.reshape(bsz * mem_len, d)
    for l in range(depth):
        p = _layer_params(l, w_in, c_gate_w2, c_gate_b, c_norm, w_out, w_up, conv_w, conv_b, w_down)
        o_a, o_b, o_c = hybrid_mixer(x2d, norm_mix[l], p, f_bias[l], bias_a, bsz=bsz, seq=seq)
        kv = norm_matmul(mem2d, mem_norm, w_ckv[l].astype(BF16), tm=min(512, bsz * mem_len), tn=512,
                         name="mem_kv").reshape(bsz, mem_len, 2 * CROSS_WIDTH)
        x2d = mix_out_cross(x2d, o_a, o_b, o_c, p["wa"], p["wb"], p["wc"], norm_cross[l],
                            w_cq[l].astype(BF16), kv, w_co[l].astype(BF16), seq=seq)
        x2d = conv_ffn(x2d, norm_ffn[l], p["wv"], p["wg"], p["cw"], p["cb"], p["wd"], norm_final,
                       seq=seq, final=l == depth - 1)
    return x2d.reshape(bsz, seq, d)
```

```python
import functools
import math

import numpy as np
import jax
import jax.numpy as jnp
from jax import lax
from jax.experimental import pallas as pl
from jax.experimental.pallas import tpu as pltpu

F32 = jnp.float32
BF16 = jnp.bfloat16

V7X_LANES = 128
V7X_VMEM_LIMIT_BYTES = 56 * 1024 * 1024

D_MODEL = 2048
HEAD_DIM = 64
A_WIDTH = 3 * D_MODEL // 8
A_HEADS = A_WIDTH // HEAD_DIM
A_PATTERNS = ((128, 1), (512, 4), (2048, 16))
A_SPAN = 128
B_WIDTH = D_MODEL // 4
B_HEADS = B_WIDTH // HEAD_DIM
C_HEADS = 4
C_VW = 3 * D_MODEL // 8
C_DV = C_VW // C_HEADS
C_DK = C_DV // 2
C_KW = C_HEADS * C_DK
C_GATE_RANK = 16
C_GATE_TAU = 16.0
C_CHUNK = 64
C_SUB = 8
C_DK_PAD = 128
C_DV_PAD = 256
CROSS_HEADS = 4
CROSS_DH = 128
CROSS_WIDTH = CROSS_HEADS * CROSS_DH
D_FF = ((8 * D_MODEL // 3 + 127) // 128) * 128
FF_TILE = 512
D_FF_PAD = ((D_FF + FF_TILE - 1) // FF_TILE) * FF_TILE
CONV_W = 3
REL_BUCKETS = 32
REL_MAX_DIST = 2048
EPS = 1e-6
NEG = -1e30
LOG2E = 1.4426950408889634

COL_VC = 0
COL_RC = COL_VC + C_HEADS * C_DV_PAD
COL_QC = COL_RC + C_HEADS * C_DV_PAD
COL_KC = COL_QC + C_HEADS * C_DK_PAD
COL_QA = COL_KC + C_HEADS * C_DK_PAD
COL_KA = COL_QA + A_WIDTH
COL_VA = COL_KA + A_WIDTH
COL_QB = COL_VA + A_WIDTH
COL_KB = COL_QB + B_WIDTH
COL_VB = COL_KB + B_WIDTH
PROJ_PAD = COL_VB + B_WIDTH
SMALL_FL = 0
SMALL_GL = B_HEADS


def _cparams(sem):
    return pltpu.CompilerParams(dimension_semantics=sem,
                                vmem_limit_bytes=V7X_VMEM_LIMIT_BYTES)


def _rms_rows(x, g):
    ms = jnp.mean(x * x, axis=-1, keepdims=True)
    return x * lax.rsqrt(ms + EPS) * g


def _split3(x):
    hi = x.astype(BF16)
    r1 = x - hi.astype(F32)
    mid = r1.astype(BF16)
    lo = (r1 - mid.astype(F32)).astype(BF16)
    return hi, mid, lo


def _log_sigmoid(x):
    return jnp.minimum(x, 0.0) - jnp.log(1.0 + jnp.exp(-jnp.abs(x)))


def _tril_ones(n, dtype):
    r = lax.broadcasted_iota(jnp.int32, (n, n), 0)
    c = lax.broadcasted_iota(jnp.int32, (n, n), 1)
    return jnp.where(r >= c, 1.0, 0.0).astype(dtype)


NORM_ROWS = 256


def _norm_to_scratch(x_ref, g_ref, h_scr):
    def body(r, c):
        sl = pl.ds(pl.multiple_of(r * NORM_ROWS, NORM_ROWS), NORM_ROWS)
        h_scr[sl, :] = _rms_rows(x_ref[sl, :], g_ref[...]).astype(BF16)
        return c
    lax.fori_loop(0, x_ref.shape[0] // NORM_ROWS, body, 0)


def _norm_matmul_kernel(x_ref, g_ref, w_ref, o_ref, h_scr):
    @pl.when(pl.program_id(1) == 0)
    def _():
        _norm_to_scratch(x_ref, g_ref, h_scr)
    o_ref[...] = jnp.dot(h_scr[...], w_ref[...],
                         preferred_element_type=F32).astype(o_ref.dtype)


def norm_matmul(x2d, gain, w, *, tm, tn, name):
    m, k = x2d.shape
    n = w.shape[1]
    return pl.pallas_call(
        _norm_matmul_kernel, grid=(m // tm, n // tn),
        in_specs=[pl.BlockSpec((tm, k), lambda i, j: (i, 0)),
                  pl.BlockSpec((1, k), lambda i, j: (0, 0)),
                  pl.BlockSpec((k, tn), lambda i, j: (0, j))],
        out_specs=pl.BlockSpec((tm, tn), lambda i, j: (i, j)),
        out_shape=jax.ShapeDtypeStruct((m, n), BF16),
        scratch_shapes=[pltpu.VMEM((tm, k), BF16)],
        compiler_params=_cparams(("parallel", "arbitrary")), name=name)(x2d, gain.reshape(1, k), w)


IN_PROJ_ROWS = 1024
IN_PROJ_TILE = 3 * A_WIDTH


def _in_proj_kernel(x_ref, g_ref, w_ref, ws_ref, o_ref, os_ref, h_scr):
    @pl.when(pl.program_id(1) == 0)
    def _():
        _norm_to_scratch(x_ref, g_ref, h_scr)
        os_ref[...] = jnp.dot(h_scr[...], ws_ref[...], preferred_element_type=F32)

    o_ref[...] = jnp.dot(h_scr[...], w_ref[...], preferred_element_type=F32).astype(o_ref.dtype)


def in_proj(x2d, gain, w_all, w_small):
    m, k = x2d.shape
    tm, tn = IN_PROJ_ROWS, IN_PROJ_TILE
    ns = w_small.shape[1]
    return pl.pallas_call(
        _in_proj_kernel, grid=(m // tm, PROJ_PAD // tn),
        in_specs=[pl.BlockSpec((tm, k), lambda i, j: (i, 0)),
                  pl.BlockSpec((1, k), lambda i, j: (0, 0)),
                  pl.BlockSpec((k, tn), lambda i, j: (0, j)),
                  pl.BlockSpec((k, ns), lambda i, j: (0, 0))],
        out_specs=[pl.BlockSpec((tm, tn), lambda i, j: (i, j)),
                   pl.BlockSpec((tm, ns), lambda i, j: (i, 0))],
        out_shape=[jax.ShapeDtypeStruct((m, PROJ_PAD), BF16),
                   jax.ShapeDtypeStruct((m, ns), F32)],
        scratch_shapes=[pltpu.VMEM((tm, k), BF16)],
        compiler_params=_cparams(("parallel", "arbitrary")), name="in_proj",
    )(x2d, gain.reshape(1, k), w_all, w_small)


def _t5_bucket(dist):
    max_exact = REL_BUCKETS // 2
    d = np.maximum(dist, 1).astype(np.float32)
    large = max_exact + (np.log(d / max_exact) / math.log(REL_MAX_DIST / max_exact)
                         * (REL_BUCKETS - max_exact)).astype(np.int32)
    return np.where(dist < max_exact, dist, np.minimum(large, REL_BUCKETS - 1)).astype(np.int32)


def _dilated_bias(rel_table, dilation):
    span = A_SPAN
    steps = np.arange(2 * span, -span, -1)
    valid = (steps >= 0) & (steps <= span)
    per_step = rel_table[_t5_bucket(np.clip(steps, 0, span) * dilation)].astype(F32)
    per_step = jnp.where(valid[:, None], per_step * LOG2E, NEG).T
    later = jnp.stack([per_step[:, 2 * span - i - span: 4 * span - i - span] for i in range(span)],
                      axis=1)
    first = jnp.where((np.arange(2 * span) >= span)[None, None, :], later, NEG)
    return jnp.stack([first, later], axis=0)


DIL_TILE = A_SPAN * max(d for _, d in A_PATTERNS)
DIL_UNITS = DIL_TILE // A_SPAN
DIL_GROUP = 4
DIL_COMBINE_ROWS = 256


def _dilated_kernel(qb_ref, kp_ref, kc_ref, vp_ref, vc_ref, bias_ref, o_ref,
                    q_ref, kk_scr, vv_scr, m_scr, n_scr, s_scr):
    tile, span = DIL_TILE, A_SPAN
    first_tile = pl.program_id(1) == 0
    q_ref[...] = qb_ref[...].astype(F32)
    kk_scr[0:tile, :] = kp_ref[...].astype(F32)
    kk_scr[tile:2 * tile, :] = kc_ref[...].astype(F32)
    vv_scr[0:tile, :] = vp_ref[...].astype(F32)
    vv_scr[tile:2 * tile, :] = vc_ref[...].astype(F32)
    lane_q = lax.broadcasted_iota(jnp.int32, (span, V7X_LANES), 1)
    lane_k = lax.broadcasted_iota(jnp.int32, (2 * span, V7X_LANES), 1)
    low_q = lane_q < HEAD_DIM
    low_k = lane_k < HEAD_DIM
    scale = HEAD_DIM ** -0.5 * LOG2E
    qmask = (jnp.where(low_q, scale, 0.0), jnp.where(low_q, 0.0, scale))
    nt = (((1,), (1,)), ((), ()))

    for pat, (_, d) in enumerate(A_PATTERNS):
        for u0 in range(0, DIL_UNITS, DIL_GROUP):
            units = []
            for u in range(u0, u0 + DIL_GROUP):
                n, r = divmod(u, d)
                q0 = r + span * d * n
                k0 = tile + q0 - span * d
                if d == 1:
                    qrows, krows = pl.ds(q0, span), pl.ds(k0, 2 * span)
                else:
                    qrows, krows = pl.ds(q0, span, stride=d), pl.ds(k0, 2 * span, stride=d)
                qf = q_ref[qrows, :]
                kb = kk_scr[krows, :].astype(BF16)
                sel = jnp.where(first_tile, 0, 1) if n == 0 else 1
                logits = [lax.dot_general((qf * qmask[e]).astype(BF16), kb, nt,
                                          preferred_element_type=F32) + bias_ref[pat, sel, e]
                          for e in range(2)]
                units.append((qrows, krows, logits))
            results = []
            for qrows, krows, logits in units:
                vf = vv_scr[krows, :]
                pvs, ms = [], []
                for e in range(2):
                    m = jnp.max(logits[e], axis=-1, keepdims=True)
                    p = jnp.exp2(logits[e] - m)
                    ve = (jnp.where(low_k, vf, 1.0) if e == 0 else jnp.where(low_k, 1.0, vf)).astype(BF16)
                    pvs.append(jnp.dot(p.astype(BF16), ve, preferred_element_type=F32))
                    ms.append(m)
                results.append((qrows, pvs, ms))
            for qrows, pvs, ms in results:
                m_scr[pat, qrows, :] = jnp.where(low_q, ms[0], ms[1])
                n_scr[pat, qrows, :] = jnp.where(low_q, pvs[0], pvs[1])
                s_scr[pat, qrows, :] = pltpu.roll(jnp.where(low_q, pvs[1], pvs[0]), HEAD_DIM, axis=1)

    def combine(c, carry):
        rows = pl.ds(pl.multiple_of(c * DIL_COMBINE_ROWS, DIL_COMBINE_ROWS), DIL_COMBINE_ROWS)
        ms = [m_scr[pat, rows, :] for pat in range(len(A_PATTERNS))]
        top = functools.reduce(jnp.maximum, ms)
        ws = [jnp.exp2(m - top) for m in ms]
        num = sum(w * n_scr[pat, rows, :] for pat, w in enumerate(ws))
        den = sum(w * s_scr[pat, rows, :] for pat, w in enumerate(ws))
        o_ref[rows, :] = (num / den).astype(o_ref.dtype)
        return carry

    lax.fori_loop(0, tile // DIL_COMBINE_ROWS, combine, 0)


def dilated_mixture(proj3, bias):
    bsz, s, _ = proj3.shape
    pairs = A_HEADS // 2
    blk = (None, DIL_TILE, V7X_LANES)
    cq, ck, cv = COL_QA // V7X_LANES, COL_KA // V7X_LANES, COL_VA // V7X_LANES

    def cur(col):
        return pl.BlockSpec(blk, lambda b, t, g: (b, t, col + g))

    def prev(col):
        return pl.BlockSpec(blk, lambda b, t, g: (b, jnp.maximum(t - 1, 0), col + g))

    tile_f32 = pltpu.VMEM((DIL_TILE, V7X_LANES), F32)
    both_f32 = pltpu.VMEM((2 * DIL_TILE, V7X_LANES), F32)
    stats_f32 = pltpu.VMEM((len(A_PATTERNS), DIL_TILE, V7X_LANES), F32)
    return pl.pallas_call(
        _dilated_kernel, grid=(bsz, s // DIL_TILE, pairs),
        in_specs=[cur(cq), prev(ck), cur(ck), prev(cv), cur(cv),
                  pl.BlockSpec((len(A_PATTERNS), 2, 2, A_SPAN, 2 * A_SPAN),
                               lambda b, t, g: (0, 0, g, 0, 0))],
        out_specs=pl.BlockSpec(blk, lambda b, t, g: (b, t, g)),
        out_shape=jax.ShapeDtypeStruct((bsz, s, A_WIDTH), BF16),
        scratch_shapes=[tile_f32, both_f32, both_f32, stats_f32, stats_f32, stats_f32],
        compiler_params=_cparams(("parallel", "parallel", "arbitrary")), name="dilated_mixture",
    )(proj3, proj3, proj3, proj3, proj3, bias)


FOX_PREP_ROWS = 512
FOX_TQ = 512
FOX_TK = 512
FOX_HEADS_PER_STEP = 4


def _fox_prep_kernel(sm_ref, fb_ref, selq_ref, selk_ref, q0_ref, q1_ref, k0_ref, k1_ref, v0_ref, v1_ref,
                     qa_ref, ka_ref, va_ref, carry_scr):
    t = sm_ref.shape[0]

    @pl.when(pl.program_id(1) == 0)
    def _():
        carry_scr[...] = jnp.zeros_like(carry_scr)

    x = _log_sigmoid(sm_ref[...] + fb_ref[...])
    tri = _tril_ones(t, BF16)
    hi, mid, lo = _split3(x)
    c = (jnp.dot(tri, hi, preferred_element_type=F32)
         + jnp.dot(tri, mid, preferred_element_type=F32)
         + jnp.dot(tri, lo, preferred_element_type=F32)) + carry_scr[0:1, :]
    carry_scr[...] = jnp.broadcast_to(c[t - 1:t, :], carry_scr.shape)
    chi, cmid, clo = [p.astype(F32) for p in _split3(c * LOG2E)]

    lane = lax.broadcasted_iota(jnp.int32, (t, V7X_LANES), 1)
    low = lane < HEAD_DIM
    scale = HEAD_DIM ** -0.5 * LOG2E
    src = jnp.where(lane < B_HEADS, chi,
                    jnp.where(lane < 2 * B_HEADS, pltpu.roll(cmid, B_HEADS, axis=1),
                              jnp.where(lane < 3 * B_HEADS, pltpu.roll(clo, 2 * B_HEADS, axis=1),
                                        jnp.where(lane == 3 * B_HEADS, 1.0, 0.0)))).astype(BF16)
    q_refs, k_refs, v_refs = (q0_ref, q1_ref), (k0_ref, k1_ref), (v0_ref, v1_ref)
    for h in range(B_HEADS):
        src_ref, pair = divmod(h // 2, 2)
        cs = slice(V7X_LANES * pair, V7X_LANES * (pair + 1))
        qp = q_refs[src_ref][:, cs].astype(F32)
        kp = k_refs[src_ref][:, cs].astype(F32)
        vp = v_refs[src_ref][:, cs].astype(F32)
        if h % 2 == 1:
            qp = pltpu.roll(qp, HEAD_DIM, axis=1)
            kp = pltpu.roll(kp, HEAD_DIM, axis=1)
        q_aug = jnp.dot(src, selq_ref[h], preferred_element_type=F32)
        k_aug = jnp.dot(src, selk_ref[h], preferred_element_type=F32)
        qa_ref[h] = jnp.where(low, qp * scale, q_aug).T.astype(BF16)
        ka_ref[h] = jnp.where(low, kp, k_aug).astype(BF16)
        v_aug = jnp.where(low, vp, 1.0) if h % 2 == 0 else jnp.where(low, 1.0, vp)
        va_ref[h, 0] = v_aug.T.astype(BF16)


def _fox_selectors():
    selq = np.zeros((B_HEADS, V7X_LANES, V7X_LANES), np.float32)
    selk = np.zeros((B_HEADS, V7X_LANES, V7X_LANES), np.float32)
    one = 3 * B_HEADS
    for h in range(B_HEADS):
        for part in range(3):
            selq[h, part * B_HEADS + h, HEAD_DIM + part] = 1.0
            selq[h, one, HEAD_DIM + 3 + part] = 1.0
            selk[h, one, HEAD_DIM + part] = 1.0
            selk[h, part * B_HEADS + h, HEAD_DIM + 3 + part] = -1.0
    return jnp.asarray(selq, BF16), jnp.asarray(selk, BF16)


def fox_prep(proj3, small3, f_bias):
    bsz, s, _ = proj3.shape
    t = FOX_PREP_ROWS
    half = B_WIDTH // 2
    fb = jnp.zeros((1, V7X_LANES), F32).at[0, SMALL_FL:SMALL_FL + B_HEADS].set(f_bias)

    def colspec(col):
        return pl.BlockSpec((None, t, half), lambda b, i, c=col // half: (b, i, c))

    assert t == FOX_TK
    selq, selk = _fox_selectors()
    sel_spec = pl.BlockSpec(selq.shape, lambda b, i: (0, 0, 0))
    k_spec = pl.BlockSpec((None, B_HEADS, t, V7X_LANES), lambda b, i: (b, 0, i, 0))
    k_shape = jax.ShapeDtypeStruct((bsz, B_HEADS, s, V7X_LANES), BF16)
    q_spec = pl.BlockSpec((None, B_HEADS, V7X_LANES, t), lambda b, i: (b, 0, 0, i))
    q_shape = jax.ShapeDtypeStruct((bsz, B_HEADS, V7X_LANES, s), BF16)
    v_spec = pl.BlockSpec((None, B_HEADS, 1, V7X_LANES, t), lambda b, i: (b, 0, i, 0, 0))
    v_shape = jax.ShapeDtypeStruct((bsz, B_HEADS, s // t, V7X_LANES, t), BF16)
    return pl.pallas_call(
        _fox_prep_kernel, grid=(bsz, s // t),
        in_specs=[pl.BlockSpec((None, t, V7X_LANES), lambda b, i: (b, i, 0)),
                  pl.BlockSpec((1, V7X_LANES), lambda b, i: (0, 0)),
                  sel_spec, sel_spec,
                  colspec(COL_QB), colspec(COL_QB + half),
                  colspec(COL_KB), colspec(COL_KB + half),
                  colspec(COL_VB), colspec(COL_VB + half)],
        out_specs=[q_spec, k_spec, v_spec],
        out_shape=[q_shape, k_shape, v_shape],
        scratch_shapes=[pltpu.VMEM((8, V7X_LANES), F32)],
        compiler_params=_cparams(("parallel", "arbitrary")), name="fox_prep",
    )(small3, fb, selq, selk, proj3, proj3, proj3, proj3, proj3, proj3)


def _fox_kernel(q_ref, k_ref, v_ref, o_ref, m_scr, acc_scr):
    tq, tk = FOX_TQ, FOX_TK
    qi = pl.program_id(2)
    nfull = qi * (tq // tk)
    krow = lax.broadcasted_iota(jnp.int32, (tk, tq), 0)
    qcol = lax.broadcasted_iota(jnp.int32, (tk, tq), 1)
    frow = lax.broadcasted_iota(jnp.int32, (V7X_LANES, tq), 0)
    nh = q_ref.shape[0]
    qs = [q_ref[e] for e in range(nh)]
    m_scr[...] = jnp.full(m_scr.shape, NEG, F32)
    acc_scr[...] = jnp.zeros(acc_scr.shape, F32)

    def heads(kt, mask):
        rows = pl.ds(pl.multiple_of(kt * tk, tk), tk)
        scores = [jnp.dot(k_ref[e, rows, :], qs[e], preferred_element_type=F32) for e in range(nh)]
        for e in range(nh):
            m = m_scr[e, 0:1, :]
            s = scores[e] if mask is None else jnp.where(mask, scores[e], NEG)
            m_new = jnp.maximum(m, jnp.max(s, axis=0, keepdims=True))
            alpha = jnp.exp2(m - m_new)
            p = jnp.exp2(s - m_new)
            pv = jnp.dot(v_ref[e, kt], p.astype(BF16), preferred_element_type=F32)
            acc_scr[e] = alpha * acc_scr[e] + pv
            m_scr[e] = jnp.broadcast_to(m_new, (8, tq))

    def body(kt, c):
        heads(kt, None)
        return c

    lax.fori_loop(0, nfull, body, 0)
    for dd in range(tq // tk):
        heads(nfull + dd, krow + dd * tk <= qcol)
    for g in range(nh // 2):
        even, odd = acc_scr[2 * g], acc_scr[2 * g + 1]
        res = [acc / pltpu.roll(acc, HEAD_DIM, axis=0) for acc in (even, odd)]
        pair = jnp.where(frow < HEAD_DIM, res[0], res[1])
        o_ref[:, V7X_LANES * g:V7X_LANES * (g + 1)] = pair.T.astype(o_ref.dtype)


def fox_attention(q_aug, k_aug, v_aug):
    bsz, nh, s, _ = k_aug.shape
    tq, tk, hs = FOX_TQ, FOX_TK, FOX_HEADS_PER_STEP
    return pl.pallas_call(
        _fox_kernel, grid=(bsz, nh // hs, s // tq),
        in_specs=[pl.BlockSpec((None, hs, V7X_LANES, tq), lambda b, g, i: (b, g, 0, i)),
                  pl.BlockSpec((None, hs, s, V7X_LANES), lambda b, g, i: (b, g, 0, 0)),
                  pl.BlockSpec((None, hs, s // tk, V7X_LANES, tk), lambda b, g, i: (b, g, 0, 0, 0))],
        out_specs=pl.BlockSpec((None, tq, hs * HEAD_DIM), lambda b, g, i: (b, i, g)),
        out_shape=jax.ShapeDtypeStruct((bsz, s, B_WIDTH), BF16),
        scratch_shapes=[pltpu.VMEM((hs, 8, tq), F32), pltpu.VMEM((hs, V7X_LANES, tq), F32)],
        compiler_params=_cparams(("parallel", "parallel", "arbitrary")), name="fox_attention",
    )(q_aug, k_aug, v_aug)


GLA_ROWS = 512
GLA_BATCH = 4


def _gla_chunk(rows, q_ref, k_ref, v_ref, r_ref, sm_bf, w2_ref, cb_ref, gn_ref, o_ref,
               st_scr, b_scr, k_scr, tri, row_sub, row_ck, col_ck):
    ck, sub = C_CHUNK, C_SUB
    nblk = ck // sub
    heads = range(q_ref.shape[0] * C_HEADS)
    bi = [h // C_HEADS for h in heads]
    hh = [h % C_HEADS for h in heads]
    nt = (((1,), (1,)), ((), ()))
    kcols = [slice(C_DK_PAD * hh[h], C_DK_PAD * (hh[h] + 1)) for h in heads]
    vcols = [slice(C_DV_PAD * hh[h], C_DV_PAD * (hh[h] + 1)) for h in heads]
    g = [jnp.dot(sm_bf[bi[h]], w2_ref[hh[h]], preferred_element_type=F32) + cb_ref[hh[h]] for h in heads]
    parts = [_split3(_log_sigmoid(g[h]) * (LOG2E / C_GATE_TAU)) for h in heads]
    b = [(jnp.dot(tri, parts[h][0], preferred_element_type=F32)
          + jnp.dot(tri, parts[h][1], preferred_element_type=F32)
          + jnp.dot(tri, parts[h][2], preferred_element_type=F32)) for h in heads]
    qf = [q_ref[bi[h], rows, kcols[h]].astype(F32) * (C_DK ** -0.5) for h in heads]
    kf = [k_ref[bi[h], rows, kcols[h]].astype(F32) for h in heads]
    vb = [v_ref[bi[h], rows, vcols[h]] for h in heads]
    st = [st_scr[h] for h in heads]
    for h in heads:
        b_scr[h] = b[h]
        k_scr[h] = kf[h]
    o_inter = [lax.dot_general((qf[h] * jnp.exp2(b[h])).astype(BF16), st[h].astype(BF16), nt,
                               preferred_element_type=F32) for h in heads]
    a = {}
    for h in heads:
        for blk in range(1, nblk):
            r0 = blk * sub
            ref_row = b_scr[h, r0 - 1:r0, :]
            kd = jnp.where(row_ck < r0, kf[h] * jnp.exp2(jnp.minimum(ref_row - b[h], 0.0)), 0.0)
            qd = qf[h][r0:r0 + sub, :] * jnp.exp2(b[h][r0:r0 + sub, :] - ref_row)
            a[h, blk] = lax.dot_general(qd.astype(BF16), kd.astype(BF16), nt, preferred_element_type=F32)
    kdec = [(kf[h] * jnp.exp2(b_scr[h, ck - 1:ck, :] - b[h])).astype(BF16) for h in heads]
    upd = [lax.dot_general(vb[h], kdec[h], (((0,), (0,)), ((), ())), preferred_element_type=F32)
           for h in heads]
    for h in heads:
        st_scr[h] = st[h] * jnp.exp2(b_scr[h, ck - 1:ck, :]) + upd[h]
    for h in heads:
        outs = []
        for blk in range(nblk):
            r0 = blk * sub
            qb = qf[h][r0:r0 + sub, :]
            bb = b[h][r0:r0 + sub, :]
            scores = a[h, blk] if blk > 0 else jnp.zeros((sub, ck), F32)
            for j in range(sub):
                bj = b_scr[h, r0 + j:r0 + j + 1, :]
                kj = k_scr[h, r0 + j:r0 + j + 1, :]
                x = qb * kj * jnp.exp2(bb - bj)
                colv = jnp.sum(x, axis=-1, keepdims=True)
                scores = jnp.where(jnp.logical_and(col_ck == r0 + j, row_sub >= j), colv, scores)
            outs.append(o_inter[h][r0:r0 + sub, :]
                        + jnp.dot(scores.astype(BF16), vb[h], preferred_element_type=F32))
        o = jnp.concatenate(outs, axis=0)
        ms = jnp.sum(o * o, axis=-1, keepdims=True) * (1.0 / C_DV)
        on = o * lax.rsqrt(ms + EPS) * gn_ref[...]
        rr = r_ref[bi[h], rows, vcols[h]].astype(F32)
        o_ref[bi[h], rows, vcols[h]] = (on * (rr / (1.0 + jnp.exp(-rr)))).astype(o_ref.dtype)


def _gla_kernel(q_ref, k_ref, v_ref, r_ref, sm_ref, w2_ref, cb_ref, gn_ref, o_ref,
                st_scr, b_scr, k_scr):
    ck, sub = C_CHUNK, C_SUB

    @pl.when(pl.program_id(1) == 0)
    def _():
        st_scr[...] = jnp.zeros_like(st_scr)

    tri = _tril_ones(ck, BF16)
    row_sub = lax.broadcasted_iota(jnp.int32, (sub, 1), 0)
    row_ck = lax.broadcasted_iota(jnp.int32, (ck, 1), 0)
    col_ck = lax.broadcasted_iota(jnp.int32, (sub, ck), 1)

    def chunk(c, carry):
        rows = pl.ds(pl.multiple_of(c * ck, ck), ck)
        sm_bf = [sm_ref[i, rows, :].astype(BF16) for i in range(sm_ref.shape[0])]
        _gla_chunk(rows, q_ref, k_ref, v_ref, r_ref, sm_bf, w2_ref, cb_ref, gn_ref, o_ref,
                   st_scr, b_scr, k_scr, tri, row_sub, row_ck, col_ck)
        return carry

    lax.fori_loop(0, q_ref.shape[1] // ck, chunk, 0)


def gla_mixer(proj3, small3, w2p, cbp, gnp):
    bsz, s, _ = proj3.shape
    t = GLA_ROWS
    kw, vw = C_HEADS * C_DK_PAD, C_HEADS * C_DV_PAD

    nb = math.gcd(bsz, GLA_BATCH)

    def col(base, width):
        return pl.BlockSpec((nb, t, width), lambda b, i, c=base // width: (b, i, c))

    return pl.pallas_call(
        _gla_kernel, grid=(bsz // nb, s // t),
        in_specs=[col(COL_QC, kw), col(COL_KC, kw), col(COL_VC, vw), col(COL_RC, vw),
                  pl.BlockSpec((nb, t, V7X_LANES), lambda b, i: (b, i, 0)),
                  pl.BlockSpec((C_HEADS, V7X_LANES, C_DK_PAD), lambda b, i: (0, 0, 0)),
                  pl.BlockSpec((C_HEADS, 1, C_DK_PAD), lambda b, i: (0, 0, 0)),
                  pl.BlockSpec((1, C_DV_PAD), lambda b, i: (0, 0))],
        out_specs=pl.BlockSpec((nb, t, vw), lambda b, i: (b, i, 0)),
        out_shape=jax.ShapeDtypeStruct((bsz, s, vw), BF16),
        scratch_shapes=[pltpu.VMEM((nb * C_HEADS, C_DV_PAD, C_DK_PAD), F32),
                        pltpu.VMEM((nb * C_HEADS, C_CHUNK, C_DK_PAD), F32),
                        pltpu.VMEM((nb * C_HEADS, C_CHUNK, C_DK_PAD), F32)],
        compiler_params=_cparams(("parallel", "arbitrary")), name="gla_mixer",
    )(proj3, proj3, proj3, proj3, small3, w2p, cbp, gnp)


MIX_OUT_ROWS = 512


def _mix_out_cross_kernel(x_ref, a_ref, b_ref, c_ref, wa_ref, wb_ref, wc_ref,
                          g_ref, wq_ref, k_ref, v_ref, wo_ref, o_ref):
    acc = jnp.dot(a_ref[...], wa_ref[...], preferred_element_type=F32)
    acc += jnp.dot(b_ref[...], wb_ref[...], preferred_element_type=F32)
    acc += jnp.dot(c_ref[...], wc_ref[...], preferred_element_type=F32)
    x = x_ref[...] + acc
    h = _rms_rows(x, g_ref[...]).astype(BF16)
    q = jnp.dot(h, wq_ref[...], preferred_element_type=F32).astype(BF16)
    cols = [slice(CROSS_DH * hd, CROSS_DH * (hd + 1)) for hd in range(CROSS_HEADS)]
    scores = [lax.dot_general(q[:, cs], k_ref[:, cs], (((1,), (1,)), ((), ())),
                              preferred_element_type=F32) * (CROSS_DH ** -0.5) for cs in cols]
    heads = []
    for cs, logits in zip(cols, scores):
        m = jnp.max(logits, axis=-1, keepdims=True)
        p = jnp.exp(logits - m)
        ssum = jnp.sum(p, axis=-1, keepdims=True)
        pv = jnp.dot(p.astype(BF16), v_ref[:, cs], preferred_element_type=F32)
        heads.append((pv / ssum).astype(BF16))
    o = jnp.concatenate(heads, axis=1)
    o_ref[...] = x + jnp.dot(o, wo_ref[...], preferred_element_type=F32)


def mix_out_cross(x2d, oa, ob, oc, wa, wb, wc, gain, wq, kv, wo, *, seq):
    m, d = x2d.shape
    tm = MIX_OUT_ROWS
    mem_len = kv.shape[1]
    per_batch = seq // tm

    def rows(a):
        return pl.BlockSpec((tm, a.shape[1]), lambda i: (i, 0))

    def resident(a):
        return pl.BlockSpec(a.shape, lambda i: (0,) * a.ndim, pipeline_mode=pl.Buffered(1))

    g2 = gain.reshape(1, d)
    return pl.pallas_call(
        _mix_out_cross_kernel, grid=(m // tm,),
        in_specs=[rows(x2d), rows(oa), rows(ob), rows(oc), resident(wa), resident(wb), resident(wc),
                  resident(g2), resident(wq),
                  pl.BlockSpec((None, mem_len, CROSS_WIDTH), lambda i: (i // per_batch, 0, 0)),
                  pl.BlockSpec((None, mem_len, CROSS_WIDTH), lambda i: (i // per_batch, 0, 1)),
                  resident(wo)],
        out_specs=pl.BlockSpec((tm, d), lambda i: (i, 0)),
        out_shape=jax.ShapeDtypeStruct((m, d), F32),
        compiler_params=_cparams(("parallel",)), name="mix_out_cross",
    )(x2d, oa, ob, oc, wa, wb, wc, g2, wq, kv, kv, wo)


FFN_ROWS = 1024
FFN_HALO = 16


def _ffn_kernel(x_ref, xh_ref, g_ref, wv_ref, wg_ref, cw_ref, cb_ref, wd_ref, gf_ref, o_ref,
                h_scr, hh_scr, *, final, tiles_per_seq):
    j = pl.program_id(1)

    @pl.when(j == 0)
    def _():
        _norm_to_scratch(x_ref, g_ref, h_scr)
        keep = pl.program_id(0) % tiles_per_seq != 0
        hh_scr[...] = jnp.where(keep, _rms_rows(xh_ref[...], g_ref[...]), 0.0).astype(BF16)
        o_ref[...] = x_ref[...]

    h = h_scr[...]
    halo = jnp.dot(hh_scr[...], wg_ref[...], preferred_element_type=F32)
    gate = jnp.dot(h, wg_ref[...], preferred_element_type=F32)
    val = jnp.dot(h, wv_ref[...], preferred_element_type=F32)
    row = lax.broadcasted_iota(jnp.int32, gate.shape, 0)
    prev1 = halo[FFN_HALO - 1:, :]
    prev2 = halo[FFN_HALO - 2:FFN_HALO - 1, :]
    g1 = jnp.where(row == 0, prev1, pltpu.roll(gate, 1, axis=0))
    g2 = jnp.where(row == 0, prev2, jnp.where(row == 1, prev1, pltpu.roll(gate, 2, axis=0)))
    gc = cb_ref[...] + cw_ref[0:1, :] * g2
    gc = gc + cw_ref[1:2, :] * g1
    gc = gc + cw_ref[2:3, :] * gate
    act = (gc / (1.0 + jnp.exp(-gc)) * val).astype(BF16)
    o_ref[...] += jnp.dot(act, wd_ref[...], preferred_element_type=F32)

    if final:
        @pl.when(j == pl.num_programs(1) - 1)
        def _():
            def body(r, c):
                sl = pl.ds(pl.multiple_of(r * NORM_ROWS, NORM_ROWS), NORM_ROWS)
                o_ref[sl, :] = _rms_rows(o_ref[sl, :], gf_ref[...])
                return c
            lax.fori_loop(0, o_ref.shape[0] // NORM_ROWS, body, 0)


def conv_ffn(x2d, gain, wv, wg, cw, cb, wd, gain_final, *, seq, final):
    m, d = x2d.shape
    tm, tf = FFN_ROWS, FF_TILE
    nt = m // tm
    ff = wv.shape[1]
    halo_blocks = tm // FFN_HALO
    return pl.pallas_call(
        functools.partial(_ffn_kernel, final=final, tiles_per_seq=seq // tm), grid=(nt, ff // tf),
        in_specs=[pl.BlockSpec((tm, d), lambda i, j: (i, 0)),
                  pl.BlockSpec((FFN_HALO, d), lambda i, j: (jnp.maximum(i * halo_blocks - 1, 0), 0)),
                  pl.BlockSpec((1, d), lambda i, j: (0, 0)),
                  pl.BlockSpec((d, tf), lambda i, j: (0, j)),
                  pl.BlockSpec((d, tf), lambda i, j: (0, j)),
                  pl.BlockSpec((CONV_W, tf), lambda i, j: (0, j)),
                  pl.BlockSpec((1, tf), lambda i, j: (0, j)),
                  pl.BlockSpec((tf, d), lambda i, j: (j, 0)),
                  pl.BlockSpec((1, d), lambda i, j: (0, 0))],
        out_specs=pl.BlockSpec((tm, d), lambda i, j: (i, 0)),
        out_shape=jax.ShapeDtypeStruct((m, d), F32),
        scratch_shapes=[pltpu.VMEM((tm, d), BF16), pltpu.VMEM((FFN_HALO, d), BF16)],
        compiler_params=_cparams(("parallel", "arbitrary")), name="conv_ffn",
    )(x2d, x2d, gain.reshape(1, d), wv, wg, cw, cb.reshape(1, ff), wd, gain_final.reshape(1, d))


def _pad_heads(w, heads, width, padded):
    lead = w.shape[:-1]
    w = w.reshape(lead + (heads, width))
    w = jnp.pad(w, [(0, 0)] * len(lead) + [(0, 0), (0, padded - width)])
    return w.reshape(lead + (heads * padded,))


def _layer_params(l, w_in, c_gate_w2, c_gate_b, c_norm, w_out, w_up, conv_w, conv_b, w_down):
    splits = np.cumsum([A_WIDTH] * 3 + [B_WIDTH] * 3 + [B_HEADS, C_KW, C_KW, C_VW, C_VW, C_GATE_RANK])
    w_in_l = w_in[l].astype(BF16)
    fl, qc, kc, vc, rc, gl = jnp.split(w_in_l[:, splits[5]:], (splits[6:11] - splits[5]).tolist(), axis=1)
    w_all = jnp.concatenate([
        _pad_heads(vc, C_HEADS, C_DV, C_DV_PAD), _pad_heads(rc, C_HEADS, C_DV, C_DV_PAD),
        _pad_heads(qc, C_HEADS, C_DK, C_DK_PAD), _pad_heads(kc, C_HEADS, C_DK, C_DK_PAD),
        w_in_l[:, :splits[5]]],
        axis=1)
    w_small = jnp.zeros((D_MODEL, V7X_LANES), BF16)
    w_small = w_small.at[:, SMALL_FL:SMALL_FL + B_HEADS].set(fl)
    w_small = w_small.at[:, SMALL_GL:SMALL_GL + C_GATE_RANK].set(gl)
    w2 = _pad_heads(c_gate_w2[l], C_HEADS, C_DK, C_DK_PAD).reshape(C_GATE_RANK, C_HEADS, C_DK_PAD)
    w2p = jnp.zeros((C_HEADS, V7X_LANES, C_DK_PAD), F32)
    w2p = w2p.at[:, SMALL_GL:SMALL_GL + C_GATE_RANK, :].set(jnp.transpose(w2, (1, 0, 2))).astype(BF16)
    cbp = _pad_heads(c_gate_b[l], C_HEADS, C_DK, C_DK_PAD).reshape(C_HEADS, 1, C_DK_PAD)
    gnp = jnp.pad(c_norm[l], (0, C_DV_PAD - C_DV)).reshape(1, C_DV_PAD)
    wo = w_out[l].astype(BF16)
    wa = wo[:A_WIDTH]
    wb = wo[A_WIDTH:A_WIDTH + B_WIDTH]
    wc = wo[A_WIDTH + B_WIDTH:].reshape(C_HEADS, C_DV, D_MODEL)
    wc = jnp.pad(wc, ((0, 0), (0, C_DV_PAD - C_DV), (0, 0))).reshape(C_HEADS * C_DV_PAD, D_MODEL)
    fpad = D_FF_PAD - D_FF
    w_up_l = w_up[l].astype(BF16)
    wv = jnp.pad(w_up_l[:, :D_FF], ((0, 0), (0, fpad)))
    wg = jnp.pad(w_up_l[:, D_FF:], ((0, 0), (0, fpad)))
    cw = jnp.pad(conv_w[l], ((0, 0), (0, fpad)))
    cb = jnp.pad(conv_b[l], (0, fpad))
    wd = jnp.pad(w_down[l].astype(BF16), ((0, fpad), (0, 0)))
    return dict(w_all=w_all, w_small=w_small, w2p=w2p, cbp=cbp, gnp=gnp,
                wa=wa, wb=wb, wc=wc, wv=wv, wg=wg, cw=cw, cb=cb, wd=wd)


def hybrid_mixer(x2d, gain, p, f_bias, bias_a, *, bsz, seq):
    proj, small = in_proj(x2d, gain, p["w_all"], p["w_small"])
    proj3 = proj.reshape(bsz, seq, PROJ_PAD)
    small3 = small.reshape(bsz, seq, V7X_LANES)
    o_a = dilated_mixture(proj3, bias_a).reshape(bsz * seq, A_WIDTH)
    o_b = fox_attention(*fox_prep(proj3, small3, f_bias)).reshape(bsz * seq, B_WIDTH)
    o_c = gla_mixer(proj3, small3, p["w2p"], p["cbp"], p["gnp"]).reshape(bsz * seq, C_HEADS * C_DV_PAD)
    return o_a, o_b, o_c


def kernel(x, mem, rel_table, mem_norm, norm_final, norm_mix, w_in, f_bias, c_gate_w2, c_gate_b,
           c_norm, w_out, norm_cross, w_cq, w_ckv, w_co, norm_ffn, w_up, conv_w, conv_b, w_down):
    bsz, seq, d = x.shape
    depth = w_in.shape[0]
    mem_len = mem.shape[1]
    assert d == D_MODEL and seq % (A_SPAN * A_PATTERNS[-1][1]) == 0 and seq % 1024 == 0
    bias_a = jnp.stack([_dilated_bias(rel_table, dil) for _, dil in A_PATTERNS], axis=0)
    x2d = x.reshape(bsz * seq, d)
    mem2d = mem.reshape(bsz * mem_len, d)
    for l in range(depth):
        p = _layer_params(l, w_in, c_gate_w2, c_gate_b, c_norm, w_out, w_up, conv_w, conv_b, w_down)
        o_a, o_b, o_c = hybrid_mixer(x2d, norm_mix[l], p, f_bias[l], bias_a, bsz=bsz, seq=seq)
        kv = norm_matmul(mem2d, mem_norm, w_ckv[l].astype(BF16), tm=min(512, bsz * mem_len), tn=512,
                         name="mem_kv").reshape(bsz, mem_len, 2 * CROSS_WIDTH)
        x2d = mix_out_cross(x2d, o_a, o_b, o_c, p["wa"], p["wb"], p["wc"], norm_cross[l],
                            w_cq[l].astype(BF16), kv, w_co[l].astype(BF16), seq=seq)
        x2d = conv_ffn(x2d, norm_ffn[l], p["wv"], p["wg"], p["cw"], p["cb"], p["wd"], norm_final,
                       seq=seq, final=l == depth - 1)
    return x2d.reshape(bsz, seq, d)
```
